```python
import math
import jax, jax.numpy as jnp
from jax import lax
import numpy as np

D_MODEL = 1024
BATCH = 16
SEQ = 4096
DEPTH = 2

MEM_LEN = 256
Q_BLOCK = 128
NSA_Q_BLOCK = 32
N_BUCKETS = 32
MAX_DISTANCE = 128
RMS_EPS = 1e-6
ROPE_THETA = 10000.0
NEG_INF = -1e30
D_FF = 2816
A_HEADS = 8
A_KV = 2
A_GRP = A_HEADS // A_KV
A_DH = 64
CMP_LEN = 32
CMP_STRIDE = 16
SLC_LEN = 64
SLC_TOP = 16
WIN = 512
FORCE_BONUS = 1e3
B_HEADS = 8
B_DH = 64
IDX_HEADS = 8
IDX_DH = 32
IDX_TOPK_MAX = 256
C_HEADS = 8
C_Q_RANK = 256
C_KV_RANK = 128
C_DN = 64
C_DR = 32
C_DV = 64
D_HEADS = 4
D_DH = 64
D_DV = 128
X_HEADS = 4
X_DH = 128
N_BIAS_HEADS = A_HEADS + B_HEADS + D_HEADS

EV_SIZES = (A_HEADS * A_DH, 6 * A_KV * A_DH, 3 * A_HEADS,
            B_HEADS * B_DH, B_DH, B_DH, IDX_HEADS * IDX_DH, IDX_DH, IDX_HEADS)
EV_IN = sum(EV_SIZES)
EV_OUT = A_HEADS * A_DH + B_HEADS * B_DH
OD_SIZES = (C_Q_RANK, C_KV_RANK, C_DR, D_HEADS * 2 * D_DH, D_HEADS * 2 * D_DH, D_HEADS * D_DV)
OD_IN = sum(OD_SIZES)
OD_OUT = C_HEADS * C_DV + D_HEADS * D_DV

kernel_name = 'hybrid_nsa_dsa_mla_diff_macaron_block'


def rmsnorm(x, g):
    x32 = x.astype(jnp.float32)
    y = x32 * lax.rsqrt(jnp.mean(x32 * x32, axis=-1, keepdims=True) + RMS_EPS)
    return (y * g.astype(jnp.float32)).astype(x.dtype)


def swiglu(h, wg, wu, wd):
    return (jax.nn.silu(h @ wg) * (h @ wu)) @ wd


def split_cols(p, sizes):
    return jnp.split(p, [int(c) for c in np.cumsum(sizes)[:-1]], axis=-1)


def masked_softmax(logits, mask):
    p = jax.nn.softmax(jnp.where(mask, logits, NEG_INF), axis=-1)
    return p * mask


def rel_bucket(dist):
    max_exact = N_BUCKETS // 2
    d = jnp.maximum(dist, 0)
    log_ratio = jnp.log(jnp.maximum(d, 1).astype(jnp.float32) / max_exact) / math.log(MAX_DISTANCE / max_exact)
    large = jnp.minimum(max_exact + (log_ratio * (N_BUCKETS - max_exact)).astype(jnp.int32), N_BUCKETS - 1)
    return jnp.where(d < max_exact, d, large)


def rel_bias(dist, tab):
    b = tab.astype(jnp.float32)[rel_bucket(dist)]
    nh = tab.ndim - 1
    return jnp.moveaxis(b, tuple(range(dist.ndim, dist.ndim + nh)), tuple(range(nh)))


def rope(x):
    S = x.shape[1]
    half = x.shape[-1] // 2
    inv = ROPE_THETA ** (-jnp.arange(half, dtype=jnp.float32) / half)
    ang = jnp.arange(S, dtype=jnp.float32)[:, None] * inv[None, :]
    shape = (S,) + (1,) * (x.ndim - 3) + (half,)
    cos, sin = jnp.cos(ang).reshape(shape), jnp.sin(ang).reshape(shape)
    x1 = x[..., :half].astype(jnp.float32)
    x2 = x[..., half:].astype(jnp.float32)
    return jnp.concatenate([x1 * cos - x2 * sin, x1 * sin + x2 * cos], axis=-1).astype(x.dtype)


def blockwise(fn, blk, *q_side):
    B, S = q_side[0].shape[:2]
    n = S // blk
    xs = tuple(jnp.moveaxis(a.reshape((B, n, blk) + a.shape[2:]), 1, 0) for a in q_side)
    out = lax.map(lambda a: fn(a[0], *a[1:]), (jnp.arange(n, dtype=jnp.int32),) + xs)
    return jnp.moveaxis(out, 0, 1).reshape((B, S) + out.shape[3:])


def gather_rows(a, idx):
    return jax.vmap(lambda ai, ii: ai[ii])(a, idx)


def nsa_mixer(q, kv6, gate_logits, tab, cmp_pe, cmp_w):
    B, S = q.shape[:2]
    q = q.reshape(B, S, A_KV, A_GRP, A_DH)
    gate_logits = gate_logits.reshape(B, S, A_KV, A_GRP, 3)
    kv6 = kv6.reshape(B, S, 6, A_KV, A_DH)
    k_c, v_c, k_s, v_s, k_w, v_w = (kv6[:, :, i] for i in range(6))
    tab_g = tab.reshape(N_BUCKETS, A_KV, A_GRP)
    scale = A_DH ** -0.5
    n_cmp = (S - CMP_LEN) // CMP_STRIDE + 1
    cmp_start = jnp.arange(n_cmp) * CMP_STRIDE
    cmp_end = cmp_start + CMP_LEN - 1
    tok = cmp_start[:, None] + jnp.arange(CMP_LEN)[None, :]
    k_cmp = jnp.einsum('bnlgd,lde->bnge', k_c[:, tok] + cmp_pe[0][:, None, :], cmp_w[0])
    v_cmp = jnp.einsum('bnlgd,lde->bnge', v_c[:, tok] + cmp_pe[1][:, None, :], cmp_w[1])
    n_sb = S // SLC_LEN
    n_top = min(SLC_TOP, n_sb)
    sb_start = jnp.arange(n_sb) * SLC_LEN
    overlap = ((cmp_start[:, None] < sb_start[None, :] + SLC_LEN)
               & (cmp_end[:, None] >= sb_start[None, :])).astype(jnp.float32)
    k_sb = k_s.reshape(B, n_sb, SLC_LEN, A_KV, A_DH).transpose(0, 3, 1, 2, 4)
    v_sb = v_s.reshape(B, n_sb, SLC_LEN, A_KV, A_DH).transpose(0, 3, 1, 2, 4)
    k_wp = jnp.pad(k_w, ((0, 0), (WIN, 0), (0, 0), (0, 0)))
    v_wp = jnp.pad(v_w, ((0, 0), (WIN, 0), (0, 0), (0, 0)))
    blk_ids = jnp.arange(n_sb)
    g_ids = jnp.arange(A_KV)[:, None, None]

    def block(qi, qb, gb):
        T = qb.shape[1]
        t = qi * T + jnp.arange(T)
        s_c = (jnp.einsum('btgrd,bngd->bgrtn', qb, k_cmp).astype(jnp.float32) * scale
               + rel_bias(t[:, None] - cmp_end[None, :], tab_g))
        p_c = masked_softmax(s_c, cmp_end[None, :] <= t[:, None])
        o_c = jnp.einsum('bgrtn,bngd->btgrd', p_c.astype(v_cmp.dtype), v_cmp)
        imp = jnp.einsum('bgrtn,nj->bgtj', p_c, overlap)
        cur = (t // SLC_LEN)[:, None]
        forced = (blk_ids == 0) | (blk_ids == cur) | (blk_ids == cur - 1)
        score = jnp.where(blk_ids <= cur, imp + FORCE_BONUS * forced, NEG_INF)
        _, sel = lax.top_k(score, n_top)
        k_sel = jax.vmap(gather_rows)(k_sb, sel).reshape(B, A_KV, T, n_top * SLC_LEN, A_DH)
        v_sel = jax.vmap(gather_rows)(v_sb, sel).reshape(B, A_KV, T, n_top * SLC_LEN, A_DH)
        pos = (sel[..., None] * SLC_LEN + jnp.arange(SLC_LEN)).reshape(B, A_KV, T, n_top * SLC_LEN)
        dist = t[:, None] - pos
        bias_s = jnp.moveaxis(tab_g.astype(jnp.float32)[rel_bucket(dist), g_ids], -1, 2)
        s_s = jnp.einsum('btgrd,bgtkd->bgrtk', qb, k_sel).astype(jnp.float32) * scale + bias_s
        p_s = masked_softmax(s_s, (dist >= 0)[:, :, None])
        o_s = jnp.einsum('bgrtk,bgtkd->btgrd', p_s.astype(v_sel.dtype), v_sel)
        k_win = lax.dynamic_slice_in_dim(k_wp, qi * T, T + WIN, axis=1)
        v_win = lax.dynamic_slice_in_dim(v_wp, qi * T, T + WIN, axis=1)
        s_pos = qi * T - WIN + jnp.arange(T + WIN)
        dist_w = t[:, None] - s_pos[None, :]
        vis_w = (dist_w >= 0) & (dist_w < WIN) & (s_pos[None, :] >= 0)
        s_w = (jnp.einsum('btgrd,bsgd->bgrts', qb, k_win).astype(jnp.float32) * scale
               + rel_bias(dist_w, tab_g))
        p_w = masked_softmax(s_w, vis_w)
        o_w = jnp.einsum('bgrts,bsgd->btgrd', p_w.astype(v_win.dtype), v_win)
        g = jax.nn.sigmoid(gb.astype(jnp.float32))
        o = g[..., 0:1] * o_c + g[..., 1:2] * o_s + g[..., 2:3] * o_w
        return o.astype(qb.dtype)

    out = blockwise(block, NSA_Q_BLOCK, q, gate_logits)
    return out.reshape(B, S, A_HEADS * A_DH)


def dsa_mixer(q, k, v, iq, ik, iw, tab):
    B, S = q.shape[:2]
    k_top = min(IDX_TOPK_MAX, S // 4)
    q = q.reshape(B, S, B_HEADS, B_DH)
    iq = iq.reshape(B, S, IDX_HEADS, IDX_DH)
    scale = B_DH ** -0.5
    key_pos = jnp.arange(S)

    def block(qi, qb, iqb, iwb):
        T = qb.shape[1]
        t = qi * T + jnp.arange(T)
        idx_logit = jax.nn.relu(jnp.einsum('bthd,bsd->bths', iqb, ik).astype(jnp.float32))
        idx_score = jnp.einsum('bth,bths->bts', iwb.astype(jnp.float32), idx_logit)
        idx_score = jnp.where(key_pos[None, :] <= t[:, None], idx_score, NEG_INF)
        _, sel = lax.top_k(idx_score, k_top)
        k_sel = gather_rows(k, sel)
        v_sel = gather_rows(v, sel)
        dist = t[:, None] - sel
        s = (jnp.einsum('bthd,btkd->bhtk', qb, k_sel).astype(jnp.float32) * scale
             + jnp.moveaxis(tab.astype(jnp.float32)[rel_bucket(dist)], -1, 1))
        p = masked_softmax(s, (dist >= 0)[:, None])
        o = jnp.einsum('bhtk,btkd->bthd', p.astype(v_sel.dtype), v_sel)
        return o.reshape(B, T, B_HEADS * B_DH)

    return blockwise(block, Q_BLOCK, q, iq, iw)


def mla_mixer(c_q, c_kv, k_rope, q_norm, kv_norm, w_uq, w_ukv):
    B, S = c_q.shape[:2]
    q = (rmsnorm(c_q, q_norm) @ w_uq).reshape(B, S, C_HEADS, C_DN + C_DR)
    q_nope, q_rope = q[..., :C_DN], rope(q[..., C_DN:])
    kv = (rmsnorm(c_kv, kv_norm) @ w_ukv).reshape(B, S, C_HEADS, C_DN + C_DV)
    k_nope, v = kv[..., :C_DN], kv[..., C_DN:]
    k_r = rope(k_rope)
    scale = (C_DN + C_DR) ** -0.5
    key_pos = jnp.arange(S)

    def block(qi, qn, qr):
        T = qn.shape[1]
        t = qi * T + jnp.arange(T)
        s = (jnp.einsum('bthd,bshd->bhts', qn, k_nope)
             + jnp.einsum('bthd,bsd->bhts', qr, k_r)).astype(jnp.float32) * scale
        p = masked_softmax(s, key_pos[None, :] <= t[:, None])
        o = jnp.einsum('bhts,bshd->bthd', p.astype(v.dtype), v)
        return o.reshape(B, T, C_HEADS * C_DV)

    return blockwise(block, Q_BLOCK, q_nope, q_rope)


def diff_mixer(q, k, v, lam, subln, tab, lam_init):
    B, S = q.shape[:2]
    q = q.reshape(B, S, D_HEADS, 2, D_DH)
    k = k.reshape(B, S, D_HEADS, 2, D_DH)
    v = v.reshape(B, S, D_HEADS, D_DV)
    lam32 = lam.astype(jnp.float32)
    lam_full = jnp.exp(jnp.sum(lam32[0] * lam32[1])) - jnp.exp(jnp.sum(lam32[2] * lam32[3])) + lam_init
    scale = D_DH ** -0.5
    key_pos = jnp.arange(S)

    def block(qi, qb):
        T = qb.shape[1]
        t = qi * T + jnp.arange(T)
        dist = t[:, None] - key_pos[None, :]
        s = jnp.einsum('bthed,bshed->behts', qb, k).astype(jnp.float32) * scale + rel_bias(dist, tab)
        p = masked_softmax(s, dist >= 0)
        a = p[:, 0] - lam_full * p[:, 1]
        o = jnp.einsum('bhts,bshd->bthd', a.astype(v.dtype), v)
        o = rmsnorm(o, subln) * (1.0 - lam_init)
        return o.reshape(B, T, D_HEADS * D_DV)

    return blockwise(block, Q_BLOCK, q)


def mem_xattn(h, m, wq, wkv, wo):
    B, S = h.shape[:2]
    q = (h @ wq).reshape(B, S, X_HEADS, X_DH)
    kv = (m @ wkv).reshape(B, m.shape[1], 2, X_HEADS, X_DH)
    k, v = kv[:, :, 0], kv[:, :, 1]
    s = jnp.einsum('bshd,bmhd->bhsm', q, k).astype(jnp.float32) * X_DH ** -0.5
    p = jax.nn.softmax(s, axis=-1)
    o = jnp.einsum('bhsm,bmhd->bshd', p.astype(v.dtype), v)
    return o.reshape(B, S, X_HEADS * X_DH) @ wo


def even_mixer(h, w_in, w_out, cmp_pe, cmp_w, rel_tab):
    qa, kva, ga, qb, kb, vb, iq, ik, iw = split_cols(h @ w_in, EV_SIZES)
    o_a = nsa_mixer(qa, kva, ga, rel_tab[:, :A_HEADS], cmp_pe, cmp_w)
    o_b = dsa_mixer(qb, kb, vb, iq, ik, iw, rel_tab[:, A_HEADS:A_HEADS + B_HEADS])
    return jnp.concatenate([o_a, o_b], axis=-1) @ w_out


def odd_mixer(h, w_in, w_out, q_norm, kv_norm, w_uq, w_ukv, lam, subln, rel_tab, lam_init):
    c_q, c_kv, k_rope, qd, kd, vd = split_cols(h @ w_in, OD_SIZES)
    o_c = mla_mixer(c_q, c_kv, k_rope, q_norm, kv_norm, w_uq, w_ukv)
    o_d = diff_mixer(qd, kd, vd, lam, subln, rel_tab[:, A_HEADS + B_HEADS:], lam_init)
    return jnp.concatenate([o_c, o_d], axis=-1) @ w_out


def setup_inputs(seed: int = 0) -> dict:
    key = jax.random.key(seed)
    ks = jax.random.split(key, 24)
    n_even = (DEPTH + 1) // 2
    n_odd = DEPTH // 2
    f32 = jnp.float32

    def dense(k, shape, fan_in):
        return jax.random.normal(k, shape, f32) * fan_in ** -0.5

    def gain(k, shape):
        return 1.0 + 0.05 * jax.random.normal(k, shape, f32)

    return {
        'x': jax.random.normal(ks[0], (BATCH, SEQ, D_MODEL), f32),
        'mem': jax.random.normal(ks[1], (BATCH, MEM_LEN, D_MODEL), f32),
        'rel_bias_table': 0.2 * jax.random.normal(ks[2], (N_BUCKETS, N_BIAS_HEADS), f32),
        'norm_pre': gain(ks[3], (DEPTH, 4, D_MODEL)),
        'norm_post': gain(ks[4], (DEPTH, 4, D_MODEL)),
        'mem_norm': gain(ks[5], (DEPTH, D_MODEL)),
        'ffn_wg': dense(ks[6], (DEPTH, 2, D_MODEL, D_FF), D_MODEL),
        'ffn_wu': dense(ks[7], (DEPTH, 2, D_MODEL, D_FF), D_MODEL),
        'ffn_wd': dense(ks[8], (DEPTH, 2, D_FF, D_MODEL), D_FF),
        'ev_w_in': dense(ks[9], (n_even, D_MODEL, EV_IN), D_MODEL),
        'ev_cmp_pe': 0.5 * jax.random.normal(ks[10], (n_even, 2, CMP_LEN, A_DH), f32),
        'ev_cmp_w': dense(ks[11], (n_even, 2, CMP_LEN, A_DH, A_DH), CMP_LEN * A_DH),
        'ev_w_out': dense(ks[12], (n_even, EV_OUT, D_MODEL), EV_OUT),
        'od_w_in': dense(ks[13], (n_odd, D_MODEL, OD_IN), D_MODEL),
        'od_q_norm': gain(ks[14], (n_odd, C_Q_RANK)),
        'od_kv_norm': gain(ks[15], (n_odd, C_KV_RANK)),
        'od_w_uq': dense(ks[16], (n_odd, C_Q_RANK, C_HEADS * (C_DN + C_DR)), C_Q_RANK),
        'od_w_ukv': dense(ks[17], (n_odd, C_KV_RANK, C_HEADS * (C_DN + C_DV)), C_KV_RANK),
        'od_lambda': 0.1 * jax.random.normal(ks[18], (n_odd, 4, D_DH), f32),
        'od_subln': gain(ks[19], (n_odd, D_DV)),
        'od_w_out': dense(ks[20], (n_odd, OD_OUT, D_MODEL), OD_OUT),
        'xa_wq': dense(ks[21], (DEPTH, D_MODEL, X_HEADS * X_DH), D_MODEL),
        'xa_wkv': dense(ks[22], (DEPTH, D_MODEL, 2 * X_HEADS * X_DH), D_MODEL),
        'xa_wo': dense(ks[23], (DEPTH, X_HEADS * X_DH, D_MODEL), X_HEADS * X_DH),
    }


def reference(x, mem, rel_bias_table, norm_pre, norm_post, mem_norm, ffn_wg, ffn_wu, ffn_wd,
              ev_w_in, ev_cmp_pe, ev_cmp_w, ev_w_out, od_w_in, od_q_norm, od_kv_norm, od_w_uq,
              od_w_ukv, od_lambda, od_subln, od_w_out, xa_wq, xa_wkv, xa_wo):
    for i in range(DEPTH):
        g_pre, g_post = norm_pre[i], norm_post[i]
        h = rmsnorm(x, g_pre[0])
        x = x + 0.5 * rmsnorm(swiglu(h, ffn_wg[i, 0], ffn_wu[i, 0], ffn_wd[i, 0]), g_post[0])
        h = rmsnorm(x, g_pre[1])
        if i % 2 == 0:
            e = i // 2
            y = even_mixer(h, ev_w_in[e], ev_w_out[e], ev_cmp_pe[e], ev_cmp_w[e], rel_bias_table)
        else:
            o = i // 2
            lam_init = 0.8 - 0.6 * math.exp(-0.3 * i)
            y = odd_mixer(h, od_w_in[o], od_w_out[o], od_q_norm[o], od_kv_norm[o], od_w_uq[o],
                          od_w_ukv[o], od_lambda[o], od_subln[o], rel_bias_table, lam_init)
        x = x + rmsnorm(y, g_post[1])
        h = rmsnorm(x, g_pre[2])
        m = rmsnorm(mem, mem_norm[i])
        x = x + rmsnorm(mem_xattn(h, m, xa_wq[i], xa_wkv[i], xa_wo[i]), g_post[2])
        h = rmsnorm(x, g_pre[3])
        x = x + 0.5 * rmsnorm(swiglu(h, ffn_wg[i, 1], ffn_wu[i, 1], ffn_wd[i, 1]), g_post[3])
    return x
```

```python
import functools
import math

import numpy as np
import jax
import jax.numpy as jnp
from jax import lax
from jax.experimental import pallas as pl
from jax.experimental.pallas import tpu as pltpu

F32, BF16, I32 = jnp.float32, jnp.bfloat16, jnp.int32

N_BUCKETS = 32
MAX_DISTANCE = 128
RMS_EPS = 1e-6
ROPE_THETA = 10000.0
NEG_INF = -1e30
A_HEADS, A_KV, A_DH = 8, 2, 64
A_GRP = A_HEADS // A_KV
CMP_LEN, CMP_STRIDE, SLC_LEN, SLC_TOP, WIN = 32, 16, 64, 16, 512
FORCE_BONUS = 1e3
B_HEADS, B_DH, IDX_HEADS, IDX_DH, IDX_TOPK_MAX = 8, 64, 8, 32, 256
C_HEADS, C_Q_RANK, C_KV_RANK, C_DN, C_DR, C_DV = 8, 256, 128, 64, 32, 64
D_HEADS, D_DH, D_DV = 4, 64, 128
X_HEADS, X_DH = 4, 128
EV_SIZES = (A_HEADS * A_DH, 6 * A_KV * A_DH, 3 * A_HEADS,
            B_HEADS * B_DH, B_DH, B_DH, IDX_HEADS * IDX_DH, IDX_DH, IDX_HEADS)
OD_SIZES = (C_Q_RANK, C_KV_RANK, C_DR, D_HEADS * 2 * D_DH, D_HEADS * 2 * D_DH, D_HEADS * D_DV)

LANES = 128
VMEM_LIMIT_BYTES = 56 * 1024 * 1024
ONES_LANE = 64
INT_MIN = -2 ** 31

TOK_TILE = 512
FFN_CHUNK = 256
SPARSE_TQ = 128
DENSE_TQ = 256


def _params(*sem):
    return pltpu.CompilerParams(dimension_semantics=sem, vmem_limit_bytes=VMEM_LIMIT_BYTES)


def _const_spec(shape):
    nd = len(shape)
    return pl.BlockSpec(shape, lambda *_: (0,) * nd, pipeline_mode=pl.Buffered(1))


def _dot(a, b):
    return jnp.dot(a, b, preferred_element_type=F32)


def _dot_t(a, b):
    return lax.dot_general(a, b, (((1,), (1,)), ((), ())), preferred_element_type=F32)


def _rms(x, g):
    return x * lax.rsqrt(jnp.mean(x * x, axis=-1, keepdims=True) + RMS_EPS) * g


def _sigmoid(x):
    return 1.0 / (1.0 + jnp.exp(-x))


def _online_step(q, k, v, add, carry):
    m, acc = carry
    s = _dot_t(q, k)
    if add is not None:
        s = s + add
    m_new = jnp.maximum(m, jnp.max(s, axis=-1, keepdims=True))
    p = jnp.exp(s - m_new)
    acc = jnp.exp(m - m_new) * acc + _dot(p.astype(BF16), v)
    return m_new, acc


def _flash_init(m_rows, n_lanes):
    return jnp.full((m_rows, 1), -jnp.inf, F32), jnp.zeros((m_rows, n_lanes), F32)


def _ffn_body(x_ref, gpre_ref, gpost_ref, wgu_ref, wd_ref, o_ref, *, n_chunks, fc):
    x = x_ref[...]
    h = _rms(x, gpre_ref[...]).astype(BF16)
    acc = jnp.zeros(x.shape, F32)
    for c in range(n_chunks):
        gu = _dot(h, wgu_ref[:, c * 2 * fc:(c + 1) * 2 * fc])
        g, u = gu[:, :fc], gu[:, fc:]
        a = g * _sigmoid(g) * u
        acc = acc + _dot(a.astype(BF16), wd_ref[c * fc:(c + 1) * fc, :])
    o_ref[...] = x + 0.5 * _rms(acc, gpost_ref[...])


def _ffn(x, g_pre, g_post, wg, wu, wd):
    n, d = x.shape
    f = wg.shape[1]
    fc = FFN_CHUNK
    n_chunks = f // fc
    assert n_chunks * fc == f and n % TOK_TILE == 0
    wgu = jnp.concatenate([wg.reshape(d, n_chunks, fc), wu.reshape(d, n_chunks, fc)], axis=-1)
    wgu = wgu.reshape(d, 2 * f).astype(BF16)
    return pl.pallas_call(
        functools.partial(_ffn_body, n_chunks=n_chunks, fc=fc),
        grid=(n // TOK_TILE,),
        in_specs=[pl.BlockSpec((TOK_TILE, d), lambda i: (i, 0)),
                  _const_spec((1, d)), _const_spec((1, d)),
                  _const_spec((d, 2 * f)), _const_spec((f, d))],
        out_specs=pl.BlockSpec((TOK_TILE, d), lambda i: (i, 0)),
        out_shape=jax.ShapeDtypeStruct((n, d), F32),
        compiler_params=_params("arbitrary"),
        name="ffn",
    )(x, g_pre.reshape(1, d), g_post.reshape(1, d), wgu, wd.astype(BF16))


def _proj_body(x_ref, g_ref, w_ref, c_ref, *o_refs, bounds):
    h = _rms(x_ref[...], g_ref[...]).astype(BF16)
    for o_ref, (a, b) in zip(o_refs, bounds):
        o_ref[...] = (_dot(h, w_ref[:, a:b]) + c_ref[:, a:b]).astype(o_ref.dtype)


def _proj(x, g, w_segs, c_segs, dtypes):
    n, d = x.shape
    widths = [w.shape[1] for w in w_segs]
    assert all(wd_ % LANES == 0 for wd_ in widths) and n % TOK_TILE == 0
    offs = np.concatenate([[0], np.cumsum(widths)])
    bounds = tuple((int(offs[i]), int(offs[i + 1])) for i in range(len(widths)))
    w = jnp.concatenate(w_segs, axis=1).astype(BF16)
    c = jnp.concatenate(c_segs, axis=1).astype(F32)
    tot = int(offs[-1])
    return pl.pallas_call(
        functools.partial(_proj_body, bounds=bounds),
        grid=(n // TOK_TILE,),
        in_specs=[pl.BlockSpec((TOK_TILE, d), lambda i: (i, 0)),
                  _const_spec((1, d)), _const_spec((d, tot)), _const_spec((1, tot))],
        out_specs=[pl.BlockSpec((TOK_TILE, wd_), lambda i: (i, 0)) for wd_ in widths],
        out_shape=[jax.ShapeDtypeStruct((n, wd_), dt) for wd_, dt in zip(widths, dtypes)],
        compiler_params=_params("arbitrary"),
        name="norm_proj",
    )(x, g.reshape(1, d), w, c)


def _out_body(a_ref, b_ref, wa_ref, wb_ref, x_ref, g_ref, o_ref):
    y = _dot(a_ref[...], wa_ref[...]) + _dot(b_ref[...], wb_ref[...])
    o_ref[...] = x_ref[...] + _rms(y, g_ref[...])


def _out_proj(oa, ob, wa, wb, x, g_post):
    n, d = x.shape
    ka, kb = oa.shape[1], ob.shape[1]
    return pl.pallas_call(
        _out_body,
        grid=(n // TOK_TILE,),
        in_specs=[pl.BlockSpec((TOK_TILE, ka), lambda i: (i, 0)),
                  pl.BlockSpec((TOK_TILE, kb), lambda i: (i, 0)),
                  _const_spec((ka, d)), _const_spec((kb, d)),
                  pl.BlockSpec((TOK_TILE, d), lambda i: (i, 0)),
                  _const_spec((1, d))],
        out_specs=pl.BlockSpec((TOK_TILE, d), lambda i: (i, 0)),
        out_shape=jax.ShapeDtypeStruct((n, d), F32),
        compiler_params=_params("arbitrary"),
        name="out_proj",
    )(oa, ob, wa.astype(BF16), wb.astype(BF16), x, g_post.reshape(1, d))


def _xattn_body(x_ref, gpre_ref, gpost_ref, wq_ref, kv_ref, wo_ref, o_ref, *, heads, dh):
    x = x_ref[0]
    h = _rms(x, gpre_ref[...]).astype(BF16)
    q = _dot(h, wq_ref[...]).astype(BF16)
    kv = kv_ref[0]
    scale = dh ** -0.5
    outs = []
    for hh in range(heads):
        k = kv[:, hh * dh:(hh + 1) * dh]
        v = kv[:, (heads + hh) * dh:(heads + hh + 1) * dh]
        s = _dot_t(q[:, hh * dh:(hh + 1) * dh], k) * scale
        p = jnp.exp(s - jnp.max(s, axis=-1, keepdims=True))
        p = p / jnp.sum(p, axis=-1, keepdims=True)
        outs.append(_dot(p.astype(BF16), v).astype(BF16))
    y = _dot(jnp.concatenate(outs, axis=-1), wo_ref[...])
    o_ref[0] = x + _rms(y, gpost_ref[...])


def _mem_xattn(x, kv, g_pre, g_post, wq, wo):
    b, s, d = x.shape
    mlen, kvw = kv.shape[1:]
    hd = X_HEADS * X_DH
    return pl.pallas_call(
        functools.partial(_xattn_body, heads=X_HEADS, dh=X_DH),
        grid=(b, s // TOK_TILE),
        in_specs=[pl.BlockSpec((1, TOK_TILE, d), lambda i, j: (i, j, 0)),
                  _const_spec((1, d)), _const_spec((1, d)), _const_spec((d, hd)),
                  pl.BlockSpec((1, mlen, kvw), lambda i, j: (i, 0, 0)),
                  _const_spec((hd, d))],
        out_specs=pl.BlockSpec((1, TOK_TILE, d), lambda i, j: (i, j, 0)),
        out_shape=jax.ShapeDtypeStruct((b, s, d), F32),
        compiler_params=_params("arbitrary", "arbitrary"),
        name="mem_xattn",
    )(x, g_pre.reshape(1, d), g_post.reshape(1, d), wq.astype(BF16), kv, wo.astype(BF16))


def _bucket_np(dist):
    max_exact = N_BUCKETS // 2
    d = np.maximum(dist, 0)
    ratio = np.log(np.maximum(d, 1).astype(np.float32) / np.float32(max_exact)) / np.float32(
        math.log(MAX_DISTANCE / max_exact))
    large = np.minimum(max_exact + (ratio * np.float32(N_BUCKETS - max_exact)).astype(np.int32),
                       N_BUCKETS - 1)
    return np.where(d < max_exact, d, large).astype(np.int32)


def _bias_tiles(tab, t, win_tile):
    r = np.arange(t)[:, None]
    c = np.arange(t)[None, :]
    far = _bucket_np(2 * t + r - c)
    assert (far == far[0, 0]).all() and _bucket_np(np.array(10 ** 6)) == far[0, 0]
    idx = [np.where(r >= c, _bucket_np(r - c), N_BUCKETS), _bucket_np(t + r - c), far]
    if win_tile:
        idx.append(np.where(r < c, far, N_BUCKETS))
    idx = np.stack(idx).astype(np.int32)
    tab_ext = jnp.concatenate([tab.astype(F32), jnp.full((1, tab.shape[1]), NEG_INF, F32)], axis=0)
    return jnp.moveaxis(tab_ext[idx], -1, 0)


def _cmp_body(x_ref, pe_ref, w_ref, o_ref):
    x = x_ref[0, 0, 0]
    xn = pltpu.roll(x, x.shape[0] - 1, 0)
    half = x.shape[1]
    lo = (x + pe_ref[0, :, :half]).astype(BF16)
    hi = (xn + pe_ref[0, :, half:]).astype(BF16)
    o_ref[0, 0, 0] = (_dot(lo, w_ref[0, :half, :]) + _dot(hi, w_ref[0, half:, :])).astype(BF16)


def _nsa_compress(kcvc, cmp_pe, cmp_w, b, s):
    nb = s // CMP_STRIDE
    half = CMP_STRIDE * A_DH
    x = kcvc.reshape(b, nb, CMP_STRIDE, 2, A_KV, A_DH).transpose(0, 3, 4, 1, 2, 5)
    x = x.reshape(b, 2, A_KV, nb, half)
    pe = cmp_pe.reshape(2, 1, CMP_LEN * A_DH)
    w = jnp.pad(cmp_w.reshape(2, CMP_LEN * A_DH, A_DH), ((0, 0), (0, 0), (0, LANES - A_DH))).astype(BF16)
    return pl.pallas_call(
        _cmp_body,
        grid=(b, 2, A_KV),
        in_specs=[pl.BlockSpec((1, 1, 1, nb, half), lambda i, j, g: (i, j, g, 0, 0)),
                  pl.BlockSpec((1, 1, 2 * half), lambda i, j, g: (j, 0, 0)),
                  pl.BlockSpec((1, 2 * half, LANES), lambda i, j, g: (j, 0, 0))],
        out_specs=pl.BlockSpec((1, 1, 1, nb, LANES), lambda i, j, g: (i, j, g, 0, 0)),
        out_shape=jax.ShapeDtypeStruct((b, 2, A_KV, nb, LANES), BF16),
        compiler_params=_params("arbitrary", "arbitrary", "arbitrary"),
        name="nsa_compress",
    )(x, pe, w)


def _nsa_body(q_ref, kc_ref, vc_ref, bc_ref, ks_ref, vs_ref, kw_ref, vw_ref, ga_ref, bias_ref,
              ovl_ref, exp_ref, o_ref, mask_ref, *, tq, seq, n_sb):
    qt = pl.program_id(2)
    rr = A_GRP
    m_rows = rr * tq
    ncp = kc_ref.shape[3]
    q = q_ref[0]
    qs = jnp.concatenate([q[:, r * LANES:(r + 1) * LANES] for r in range(rr)], axis=0)

    s_c = _dot_t(qs, kc_ref[0, 0, 0]) + bc_ref[...].reshape(m_rows, ncp)
    p = jnp.where(s_c > 0.5 * NEG_INF, jnp.exp(s_c - jnp.max(s_c, axis=-1, keepdims=True)), 0.0)
    l = jnp.sum(p, axis=-1, keepdims=True)
    pn = p * jnp.where(l > 0.0, 1.0 / l, 0.0)
    o_c = _dot(pn.astype(BF16), vc_ref[0, 0, 0])

    imp = jnp.sum(pn.reshape(rr, tq, ncp), axis=0)
    impb = jnp.dot(imp, ovl_ref[...], precision=lax.Precision.HIGHEST, preferred_element_type=F32)
    lane = lax.broadcasted_iota(I32, (tq, LANES), 1)
    cur = (qt * tq + lax.broadcasted_iota(I32, (tq, LANES), 0)) // SLC_LEN
    forced = (lane == 0) | (lane == cur) | (lane == cur - 1)
    score = jnp.where(lane <= cur, impb + jnp.where(forced, FORCE_BONUS, 0.0), NEG_INF)
    rank = jnp.zeros((tq, LANES), F32)
    for i in range(n_sb):
        ci = score[:, i:i + 1]
        later = jnp.where(lane > i, 1.0, 0.0)
        rank = rank + jnp.where(ci > score, 1.0, jnp.where(ci == score, later, 0.0))
    sel = jnp.where(rank < float(min(SLC_TOP, n_sb)), jnp.where(lane <= cur, 1.0, 0.0), 0.0)
    selb = sel.astype(BF16)
    cw = min(seq, 512)
    for c in range(seq // cw):
        mask_ref[:, c * cw:(c + 1) * cw] = (_dot(selb, exp_ref[:, c * cw:(c + 1) * cw]) - 1.0) * (-NEG_INF)

    def slc(kt, carry):
        ks = pl.multiple_of(kt * tq, tq)
        add = (bias_ref[0, jnp.minimum(qt - kt, 2)].reshape(rr, tq, tq)
               + mask_ref[:, pl.ds(ks, tq)][None]).reshape(m_rows, tq)
        return _online_step(qs, ks_ref[0, pl.ds(ks, tq), :], vs_ref[0, pl.ds(ks, tq), :], add, carry)

    _, acc_s = lax.fori_loop(0, qt + 1, slc, _flash_init(m_rows, LANES))

    n_woff = WIN // tq

    def win(kt, carry):
        ks = pl.multiple_of(kt * tq, tq)
        off = qt - kt
        add = bias_ref[0, jnp.where(off == n_woff, 3, jnp.minimum(off, 2))]
        return _online_step(qs, kw_ref[0, pl.ds(ks, tq), :], vw_ref[0, pl.ds(ks, tq), :], add, carry)

    _, acc_w = lax.fori_loop(jnp.maximum(qt - n_woff, 0), qt + 1, win, _flash_init(m_rows, LANES))

    gs = _sigmoid(ga_ref[0])
    for r in range(rr):
        a_s = acc_s[r * tq:(r + 1) * tq]
        a_w = acc_w[r * tq:(r + 1) * tq]
        o = (gs[:, 3 * r:3 * r + 1] * o_c[r * tq:(r + 1) * tq]
             + gs[:, 3 * r + 1:3 * r + 2] * (a_s / a_s[:, ONES_LANE:ONES_LANE + 1])
             + gs[:, 3 * r + 2:3 * r + 3] * (a_w / a_w[:, ONES_LANE:ONES_LANE + 1]))
        o_ref[0, :, r * LANES:(r + 1) * LANES] = o.astype(BF16)


def _nsa(q, kvc, kvsw, ga, tab, b, s):
    tq = SPARSE_TQ
    ncp = s // CMP_STRIDE
    n_sb = s // SLC_LEN
    assert n_sb <= LANES and WIN % tq == 0 and s % tq == 0
    t = np.arange(s)[:, None]
    cmp_end = np.arange(ncp)[None, :] * CMP_STRIDE + CMP_LEN - 1
    idx_c = np.where((cmp_end <= t) & (cmp_end < s), _bucket_np(t - cmp_end), N_BUCKETS).astype(np.int32)
    tab_ext = jnp.concatenate([tab.astype(F32), jnp.full((1, A_HEADS), NEG_INF, F32)], axis=0)
    bias_c = jnp.moveaxis(tab_ext[idx_c], -1, 0)
    tiles = _bias_tiles(tab, tq, True)
    tiles = tiles.reshape(A_KV, A_GRP, 4, tq, tq).transpose(0, 2, 1, 3, 4).reshape(A_KV, 4, A_GRP * tq, tq)
    cmp_start = np.arange(ncp)[:, None] * CMP_STRIDE
    sb_start = np.arange(LANES)[None, :] * SLC_LEN
    ovl = ((cmp_start < sb_start + SLC_LEN) & (cmp_start + CMP_LEN - 1 >= sb_start)
           & (np.arange(ncp)[:, None] < ncp - 1) & (np.arange(LANES)[None, :] < n_sb))
    ovl = jnp.asarray(ovl.astype(np.float32))
    expand = jnp.asarray((np.arange(LANES)[:, None] == np.arange(s)[None, :] // SLC_LEN).astype(np.float32), BF16)
    kv_spec = lambda blk: pl.BlockSpec((1, s, LANES), lambda i, g, j: (i, 0, 2 * blk + g))
    return pl.pallas_call(
        functools.partial(_nsa_body, tq=tq, seq=s, n_sb=n_sb),
        grid=(b, A_KV, s // tq),
        in_specs=[pl.BlockSpec((1, tq, A_GRP * LANES), lambda i, g, j: (i, j, g)),
                  pl.BlockSpec((1, 1, 1, ncp, LANES), lambda i, g, j: (i, 0, g, 0, 0)),
                  pl.BlockSpec((1, 1, 1, ncp, LANES), lambda i, g, j: (i, 1, g, 0, 0)),
                  pl.BlockSpec((A_GRP, tq, ncp), lambda i, g, j: (g, j, 0)),
                  kv_spec(0), kv_spec(1), kv_spec(2), kv_spec(3),
                  pl.BlockSpec((1, tq, LANES), lambda i, g, j: (i, j, g)),
                  pl.BlockSpec((1, 4, A_GRP * tq, tq), lambda i, g, j: (g, 0, 0, 0)),
                  _const_spec((ncp, LANES)), _const_spec((LANES, s))],
        out_specs=pl.BlockSpec((1, tq, A_GRP * LANES), lambda i, g, j: (i, j, g)),
        out_shape=jax.ShapeDtypeStruct((b, s, A_HEADS * LANES), BF16),
        scratch_shapes=[pltpu.VMEM((tq, s), F32)],
        compiler_params=_params("arbitrary", "arbitrary", "arbitrary"),
        name="nsa_attn",
    )(q, kvc, kvc, bias_c, kvsw, kvsw, kvsw, kvsw, ga, tiles, ovl, expand)


def _dsa_body(q_ref, k_ref, v_ref, iq_ref, ik_ref, iw_ref, bias_ref, o_ref, keys_ref, *,
              tq, seq, k_top):
    qt = pl.program_id(1)
    nh = B_HEADS
    m_rows = nh * tq
    nk = qt + 1
    row = lax.broadcasted_iota(I32, (tq, tq), 0)
    col = lax.broadcasted_iota(I32, (tq, tq), 1)

    iq = iq_ref[0]
    iqs = jnp.concatenate([iq[:, h * LANES:(h + 1) * LANES] for h in range(IDX_HEADS)], axis=0)
    iw = iw_ref[0]
    ws = jnp.concatenate([iw[:, h:h + 1] for h in range(IDX_HEADS)], axis=0)

    def scores(kt, _):
        ks = pl.multiple_of(kt * tq, tq)
        lg = _dot_t(iqs, ik_ref[0, pl.ds(ks, tq), :])
        sc = jnp.sum((jnp.maximum(lg, 0.0) * ws).reshape(IDX_HEADS, tq, tq), axis=0)
        bits = lax.bitcast_convert_type(sc, I32)
        key = jnp.where(bits < 0, bits ^ 0x7FFFFFFF, bits)
        vis = (col + kt * tq) <= (row + qt * tq)
        keys_ref[:, pl.ds(ks, tq)] = jnp.where(vis, key, INT_MIN)
        return 0

    lax.fori_loop(0, nk, scores, 0)

    def count(ind):
        def body(kt, acc):
            ks = pl.multiple_of(kt * tq, tq)
            return acc + ind(keys_ref[:, pl.ds(ks, tq)], kt)
        acc = lax.fori_loop(0, nk, body, jnp.zeros((tq, tq), F32))
        return jnp.sum(acc, axis=-1, keepdims=True)

    kf = float(k_top)
    c0 = count(lambda t, kt: jnp.where(t >= 0, 1.0, 0.0))
    thr = jnp.where(c0 >= kf, 0, INT_MIN).astype(I32)

    def bisect(i, thr):
        cand = thr | (jnp.int32(1) << (30 - i))
        c = count(lambda t, kt: jnp.where(t >= cand, 1.0, 0.0))
        return jnp.where(c >= kf, cand, thr)

    thr = lax.fori_loop(0, 31, bisect, thr)

    c_gt = count(lambda t, kt: jnp.where(t > thr, 1.0, 0.0))
    c_ge = count(lambda t, kt: jnp.where(t >= thr, 1.0, 0.0))
    need = kf - c_gt
    n_bits = int(math.log2(seq))
    assert 2 ** n_bits == seq

    def cutoff():
        def step(i, cut):
            cand = cut | (jnp.int32(1) << (n_bits - 1 - i))
            c = count(lambda t, kt: jnp.where(t == thr, jnp.where(col + kt * tq < cand, 1.0, 0.0), 0.0))
            return jnp.where(c < need, cand, cut)
        return lax.fori_loop(0, n_bits, step, jnp.zeros((tq, 1), I32))

    excess = jnp.max(jnp.where(c_ge > kf, 1.0, 0.0)) > 0.5
    cut = lax.cond(excess, cutoff, lambda: jnp.full((tq, 1), seq, I32))

    q = q_ref[0]
    qs = jnp.concatenate([q[:, h * LANES:(h + 1) * LANES] for h in range(nh)], axis=0)

    def attend(kt, carry):
        ks = pl.multiple_of(kt * tq, tq)
        key = keys_ref[:, pl.ds(ks, tq)]
        tie = jnp.where(col + kt * tq <= cut, 0.0, NEG_INF)
        sel = jnp.where(key > thr, 0.0, jnp.where(key == thr, tie, NEG_INF))
        add = (bias_ref[jnp.minimum(qt - kt, 2)].reshape(nh, tq, tq) + sel[None]).reshape(m_rows, tq)
        return _online_step(qs, k_ref[0, pl.ds(ks, tq), :], v_ref[0, pl.ds(ks, tq), :], add, carry)

    _, acc = lax.fori_loop(0, nk, attend, _flash_init(m_rows, LANES))
    o = acc / acc[:, ONES_LANE:ONES_LANE + 1]
    for h in range(nh):
        o_ref[0, :, h * LANES:(h + 1) * LANES] = o[h * tq:(h + 1) * tq].astype(BF16)


def _dsa(q, kbvb, iq, ik, iw, tab, b, s):
    tq = SPARSE_TQ
    k_top = min(IDX_TOPK_MAX, s // 4)
    tiles = _bias_tiles(tab, tq, False)
    tiles = tiles.transpose(1, 0, 2, 3).reshape(3, B_HEADS * tq, tq)
    return pl.pallas_call(
        functools.partial(_dsa_body, tq=tq, seq=s, k_top=k_top),
        grid=(b, s // tq),
        in_specs=[pl.BlockSpec((1, tq, B_HEADS * LANES), lambda i, j: (i, j, 0)),
                  pl.BlockSpec((1, s, LANES), lambda i, j: (i, 0, 0)),
                  pl.BlockSpec((1, s, LANES), lambda i, j: (i, 0, 1)),
                  pl.BlockSpec((1, tq, IDX_HEADS * LANES), lambda i, j: (i, j, 0)),
                  pl.BlockSpec((1, s, LANES), lambda i, j: (i, 0, 0)),
                  pl.BlockSpec((1, tq, LANES), lambda i, j: (i, j, 0)),
                  _const_spec((3, B_HEADS * tq, tq))],
        out_specs=pl.BlockSpec((1, tq, B_HEADS * LANES), lambda i, j: (i, j, 0)),
        out_shape=jax.ShapeDtypeStruct((b, s, B_HEADS * LANES), BF16),
        scratch_shapes=[pltpu.VMEM((tq, s), I32)],
        compiler_params=_params("arbitrary", "arbitrary"),
        name="dsa_attn",
    )(q, kbvb, kbvb, iq, ik, iw, tiles)


def _mla_prep_body(cq_ref, ckv_ref, kr_ref, qn_ref, kvn_ref, wq1_ref, wq2_ref, wk_ref, wv_ref,
                   cos_ref, sin_ref, ones_ref, q_o, k_o, v_o):
    cqn = _rms(cq_ref[...], qn_ref[...]).astype(BF16)
    ckn = _rms(ckv_ref[...], kvn_ref[...]).astype(BF16)
    cos, sin = cos_ref[...], sin_ref[...]
    cos_h = jnp.concatenate([cos] * C_HEADS, axis=1)
    sin_h = jnp.concatenate([sin] * C_HEADS, axis=1)
    q_o[...] = (_dot(cqn, wq1_ref[...]) * cos_h + _dot(cqn, wq2_ref[...]) * sin_h).astype(BF16)
    kr = kr_ref[...]
    kr_rot = kr[:, :LANES] * cos + kr[:, LANES:] * sin
    k_o[...] = (_dot(ckn, wk_ref[...]) + jnp.concatenate([kr_rot] * C_HEADS, axis=1)).astype(BF16)
    v_o[...] = (_dot(ckn, wv_ref[...]) + ones_ref[...]).astype(BF16)


def _rot_cols(w):
    half = C_DR // 2
    return jnp.concatenate([-w[..., half:], w[..., :half]], axis=-1)


def _mla_prep(cq, ckv, kr, q_norm, kv_norm, w_uq, w_ukv, s):
    n = cq.shape[0]
    hw = C_HEADS * LANES
    padq = LANES - C_DN - C_DR
    wq = w_uq.reshape(C_Q_RANK, C_HEADS, C_DN + C_DR)
    wq1 = jnp.pad(wq, ((0, 0), (0, 0), (0, padq))).reshape(C_Q_RANK, hw)
    wq2 = jnp.pad(_rot_cols(wq[..., C_DN:]), ((0, 0), (0, 0), (C_DN, padq))).reshape(C_Q_RANK, hw)
    wkv = w_ukv.reshape(C_KV_RANK, C_HEADS, C_DN + C_DV)
    wk = jnp.pad(wkv[..., :C_DN], ((0, 0), (0, 0), (0, LANES - C_DN))).reshape(C_KV_RANK, hw)
    wv = jnp.pad(wkv[..., C_DN:], ((0, 0), (0, 0), (0, LANES - C_DV))).reshape(C_KV_RANK, hw)
    half = C_DR // 2
    inv = ROPE_THETA ** (-jnp.arange(half, dtype=F32) / half)
    ang = jnp.arange(s, dtype=F32)[:, None] * inv[None, :]
    cos, sin = jnp.cos(ang), jnp.sin(ang)
    cos_t = jnp.concatenate([jnp.ones((s, C_DN), F32), cos, cos, jnp.zeros((s, padq), F32)], axis=1)
    sin_t = jnp.concatenate([jnp.zeros((s, C_DN), F32), sin, sin, jnp.zeros((s, padq), F32)], axis=1)
    ones = jnp.tile(jnp.asarray(np.arange(LANES) == ONES_LANE, F32), C_HEADS).reshape(1, hw)
    tile_tok = lambda w_: pl.BlockSpec((TOK_TILE, w_), lambda i: (i, 0))
    n_pos = s // TOK_TILE
    pos_spec = pl.BlockSpec((TOK_TILE, LANES), lambda i: (i % n_pos, 0))
    return pl.pallas_call(
        _mla_prep_body,
        grid=(n // TOK_TILE,),
        in_specs=[tile_tok(C_Q_RANK), tile_tok(C_KV_RANK), tile_tok(2 * LANES),
                  _const_spec((1, C_Q_RANK)), _const_spec((1, C_KV_RANK)),
                  _const_spec((C_Q_RANK, hw)), _const_spec((C_Q_RANK, hw)),
                  _const_spec((C_KV_RANK, hw)), _const_spec((C_KV_RANK, hw)),
                  pos_spec, pos_spec, _const_spec((1, hw))],
        out_specs=[tile_tok(hw)] * 3,
        out_shape=[jax.ShapeDtypeStruct((n, hw), BF16)] * 3,
        compiler_params=_params("arbitrary"),
        name="mla_prep",
    )(cq, ckv, kr, q_norm.reshape(1, -1), kv_norm.reshape(1, -1), wq1.astype(BF16), wq2.astype(BF16),
      wk.astype(BF16), wv.astype(BF16), cos_t, sin_t, ones)


def _mla_attn_body(q_ref, k_ref, v_ref, o_ref, *, tq, scale):
    qt = pl.program_id(2)
    q = q_ref[0]

    def tile(kt):
        ks = pl.multiple_of(kt * tq, tq)
        return k_ref[0, pl.ds(ks, tq), :], v_ref[0, pl.ds(ks, tq), :]

    def step(kt, carry, add):
        k, v = tile(kt)
        m, acc = carry
        s = _dot_t(q, k) * scale
        if add is not None:
            s = s + add
        m_new = jnp.maximum(m, jnp.max(s, axis=-1, keepdims=True))
        p = jnp.exp(s - m_new)
        return m_new, jnp.exp(m - m_new) * acc + _dot(p.astype(BF16), v)

    carry = lax.fori_loop(0, qt, lambda kt, c: step(kt, c, None), _flash_init(tq, LANES))
    row = lax.broadcasted_iota(I32, (tq, tq), 0)
    col = lax.broadcasted_iota(I32, (tq, tq), 1)
    _, acc = step(qt, carry, jnp.where(row >= col, 0.0, NEG_INF))
    o_ref[0] = (acc / acc[:, ONES_LANE:ONES_LANE + 1]).astype(BF16)


def _mla_attn(q, k, v, b, s):
    tq = DENSE_TQ
    spec_q = pl.BlockSpec((1, tq, LANES), lambda i, h, j: (i, j, h))
    spec_kv = pl.BlockSpec((1, s, LANES), lambda i, h, j: (i, 0, h))
    return pl.pallas_call(
        functools.partial(_mla_attn_body, tq=tq, scale=(C_DN + C_DR) ** -0.5),
        grid=(b, C_HEADS, s // tq),
        in_specs=[spec_q, spec_kv, spec_kv],
        out_specs=spec_q,
        out_shape=jax.ShapeDtypeStruct((b, s, C_HEADS * LANES), BF16),
        compiler_params=_params("arbitrary", "arbitrary", "arbitrary"),
        name="mla_attn",
    )(q, k, v)


def _diff_body(lam_ref, q_ref, k_ref, v_ref, bias_ref, g_ref, o_ref, *, tq, out_scale):
    qt = pl.program_id(2)
    q = q_ref[0]
    outs = []
    for e in range(2):
        qe = q[:, e * LANES:(e + 1) * LANES]

        def step(kt, carry, e=e, qe=qe):
            ks = pl.multiple_of(kt * tq, tq)
            return _online_step(qe, k_ref[0, pl.ds(ks, tq), e * LANES:(e + 1) * LANES],
                                v_ref[0, pl.ds(ks, tq), :], bias_ref[0, jnp.minimum(qt - kt, 2)], carry)

        _, acc = lax.fori_loop(0, qt + 1, step, _flash_init(tq, 2 * LANES))
        outs.append(acc[:, :D_DV] / acc[:, D_DV:D_DV + 1])
    o = outs[0] - lam_ref[0] * outs[1]
    o_ref[0] = (_rms(o, g_ref[...]) * out_scale).astype(BF16)


def _diff_attn(q, k, v, tab, lam_full, subln, lam_init, b, s):
    tq = DENSE_TQ
    tiles = _bias_tiles(tab, tq, False)
    spec_q = pl.BlockSpec((1, tq, 2 * LANES), lambda i, h, j: (i, j, h))
    spec_kv = pl.BlockSpec((1, s, 2 * LANES), lambda i, h, j: (i, 0, h))
    return pl.pallas_call(
        functools.partial(_diff_body, tq=tq, out_scale=1.0 - lam_init),
        grid=(b, D_HEADS, s // tq),
        in_specs=[pl.BlockSpec(memory_space=pltpu.SMEM),
                  spec_q, spec_kv, spec_kv,
                  pl.BlockSpec((1, 3, tq, tq), lambda i, h, j: (h, 0, 0, 0)),
                  _const_spec((1, D_DV))],
        out_specs=pl.BlockSpec((1, tq, D_DV), lambda i, h, j: (i, j, h)),
        out_shape=jax.ShapeDtypeStruct((b, s, D_HEADS * D_DV), BF16),
        compiler_params=_params("arbitrary", "arbitrary", "arbitrary"),
        name="diff_attn",
    )(lam_full.reshape(1), q, k, v, tiles, subln.reshape(1, D_DV))


def _pad_blocks(w, n, width, to=LANES, at=0):
    k = w.shape[0]
    return jnp.pad(w.reshape(k, n, width), ((0, 0), (0, 0), (at, to - width - at))).reshape(k, n * to)


def _ones_row(n_blocks, which, to=LANES, lane=ONES_LANE):
    row = np.zeros((n_blocks, to), np.float32)
    row[list(which), lane] = 1.0
    return jnp.asarray(row.reshape(1, n_blocks * to))


def _split(w, sizes):
    return jnp.split(w, [int(c) for c in np.cumsum(sizes)[:-1]], axis=-1)


def _pad_rows(w, n, width):
    d = w.shape[1]
    return jnp.pad(w.reshape(n, width, d), ((0, 0), (0, LANES - width), (0, 0))).reshape(n * LANES, d)


def _even_mixer(x2, g_pre, g_post, w_in, w_out, cmp_pe, cmp_w, rel_tab, b, s):
    qa, kva, ga, qb, kb, vb, iq, ik, iw = _split(w_in, EV_SIZES)
    half = 2 * A_KV * A_DH
    w_segs = [_pad_blocks(qa * A_DH ** -0.5, A_HEADS, A_DH), kva[:, :half],
              _pad_blocks(kva[:, half:], 4 * A_KV, A_DH), _pad_blocks(ga, A_KV, 3 * A_GRP),
              _pad_blocks(qb * B_DH ** -0.5, B_HEADS, B_DH),
              _pad_blocks(jnp.concatenate([kb, vb], axis=1), 2, B_DH),
              _pad_blocks(iq, IDX_HEADS, IDX_DH), _pad_blocks(ik, 1, IDX_DH), _pad_blocks(iw, 1, IDX_HEADS)]
    zeros = lambda w: jnp.zeros((1, w.shape[1]), F32)
    c_segs = [zeros(w) for w in w_segs]
    c_segs[2] = _ones_row(4 * A_KV, (2, 3, 6, 7))
    c_segs[5] = _ones_row(2, (1,))
    dts = [BF16, F32, BF16, F32, BF16, BF16, BF16, BF16, F32]
    q_a, kcvc, kvsw, g_a, q_b, kbvb, i_q, i_k, i_w = _proj(x2, g_pre, w_segs, c_segs, dts)
    r3 = lambda a: a.reshape(b, s, a.shape[-1])
    kvc = _nsa_compress(kcvc, cmp_pe, cmp_w, b, s)
    o_a = _nsa(r3(q_a), kvc, r3(kvsw), r3(g_a), rel_tab[:, :A_HEADS], b, s)
    o_b = _dsa(r3(q_b), r3(kbvb), r3(i_q), r3(i_k), r3(i_w), rel_tab[:, A_HEADS:A_HEADS + B_HEADS], b, s)
    na = A_HEADS * A_DH
    wa = _pad_rows(w_out[:na], A_HEADS, A_DH)
    wb = _pad_rows(w_out[na:], B_HEADS, B_DH)
    return _out_proj(o_a.reshape(b * s, -1), o_b.reshape(b * s, -1), wa, wb, x2, g_post)


def _odd_mixer(x2, g_pre, g_post, w_in, w_out, q_norm, kv_norm, w_uq, w_ukv, lam, subln, rel_tab,
               lam_init, b, s):
    c_q, c_kv, k_rope, qd, kd, vd = _split(w_in, OD_SIZES)
    w_kr = jnp.concatenate([_pad_blocks(k_rope, 1, C_DR, at=C_DN),
                            _pad_blocks(_rot_cols(k_rope), 1, C_DR, at=C_DN)], axis=1)
    w_segs = [c_q, c_kv, w_kr, _pad_blocks(qd * D_DH ** -0.5, 2 * D_HEADS, D_DH),
              _pad_blocks(kd, 2 * D_HEADS, D_DH), _pad_blocks(vd, D_HEADS, D_DV, to=2 * LANES)]
    c_segs = [jnp.zeros((1, w.shape[1]), F32) for w in w_segs]
    c_segs[5] = _ones_row(D_HEADS, range(D_HEADS), to=2 * LANES, lane=D_DV)
    dts = [F32, F32, F32, BF16, BF16, BF16]
    cq, ckv, kr, q_d, k_d, v_d = _proj(x2, g_pre, w_segs, c_segs, dts)
    q_m, k_m, v_m = _mla_prep(cq, ckv, kr, q_norm, kv_norm, w_uq, w_ukv, s)
    r3 = lambda a: a.reshape(b, s, a.shape[-1])
    o_c = _mla_attn(r3(q_m), r3(k_m), r3(v_m), b, s)
    lam32 = lam.astype(F32)
    lam_full = jnp.exp(jnp.sum(lam32[0] * lam32[1])) - jnp.exp(jnp.sum(lam32[2] * lam32[3])) + lam_init
    o_d = _diff_attn(r3(q_d), r3(k_d), r3(v_d), rel_tab[:, A_HEADS + B_HEADS:], lam_full, subln,
                     lam_init, b, s)
    nc = C_HEADS * C_DV
    wc = _pad_rows(w_out[:nc], C_HEADS, C_DV)
    return _out_proj(o_c.reshape(b * s, -1), o_d.reshape(b * s, -1), wc, w_out[nc:], x2, g_post)


def kernel(x, mem, rel_bias_table, norm_pre, norm_post, mem_norm, ffn_wg, ffn_wu, ffn_wd, ev_w_in,
           ev_cmp_pe, ev_cmp_w, ev_w_out, od_w_in, od_q_norm, od_kv_norm, od_w_uq, od_w_ukv, od_lambda,
           od_subln, od_w_out, xa_wq, xa_wkv, xa_wo):
    b, s, d = x.shape
    depth = norm_pre.shape[0]
    x2 = x.reshape(b * s, d)
    mem2 = mem.reshape(-1, d)
    for i in range(depth):
        g_pre, g_post = norm_pre[i], norm_post[i]
        x2 = _ffn(x2, g_pre[0], g_post[0], ffn_wg[i, 0], ffn_wu[i, 0], ffn_wd[i, 0])
        if i % 2 == 0:
            e = i // 2
            x2 = _even_mixer(x2, g_pre[1], g_post[1], ev_w_in[e], ev_w_out[e], ev_cmp_pe[e], ev_cmp_w[e],
                             rel_bias_table, b, s)
        else:
            o = i // 2
            lam_init = 0.8 - 0.6 * math.exp(-0.3 * i)
            x2 = _odd_mixer(x2, g_pre[1], g_post[1], od_w_in[o], od_w_out[o], od_q_norm[o], od_kv_norm[o],
                            od_w_uq[o], od_w_ukv[o], od_lambda[o], od_subln[o], rel_bias_table, lam_init, b, s)
        (kv,) = _proj(mem2, mem_norm[i], [xa_wkv[i]], [jnp.zeros((1, xa_wkv.shape[-1]), F32)], [BF16])
        x3 = _mem_xattn(x2.reshape(b, s, d), kv.reshape(b, -1, kv.shape[-1]), g_pre[2], g_post[2],
                        xa_wq[i], xa_wo[i])
        x2 = _ffn(x3.reshape(b * s, d), g_pre[3], g_post[3], ffn_wg[i, 1], ffn_wu[i, 1], ffn_wd[i, 1])
    return x2.reshape(b, s, d)
```

```python
import functools
import math

import numpy as np
import jax
import jax.numpy as jnp
from jax import lax
from jax.experimental import pallas as pl
from jax.experimental.pallas import tpu as pltpu

F32, BF16, I32 = jnp.float32, jnp.bfloat16, jnp.int32

N_BUCKETS = 32
MAX_DISTANCE = 128
RMS_EPS = 1e-6
ROPE_THETA = 10000.0
NEG_INF = -1e30
A_HEADS, A_KV, A_DH = 8, 2, 64
A_GRP = A_HEADS // A_KV
CMP_LEN, CMP_STRIDE, SLC_LEN, SLC_TOP, WIN = 32, 16, 64, 16, 512
FORCE_BONUS = 1e3
B_HEADS, B_DH, IDX_HEADS, IDX_DH, IDX_TOPK_MAX = 8, 64, 8, 32, 256
C_HEADS, C_Q_RANK, C_KV_RANK, C_DN, C_DR, C_DV = 8, 256, 128, 64, 32, 64
D_HEADS, D_DH, D_DV = 4, 64, 128
X_HEADS, X_DH = 4, 128
EV_SIZES = (A_HEADS * A_DH, 6 * A_KV * A_DH, 3 * A_HEADS,
            B_HEADS * B_DH, B_DH, B_DH, IDX_HEADS * IDX_DH, IDX_DH, IDX_HEADS)
OD_SIZES = (C_Q_RANK, C_KV_RANK, C_DR, D_HEADS * 2 * D_DH, D_HEADS * 2 * D_DH, D_HEADS * D_DV)

LANES = 128
VMEM_LIMIT_BYTES = 56 * 1024 * 1024
ONES_LANE = 64
INT_MIN = -2 ** 31

TOK_TILE = 512
FFN_CHUNK = 256
SPARSE_TQ = 128
WIDE_TILES = 4
DENSE_TQ = 512


def _params(*sem):
    return pltpu.CompilerParams(dimension_semantics=sem, vmem_limit_bytes=VMEM_LIMIT_BYTES)


def _const_spec(shape):
    nd = len(shape)
    return pl.BlockSpec(shape, lambda *_: (0,) * nd, pipeline_mode=pl.Buffered(1))


def _dot(a, b):
    return jnp.dot(a, b, preferred_element_type=F32)


def _dot_t(a, b):
    return lax.dot_general(a, b, (((1,), (1,)), ((), ())), preferred_element_type=F32)


def _rms(x, g):
    return x * lax.rsqrt(jnp.mean(x * x, axis=-1, keepdims=True) + RMS_EPS) * g


def _sigmoid(x):
    return 1.0 / (1.0 + jnp.exp(-x))


def _online_step(q, k, v, carry, add=None, shared=None, heads=1):
    m, acc = carry
    s = _dot_t(q, k)
    if add is not None:
        s = s + add
    if shared is not None:
        s = (s.reshape(heads, shared.shape[0], shared.shape[1]) + shared[None]).reshape(s.shape)
    m_new = jnp.maximum(m, jnp.max(s, axis=-1, keepdims=True))
    p = jnp.exp(s - m_new)
    acc = jnp.exp(m - m_new) * acc + _dot(p.astype(BF16), v)
    return m_new, acc


def _flash_init(m_rows, n_lanes):
    return jnp.full((m_rows, 1), -jnp.inf, F32), jnp.zeros((m_rows, n_lanes), F32)


def _ffn_body(x_ref, gpre_ref, gpost_ref, wgu_ref, wd_ref, o_ref, *, n_chunks, fc):
    x = x_ref[...]
    h = _rms(x, gpre_ref[...]).astype(BF16)
    acc = jnp.zeros(x.shape, F32)
    for c in range(n_chunks):
        gu = _dot(h, wgu_ref[:, c * 2 * fc:(c + 1) * 2 * fc])
        g, u = gu[:, :fc], gu[:, fc:]
        a = g * _sigmoid(g) * u
        acc = acc + _dot(a.astype(BF16), wd_ref[c * fc:(c + 1) * fc, :])
    o_ref[...] = x + 0.5 * _rms(acc, gpost_ref[...])


def _ffn(x, g_pre, g_post, wg, wu, wd):
    n, d = x.shape
    f = wg.shape[1]
    fc = FFN_CHUNK
    n_chunks = f // fc
    assert n_chunks * fc == f and n % TOK_TILE == 0
    wgu = jnp.concatenate([wg.reshape(d, n_chunks, fc), wu.reshape(d, n_chunks, fc)], axis=-1)
    wgu = wgu.reshape(d, 2 * f).astype(BF16)
    return pl.pallas_call(
        functools.partial(_ffn_body, n_chunks=n_chunks, fc=fc),
        grid=(n // TOK_TILE,),
        in_specs=[pl.BlockSpec((TOK_TILE, d), lambda i: (i, 0)),
                  _const_spec((1, d)), _const_spec((1, d)),
                  _const_spec((d, 2 * f)), _const_spec((f, d))],
        out_specs=pl.BlockSpec((TOK_TILE, d), lambda i: (i, 0)),
        out_shape=jax.ShapeDtypeStruct((n, d), F32),
        compiler_params=_params("arbitrary"),
        name="ffn",
    )(x, g_pre.reshape(1, d), g_post.reshape(1, d), wgu, wd.astype(BF16))


def _proj_body(x_ref, g_ref, w_ref, c_ref, *o_refs, bounds):
    h = _rms(x_ref[...], g_ref[...]).astype(BF16)
    for o_ref, (a, b) in zip(o_refs, bounds):
        o_ref[...] = (_dot(h, w_ref[:, a:b]) + c_ref[:, a:b]).astype(o_ref.dtype)


def _proj(x, g, w_segs, c_segs, dtypes):
    n, d = x.shape
    widths = [w.shape[1] for w in w_segs]
    assert all(wd_ % LANES == 0 for wd_ in widths) and n % TOK_TILE == 0
    offs = np.concatenate([[0], np.cumsum(widths)])
    bounds = tuple((int(offs[i]), int(offs[i + 1])) for i in range(len(widths)))
    w = jnp.concatenate(w_segs, axis=1).astype(BF16)
    c = jnp.concatenate(c_segs, axis=1).astype(F32)
    tot = int(offs[-1])
    return pl.pallas_call(
        functools.partial(_proj_body, bounds=bounds),
        grid=(n // TOK_TILE,),
        in_specs=[pl.BlockSpec((TOK_TILE, d), lambda i: (i, 0)),
                  _const_spec((1, d)), _const_spec((d, tot)), _const_spec((1, tot))],
        out_specs=[pl.BlockSpec((TOK_TILE, wd_), lambda i: (i, 0)) for wd_ in widths],
        out_shape=[jax.ShapeDtypeStruct((n, wd_), dt) for wd_, dt in zip(widths, dtypes)],
        compiler_params=_params("arbitrary"),
        name="norm_proj",
    )(x, g.reshape(1, d), w, c)


def _out_body(a_ref, b_ref, wa_ref, wb_ref, x_ref, g_ref, o_ref):
    y = _dot(a_ref[...], wa_ref[...]) + _dot(b_ref[...], wb_ref[...])
    o_ref[...] = x_ref[...] + _rms(y, g_ref[...])


def _out_proj(oa, ob, wa, wb, x, g_post):
    n, d = x.shape
    ka, kb = oa.shape[1], ob.shape[1]
    return pl.pallas_call(
        _out_body,
        grid=(n // TOK_TILE,),
        in_specs=[pl.BlockSpec((TOK_TILE, ka), lambda i: (i, 0)),
                  pl.BlockSpec((TOK_TILE, kb), lambda i: (i, 0)),
                  _const_spec((ka, d)), _const_spec((kb, d)),
                  pl.BlockSpec((TOK_TILE, d), lambda i: (i, 0)),
                  _const_spec((1, d))],
        out_specs=pl.BlockSpec((TOK_TILE, d), lambda i: (i, 0)),
        out_shape=jax.ShapeDtypeStruct((n, d), F32),
        compiler_params=_params("arbitrary"),
        name="out_proj",
    )(oa, ob, wa.astype(BF16), wb.astype(BF16), x, g_post.reshape(1, d))


def _xattn_body(x_ref, gpre_ref, gpost_ref, wq_ref, kv_ref, wo_ref, o_ref, *, heads, dh):
    x = x_ref[0]
    h = _rms(x, gpre_ref[...]).astype(BF16)
    q = _dot(h, wq_ref[...]).astype(BF16)
    kv = kv_ref[0]
    scale = dh ** -0.5
    outs = []
    for hh in range(heads):
        k = kv[:, hh * dh:(hh + 1) * dh]
        v = kv[:, (heads + hh) * dh:(heads + hh + 1) * dh]
        s = _dot_t(q[:, hh * dh:(hh + 1) * dh], k) * scale
        p = jnp.exp(s - jnp.max(s, axis=-1, keepdims=True))
        p = p / jnp.sum(p, axis=-1, keepdims=True)
        outs.append(_dot(p.astype(BF16), v).astype(BF16))
    y = _dot(jnp.concatenate(outs, axis=-1), wo_ref[...])
    o_ref[0] = x + _rms(y, gpost_ref[...])


def _mem_xattn(x, kv, g_pre, g_post, wq, wo):
    b, s, d = x.shape
    mlen, kvw = kv.shape[1:]
    hd = X_HEADS * X_DH
    return pl.pallas_call(
        functools.partial(_xattn_body, heads=X_HEADS, dh=X_DH),
        grid=(b, s // TOK_TILE),
        in_specs=[pl.BlockSpec((1, TOK_TILE, d), lambda i, j: (i, j, 0)),
                  _const_spec((1, d)), _const_spec((1, d)), _const_spec((d, hd)),
                  pl.BlockSpec((1, mlen, kvw), lambda i, j: (i, 0, 0)),
                  _const_spec((hd, d))],
        out_specs=pl.BlockSpec((1, TOK_TILE, d), lambda i, j: (i, j, 0)),
        out_shape=jax.ShapeDtypeStruct((b, s, d), F32),
        compiler_params=_params("arbitrary", "arbitrary"),
        name="mem_xattn",
    )(x, g_pre.reshape(1, d), g_post.reshape(1, d), wq.astype(BF16), kv, wo.astype(BF16))


def _bucket_np(dist):
    max_exact = N_BUCKETS // 2
    d = np.maximum(dist, 0)
    ratio = np.log(np.maximum(d, 1).astype(np.float32) / np.float32(max_exact)) / np.float32(
        math.log(MAX_DISTANCE / max_exact))
    large = np.minimum(max_exact + (ratio * np.float32(N_BUCKETS - max_exact)).astype(np.int32),
                       N_BUCKETS - 1)
    return np.where(d < max_exact, d, large).astype(np.int32)


def _rel_bias(tab, dist, visible, rel_to_far):
    dist = np.maximum(np.asarray(dist), 0).astype(np.int32)
    bk = _bucket_np(np.arange(max(int(dist.max()), 4 * MAX_DISTANCE) + 1))
    assert (np.diff(bk) >= 0).all() and bk[-1] == N_BUCKETS - 1
    tab = tab.astype(F32)
    col = lambda b: tab[b].reshape((-1,) + (1,) * dist.ndim)
    d = jnp.asarray(dist)[None]
    out = jnp.broadcast_to(col(0), (tab.shape[1],) + dist.shape)
    for b in range(1, N_BUCKETS):
        first = np.nonzero(bk == b)[0]
        if first.size:
            out = jnp.where(d >= int(first[0]), col(b), out)
    if rel_to_far:
        out = out - col(N_BUCKETS - 1)
    return jnp.where(jnp.asarray(np.asarray(visible))[None], out, NEG_INF)


def _far_distance():
    bk = _bucket_np(np.arange(4 * MAX_DISTANCE))
    return int(np.nonzero(bk == N_BUCKETS - 1)[0][0])


def _bias_tiles(tab, t):
    assert t + 1 >= _far_distance()
    r = np.arange(t)[:, None]
    c = np.arange(t)[None, :]
    dist = np.stack([r - c, t + r - c, 2 * t + r - c])
    everywhere = np.ones((t, t), bool)
    return _rel_bias(tab, dist, np.stack([r >= c, everywhere, everywhere]), True)


def _key_loops(qt, wide, wide_step, narrow_step, init):
    n_wide = jnp.maximum(qt - 1, 0) // wide
    carry = lax.fori_loop(0, n_wide, wide_step, init)
    return lax.fori_loop(n_wide * wide, qt + 1, narrow_step, carry)


def _cmp_body(x_ref, pe_ref, w_ref, o_ref):
    x = x_ref[0, 0, 0]
    xn = pltpu.roll(x, x.shape[0] - 1, 0)
    half = x.shape[1]
    lo = (x + pe_ref[0, :, :half]).astype(BF16)
    hi = (xn + pe_ref[0, :, half:]).astype(BF16)
    o_ref[0, 0, 0] = (_dot(lo, w_ref[0, :half, :]) + _dot(hi, w_ref[0, half:, :])).astype(BF16)


def _nsa_compress(kcvc, cmp_pe, cmp_w, b, s):
    nb = s // CMP_STRIDE
    half = CMP_STRIDE * A_DH
    x = kcvc.reshape(b, nb, CMP_STRIDE, 2, A_KV, A_DH).transpose(0, 3, 4, 1, 2, 5)
    x = x.reshape(b, 2, A_KV, nb, half)
    pe = cmp_pe.reshape(2, 1, CMP_LEN * A_DH)
    w = jnp.pad(cmp_w.reshape(2, CMP_LEN * A_DH, A_DH), ((0, 0), (0, 0), (0, LANES - A_DH))).astype(BF16)
    return pl.pallas_call(
        _cmp_body,
        grid=(b, 2, A_KV),
        in_specs=[pl.BlockSpec((1, 1, 1, nb, half), lambda i, j, g: (i, j, g, 0, 0)),
                  pl.BlockSpec((1, 1, 2 * half), lambda i, j, g: (j, 0, 0)),
                  pl.BlockSpec((1, 2 * half, LANES), lambda i, j, g: (j, 0, 0))],
        out_specs=pl.BlockSpec((1, 1, 1, nb, LANES), lambda i, j, g: (i, j, g, 0, 0)),
        out_shape=jax.ShapeDtypeStruct((b, 2, A_KV, nb, LANES), BF16),
        compiler_params=_params("arbitrary", "arbitrary", "arbitrary"),
        name="nsa_compress",
    )(x, pe, w)


def _nsa_body(q_ref, kc_ref, vc_ref, bc_ref, ks_ref, vs_ref, kw_ref, vw_ref, ga_ref, bias_ref,
              ovl_ref, exp_ref, o_ref, mask_ref, *, tq, seq, n_sb):
    qt = pl.program_id(2)
    rr = A_GRP
    m_rows = rr * tq
    ncp = kc_ref.shape[3]
    q = q_ref[0]
    qs = jnp.concatenate([q[:, r * LANES:(r + 1) * LANES] for r in range(rr)], axis=0)

    s_c = _dot_t(qs, kc_ref[0, 0, 0]) + bc_ref[...].reshape(m_rows, ncp)
    p = jnp.where(s_c > 0.5 * NEG_INF, jnp.exp(s_c - jnp.max(s_c, axis=-1, keepdims=True)), 0.0)
    l = jnp.sum(p, axis=-1, keepdims=True)
    pn = p * jnp.where(l > 0.0, 1.0 / l, 0.0)
    o_c = _dot(pn.astype(BF16), vc_ref[0, 0, 0])

    imp = jnp.sum(pn.reshape(rr, tq, ncp), axis=0)
    impb = jnp.dot(imp, ovl_ref[...], precision=lax.Precision.HIGHEST, preferred_element_type=F32)
    lane = lax.broadcasted_iota(I32, (tq, LANES), 1)
    cur = (qt * tq + lax.broadcasted_iota(I32, (tq, LANES), 0)) // SLC_LEN
    forced = (lane == 0) | (lane == cur) | (lane == cur - 1)
    score = jnp.where(lane <= cur, impb + jnp.where(forced, FORCE_BONUS, 0.0), NEG_INF)
    rank = jnp.zeros((tq, LANES), F32)
    for i in range(n_sb):
        ci = score[:, i:i + 1]
        later = jnp.where(lane > i, 1.0, 0.0)
        rank = rank + jnp.where(ci > score, 1.0, jnp.where(ci == score, later, 0.0))
    sel = jnp.where(rank < float(min(SLC_TOP, n_sb)), jnp.where(lane <= cur, 1.0, 0.0), 0.0)
    selb = sel.astype(BF16)
    cw = min(seq, 512)
    for c in range(seq // cw):
        mask_ref[:, c * cw:(c + 1) * cw] = (_dot(selb, exp_ref[:, c * cw:(c + 1) * cw]) - 1.0) * (-NEG_INF)

    def slc_wide(kw, carry):
        ks = pl.multiple_of(kw * tw, tw)
        return _online_step(qs, ks_ref[0, pl.ds(ks, tw), :], vs_ref[0, pl.ds(ks, tw), :], carry,
                            shared=mask_ref[:, pl.ds(ks, tw)], heads=rr)

    def slc(kt, carry):
        ks = pl.multiple_of(kt * tq, tq)
        return _online_step(qs, ks_ref[0, pl.ds(ks, tq), :], vs_ref[0, pl.ds(ks, tq), :], carry,
                            add=bias_ref[0, jnp.minimum(qt - kt, 2)], shared=mask_ref[:, pl.ds(ks, tq)],
                            heads=rr)

    tw = WIDE_TILES * tq
    _, acc_s = _key_loops(qt, WIDE_TILES, slc_wide, slc, _flash_init(m_rows, LANES))

    n_woff = WIN // tq

    def win(kt, carry):
        ks = pl.multiple_of(kt * tq, tq)
        off = qt - kt
        add = bias_ref[0, jnp.where(off == n_woff, 3, jnp.minimum(off, 2))]
        return _online_step(qs, kw_ref[0, pl.ds(ks, tq), :], vw_ref[0, pl.ds(ks, tq), :], carry, add=add)

    _, acc_w = lax.fori_loop(jnp.maximum(qt - n_woff, 0), qt + 1, win, _flash_init(m_rows, LANES))

    gs = _sigmoid(ga_ref[0])
    for r in range(rr):
        a_s = acc_s[r * tq:(r + 1) * tq]
        a_w = acc_w[r * tq:(r + 1) * tq]
        o = (gs[:, 3 * r:3 * r + 1] * o_c[r * tq:(r + 1) * tq]
             + gs[:, 3 * r + 1:3 * r + 2] * (a_s / a_s[:, ONES_LANE:ONES_LANE + 1])
             + gs[:, 3 * r + 2:3 * r + 3] * (a_w / a_w[:, ONES_LANE:ONES_LANE + 1]))
        o_ref[0, :, r * LANES:(r + 1) * LANES] = o.astype(BF16)


def _nsa(q, kvc, kvsw, ga, tab, b, s):
    tq = SPARSE_TQ
    ncp = s // CMP_STRIDE
    n_sb = s // SLC_LEN
    assert n_sb <= LANES and WIN % tq == 0 and s % (WIDE_TILES * tq) == 0
    t = np.arange(s)[:, None]
    cmp_end = np.arange(ncp)[None, :] * CMP_STRIDE + CMP_LEN - 1
    bias_c = _rel_bias(tab, t - cmp_end, cmp_end <= t, False)
    r = np.arange(tq)[:, None]
    c = np.arange(tq)[None, :]
    win_mask = jnp.asarray(np.where(r < c, 0.0, NEG_INF).astype(np.float32))
    tiles = jnp.concatenate([_bias_tiles(tab, tq), jnp.broadcast_to(win_mask, (A_HEADS, 1, tq, tq))], axis=1)
    tiles = tiles.reshape(A_KV, A_GRP, 4, tq, tq).transpose(0, 2, 1, 3, 4).reshape(A_KV, 4, A_GRP * tq, tq)
    cmp_start = np.arange(ncp)[:, None] * CMP_STRIDE
    sb_start = np.arange(LANES)[None, :] * SLC_LEN
    ovl = ((cmp_start < sb_start + SLC_LEN) & (cmp_start + CMP_LEN - 1 >= sb_start)
           & (np.arange(ncp)[:, None] < ncp - 1) & (np.arange(LANES)[None, :] < n_sb))
    ovl = jnp.asarray(ovl.astype(np.float32))
    expand = jnp.asarray((np.arange(LANES)[:, None] == np.arange(s)[None, :] // SLC_LEN).astype(np.float32), BF16)
    kv_spec = lambda blk: pl.BlockSpec((1, s, LANES), lambda i, g, j: (i, 0, 2 * blk + g))
    return pl.pallas_call(
        functools.partial(_nsa_body, tq=tq, seq=s, n_sb=n_sb),
        grid=(b, A_KV, s // tq),
        in_specs=[pl.BlockSpec((1, tq, A_GRP * LANES), lambda i, g, j: (i, j, g)),
                  pl.BlockSpec((1, 1, 1, ncp, LANES), lambda i, g, j: (i, 0, g, 0, 0)),
                  pl.BlockSpec((1, 1, 1, ncp, LANES), lambda i, g, j: (i, 1, g, 0, 0)),
                  pl.BlockSpec((A_GRP, tq, ncp), lambda i, g, j: (g, j, 0)),
                  kv_spec(0), kv_spec(1), kv_spec(2), kv_spec(3),
                  pl.BlockSpec((1, tq, LANES), lambda i, g, j: (i, j, g)),
                  pl.BlockSpec((1, 4, A_GRP * tq, tq), lambda i, g, j: (g, 0, 0, 0)),
                  _const_spec((ncp, LANES)), _const_spec((LANES, s))],
        out_specs=pl.BlockSpec((1, tq, A_GRP * LANES), lambda i, g, j: (i, j, g)),
        out_shape=jax.ShapeDtypeStruct((b, s, A_HEADS * LANES), BF16),
        scratch_shapes=[pltpu.VMEM((tq, s), F32)],
        compiler_params=_params("arbitrary", "arbitrary", "arbitrary"),
        name="nsa_attn",
    )(q, kvc, kvc, bias_c, kvsw, kvsw, kvsw, kvsw, ga, tiles, ovl, expand)


def _dsa_body(q_ref, k_ref, v_ref, iq_ref, ik_ref, iw_ref, bias_ref, o_ref, keys_ref, wb_ref, *,
              tq, seq, k_top):
    qt = pl.program_id(1)
    nh = B_HEADS
    m_rows = nh * tq
    tw = WIDE_TILES * tq
    n_wide = qt // WIDE_TILES + 1
    col = lax.broadcasted_iota(I32, (tq, tq), 1)
    row_w = lax.broadcasted_iota(I32, (tq, tw), 0)
    col_w = lax.broadcasted_iota(I32, (tq, tw), 1)

    iq = iq_ref[0]
    iqs = jnp.concatenate([iq[:, h * LANES:(h + 1) * LANES] for h in range(IDX_HEADS)], axis=0)
    iw = iw_ref[0]
    for h in range(IDX_HEADS):
        wb_ref[h * tq:(h + 1) * tq, :] = jnp.broadcast_to(iw[:, h:h + 1], (tq, LANES))

    def scores(kw, _):
        ks = pl.multiple_of(kw * tw, tw)
        lg = _dot_t(iqs, ik_ref[0, pl.ds(ks, tw), :])
        w = wb_ref[...]
        parts = []
        for j in range(tw // LANES):
            part = jnp.maximum(lg[:, j * LANES:(j + 1) * LANES], 0.0) * w
            parts.append(jnp.sum(part.reshape(IDX_HEADS, tq, LANES), axis=0))
        sc = jnp.concatenate(parts, axis=1)
        bits = lax.bitcast_convert_type(sc, I32)
        key = jnp.where(bits < 0, bits ^ 0x7FFFFFFF, bits)
        vis = (col_w + kw * tw) <= (row_w + qt * tq)
        keys_ref[:, pl.ds(ks, tw)] = jnp.where(vis, key, INT_MIN)
        return 0

    lax.fori_loop(0, n_wide, scores, 0)

    def count(ind):
        def body(kw, acc):
            ks = pl.multiple_of(kw * tw, tw)
            x = ind(keys_ref[:, pl.ds(ks, tw)], kw)
            for j in range(tw // LANES):
                acc = acc + x[:, j * LANES:(j + 1) * LANES]
            return acc
        acc = lax.fori_loop(0, n_wide, body, jnp.zeros((tq, LANES), F32))
        return jnp.sum(acc, axis=-1, keepdims=True)

    kf = float(k_top)
    c0 = count(lambda t, kw: jnp.where(t >= 0, 1.0, 0.0))
    thr = jnp.where(c0 >= kf, 0, INT_MIN).astype(I32)

    def bisect(i, thr):
        cand = thr | (jnp.int32(1) << (30 - i))
        c = count(lambda t, kw: jnp.where(t >= cand, 1.0, 0.0))
        return jnp.where(c >= kf, cand, thr)

    thr = lax.fori_loop(0, 31, bisect, thr)

    c_gt = count(lambda t, kw: jnp.where(t > thr, 1.0, 0.0))
    c_ge = count(lambda t, kw: jnp.where(t >= thr, 1.0, 0.0))
    need = kf - c_gt
    n_bits = int(math.log2(seq))
    assert 2 ** n_bits == seq

    def cutoff():
        def step(i, cut):
            cand = cut | (jnp.int32(1) << (n_bits - 1 - i))
            c = count(lambda t, kw: jnp.where(t == thr, jnp.where(col_w + kw * tw < cand, 1.0, 0.0), 0.0))
            return jnp.where(c < need, cand, cut)
        return lax.fori_loop(0, n_bits, step, jnp.zeros((tq, 1), I32))

    excess = jnp.max(jnp.where(c_ge > kf, 1.0, 0.0)) > 0.5
    cut = lax.cond(excess, cutoff, lambda: jnp.full((tq, 1), seq, I32))

    q = q_ref[0]
    qs = jnp.concatenate([q[:, h * LANES:(h + 1) * LANES] for h in range(nh)], axis=0)

    def sel_add(key, idx):
        tie = jnp.where(idx <= cut, 0.0, NEG_INF)
        return jnp.where(key > thr, 0.0, jnp.where(key == thr, tie, NEG_INF))

    def attend_wide(kw, carry):
        ks = pl.multiple_of(kw * tw, tw)
        sel = sel_add(keys_ref[:, pl.ds(ks, tw)], col_w + kw * tw)
        return _online_step(qs, k_ref[0, pl.ds(ks, tw), :], v_ref[0, pl.ds(ks, tw), :], carry,
                            shared=sel, heads=nh)

    def attend(kt, carry):
        ks = pl.multiple_of(kt * tq, tq)
        sel = sel_add(keys_ref[:, pl.ds(ks, tq)], col + kt * tq)
        return _online_step(qs, k_ref[0, pl.ds(ks, tq), :], v_ref[0, pl.ds(ks, tq), :], carry,
                            add=bias_ref[jnp.minimum(qt - kt, 2)], shared=sel, heads=nh)

    _, acc = _key_loops(qt, WIDE_TILES, attend_wide, attend, _flash_init(m_rows, LANES))
    o = acc / acc[:, ONES_LANE:ONES_LANE + 1]
    for h in range(nh):
        o_ref[0, :, h * LANES:(h + 1) * LANES] = o[h * tq:(h + 1) * tq].astype(BF16)


def _dsa(q, kbvb, iq, ik, iw, tab, b, s):
    tq = SPARSE_TQ
    k_top = min(IDX_TOPK_MAX, s // 4)
    assert tq == LANES and s % (WIDE_TILES * tq) == 0
    tiles = _bias_tiles(tab, tq).transpose(1, 0, 2, 3).reshape(3, B_HEADS * tq, tq)
    return pl.pallas_call(
        functools.partial(_dsa_body, tq=tq, seq=s, k_top=k_top),
        grid=(b, s // tq),
        in_specs=[pl.BlockSpec((1, tq, B_HEADS * LANES), lambda i, j: (i, j, 0)),
                  pl.BlockSpec((1, s, LANES), lambda i, j: (i, 0, 0)),
                  pl.BlockSpec((1, s, LANES), lambda i, j: (i, 0, 1)),
                  pl.BlockSpec((1, tq, IDX_HEADS * LANES), lambda i, j: (i, j, 0)),
                  pl.BlockSpec((1, s, LANES), lambda i, j: (i, 0, 0)),
                  pl.BlockSpec((1, tq, LANES), lambda i, j: (i, j, 0)),
                  _const_spec((3, B_HEADS * tq, tq))],
        out_specs=pl.BlockSpec((1, tq, B_HEADS * LANES), lambda i, j: (i, j, 0)),
        out_shape=jax.ShapeDtypeStruct((b, s, B_HEADS * LANES), BF16),
        scratch_shapes=[pltpu.VMEM((tq, s), I32), pltpu.VMEM((IDX_HEADS * tq, LANES), F32)],
        compiler_params=_params("arbitrary", "arbitrary"),
        name="dsa_attn",
    )(q, kbvb, kbvb, iq, ik, iw, tiles)


def _mla_prep_body(cq_ref, ckv_ref, kr_ref, qn_ref, kvn_ref, wq1_ref, wq2_ref, wk_ref, wv_ref,
                   cos_ref, sin_ref, ones_ref, q_o, k_o, v_o):
    cqn = _rms(cq_ref[...], qn_ref[...]).astype(BF16)
    ckn = _rms(ckv_ref[...], kvn_ref[...]).astype(BF16)
    cos, sin = cos_ref[...], sin_ref[...]
    cos_h = jnp.concatenate([cos] * C_HEADS, axis=1)
    sin_h = jnp.concatenate([sin] * C_HEADS, axis=1)
    q_o[...] = (_dot(cqn, wq1_ref[...]) * cos_h + _dot(cqn, wq2_ref[...]) * sin_h).astype(BF16)
    kr = kr_ref[...]
    kr_rot = kr[:, :LANES] * cos + kr[:, LANES:] * sin
    k_o[...] = (_dot(ckn, wk_ref[...]) + jnp.concatenate([kr_rot] * C_HEADS, axis=1)).astype(BF16)
    v_o[...] = (_dot(ckn, wv_ref[...]) + ones_ref[...]).astype(BF16)


def _rot_cols(w):
    half = C_DR // 2
    return jnp.concatenate([-w[..., half:], w[..., :half]], axis=-1)


def _mla_prep(cq, ckv, kr, q_norm, kv_norm, w_uq, w_ukv, s):
    n = cq.shape[0]
    hw = C_HEADS * LANES
    padq = LANES - C_DN - C_DR
    wq = w_uq.reshape(C_Q_RANK, C_HEADS, C_DN + C_DR)
    wq1 = jnp.pad(wq, ((0, 0), (0, 0), (0, padq))).reshape(C_Q_RANK, hw)
    wq2 = jnp.pad(_rot_cols(wq[..., C_DN:]), ((0, 0), (0, 0), (C_DN, padq))).reshape(C_Q_RANK, hw)
    wkv = w_ukv.reshape(C_KV_RANK, C_HEADS, C_DN + C_DV)
    wk = jnp.pad(wkv[..., :C_DN], ((0, 0), (0, 0), (0, LANES - C_DN))).reshape(C_KV_RANK, hw)
    wv = jnp.pad(wkv[..., C_DN:], ((0, 0), (0, 0), (0, LANES - C_DV))).reshape(C_KV_RANK, hw)
    half = C_DR // 2
    inv = ROPE_THETA ** (-jnp.arange(half, dtype=F32) / half)
    ang = jnp.arange(s, dtype=F32)[:, None] * inv[None, :]
    cos, sin = jnp.cos(ang), jnp.sin(ang)
    cos_t = jnp.concatenate([jnp.ones((s, C_DN), F32), cos, cos, jnp.zeros((s, padq), F32)], axis=1)
    sin_t = jnp.concatenate([jnp.zeros((s, C_DN), F32), sin, sin, jnp.zeros((s, padq), F32)], axis=1)
    ones = jnp.tile(jnp.asarray(np.arange(LANES) == ONES_LANE, F32), C_HEADS).reshape(1, hw)
    tile_tok = lambda w_: pl.BlockSpec((TOK_TILE, w_), lambda i: (i, 0))
    n_pos = s // TOK_TILE
    pos_spec = pl.BlockSpec((TOK_TILE, LANES), lambda i: (i % n_pos, 0))
    return pl.pallas_call(
        _mla_prep_body,
        grid=(n // TOK_TILE,),
        in_specs=[tile_tok(C_Q_RANK), tile_tok(C_KV_RANK), tile_tok(2 * LANES),
                  _const_spec((1, C_Q_RANK)), _const_spec((1, C_KV_RANK)),
                  _const_spec((C_Q_RANK, hw)), _const_spec((C_Q_RANK, hw)),
                  _const_spec((C_KV_RANK, hw)), _const_spec((C_KV_RANK, hw)),
                  pos_spec, pos_spec, _const_spec((1, hw))],
        out_specs=[tile_tok(hw)] * 3,
        out_shape=[jax.ShapeDtypeStruct((n, hw), BF16)] * 3,
        compiler_params=_params("arbitrary"),
        name="mla_prep",
    )(cq, ckv, kr, q_norm.reshape(1, -1), kv_norm.reshape(1, -1), wq1.astype(BF16), wq2.astype(BF16),
      wk.astype(BF16), wv.astype(BF16), cos_t, sin_t, ones)


def _mla_attn_body(q_ref, k_ref, v_ref, o_ref, *, tq, scale):
    qt = pl.program_id(2)
    q = q_ref[0]

    def tile(kt):
        ks = pl.multiple_of(kt * tq, tq)
        return k_ref[0, pl.ds(ks, tq), :], v_ref[0, pl.ds(ks, tq), :]

    def step(kt, carry, add):
        k, v = tile(kt)
        m, acc = carry
        s = _dot_t(q, k) * scale
        if add is not None:
            s = s + add
        m_new = jnp.maximum(m, jnp.max(s, axis=-1, keepdims=True))
        p = jnp.exp(s - m_new)
        return m_new, jnp.exp(m - m_new) * acc + _dot(p.astype(BF16), v)

    carry = lax.fori_loop(0, qt, lambda kt, c: step(kt, c, None), _flash_init(tq, LANES))
    row = lax.broadcasted_iota(I32, (tq, tq), 0)
    col = lax.broadcasted_iota(I32, (tq, tq), 1)
    _, acc = step(qt, carry, jnp.where(row >= col, 0.0, NEG_INF))
    o_ref[0] = (acc / acc[:, ONES_LANE:ONES_LANE + 1]).astype(BF16)


def _mla_attn(q, k, v, b, s):
    tq = DENSE_TQ
    spec_q = pl.BlockSpec((1, tq, LANES), lambda i, h, j: (i, j, h))
    spec_kv = pl.BlockSpec((1, s, LANES), lambda i, h, j: (i, 0, h))
    return pl.pallas_call(
        functools.partial(_mla_attn_body, tq=tq, scale=(C_DN + C_DR) ** -0.5),
        grid=(b, C_HEADS, s // tq),
        in_specs=[spec_q, spec_kv, spec_kv],
        out_specs=spec_q,
        out_shape=jax.ShapeDtypeStruct((b, s, C_HEADS * LANES), BF16),
        compiler_params=_params("arbitrary", "arbitrary", "arbitrary"),
        name="mla_attn",
    )(q, k, v)


def _diff_body(lam_ref, q_ref, k_ref, v_ref, bias_ref, g_ref, o_ref, *, tq, out_scale):
    qt = pl.program_id(2)
    q = q_ref[0]
    outs = []
    for e in range(2):
        qe = q[:, e * LANES:(e + 1) * LANES]

        def step(kt, carry, near, e=e, qe=qe):
            ks = pl.multiple_of(kt * tq, tq)
            return _online_step(qe, k_ref[0, pl.ds(ks, tq), e * LANES:(e + 1) * LANES],
                                v_ref[0, pl.ds(ks, tq), :], carry,
                                add=bias_ref[0, qt - kt] if near else None)

        n_far = jnp.maximum(qt - 1, 0)
        carry = lax.fori_loop(0, n_far, functools.partial(step, near=False), _flash_init(tq, 2 * LANES))
        _, acc = lax.fori_loop(n_far, qt + 1, functools.partial(step, near=True), carry)
        outs.append(acc[:, :D_DV] / acc[:, D_DV:D_DV + 1])
    o = outs[0] - lam_ref[0] * outs[1]
    o_ref[0] = (_rms(o, g_ref[...]) * out_scale).astype(BF16)


def _diff_attn(q, k, v, tab, lam_full, subln, lam_init, b, s):
    tq = DENSE_TQ
    tiles = _bias_tiles(tab, tq)[:, :2]
    spec_q = pl.BlockSpec((1, tq, 2 * LANES), lambda i, h, j: (i, j, h))
    spec_kv = pl.BlockSpec((1, s, 2 * LANES), lambda i, h, j: (i, 0, h))
    return pl.pallas_call(
        functools.partial(_diff_body, tq=tq, out_scale=1.0 - lam_init),
        grid=(b, D_HEADS, s // tq),
        in_specs=[pl.BlockSpec(memory_space=pltpu.SMEM),
                  spec_q, spec_kv, spec_kv,
                  pl.BlockSpec((1, 2, tq, tq), lambda i, h, j: (h, 0, 0, 0)),
                  _const_spec((1, D_DV))],
        out_specs=pl.BlockSpec((1, tq, D_DV), lambda i, h, j: (i, j, h)),
        out_shape=jax.ShapeDtypeStruct((b, s, D_HEADS * D_DV), BF16),
        compiler_params=_params("arbitrary", "arbitrary", "arbitrary"),
        name="diff_attn",
    )(lam_full.reshape(1), q, k, v, tiles, subln.reshape(1, D_DV))


def _pad_blocks(w, n, width, to=LANES, at=0):
    k = w.shape[0]
    return jnp.pad(w.reshape(k, n, width), ((0, 0), (0, 0), (at, to - width - at))).reshape(k, n * to)


def _ones_row(n_blocks, which, to=LANES, lane=ONES_LANE):
    row = np.zeros((n_blocks, to), np.float32)
    row[list(which), lane] = 1.0
    return jnp.asarray(row.reshape(1, n_blocks * to))


def _split(w, sizes):
    return jnp.split(w, [int(c) for c in np.cumsum(sizes)[:-1]], axis=-1)


def _pad_rows(w, n, width):
    d = w.shape[1]
    return jnp.pad(w.reshape(n, width, d), ((0, 0), (0, LANES - width), (0, 0))).reshape(n * LANES, d)


def _even_mixer(x2, g_pre, g_post, w_in, w_out, cmp_pe, cmp_w, rel_tab, b, s):
    qa, kva, ga, qb, kb, vb, iq, ik, iw = _split(w_in, EV_SIZES)
    half = 2 * A_KV * A_DH
    w_segs = [_pad_blocks(qa * A_DH ** -0.5, A_HEADS, A_DH), kva[:, :half],
              _pad_blocks(kva[:, half:], 4 * A_KV, A_DH), _pad_blocks(ga, A_KV, 3 * A_GRP),
              _pad_blocks(qb * B_DH ** -0.5, B_HEADS, B_DH),
              _pad_blocks(jnp.concatenate([kb, vb], axis=1), 2, B_DH),
              _pad_blocks(iq, IDX_HEADS, IDX_DH), _pad_blocks(ik, 1, IDX_DH), _pad_blocks(iw, 1, IDX_HEADS)]
    zeros = lambda w: jnp.zeros((1, w.shape[1]), F32)
    c_segs = [zeros(w) for w in w_segs]
    c_segs[2] = _ones_row(4 * A_KV, (2, 3, 6, 7))
    c_segs[5] = _ones_row(2, (1,))
    dts = [BF16, F32, BF16, F32, BF16, BF16, BF16, BF16, F32]
    q_a, kcvc, kvsw, g_a, q_b, kbvb, i_q, i_k, i_w = _proj(x2, g_pre, w_segs, c_segs, dts)
    r3 = lambda a: a.reshape(b, s, a.shape[-1])
    kvc = _nsa_compress(kcvc, cmp_pe, cmp_w, b, s)
    o_a = _nsa(r3(q_a), kvc, r3(kvsw), r3(g_a), rel_tab[:, :A_HEADS], b, s)
    o_b = _dsa(r3(q_b), r3(kbvb), r3(i_q), r3(i_k), r3(i_w), rel_tab[:, A_HEADS:A_HEADS + B_HEADS], b, s)
    na = A_HEADS * A_DH
    wa = _pad_rows(w_out[:na], A_HEADS, A_DH)
    wb = _pad_rows(w_out[na:], B_HEADS, B_DH)
    return _out_proj(o_a.reshape(b * s, -1), o_b.reshape(b * s, -1), wa, wb, x2, g_post)


def _odd_mixer(x2, g_pre, g_post, w_in, w_out, q_norm, kv_norm, w_uq, w_ukv, lam, subln, rel_tab,
               lam_init, b, s):
    c_q, c_kv, k_rope, qd, kd, vd = _split(w_in, OD_SIZES)
    w_kr = jnp.concatenate([_pad_blocks(k_rope, 1, C_DR, at=C_DN),
                            _pad_blocks(_rot_cols(k_rope), 1, C_DR, at=C_DN)], axis=1)
    w_segs = [c_q, c_kv, w_kr, _pad_blocks(qd * D_DH ** -0.5, 2 * D_HEADS, D_DH),
              _pad_blocks(kd, 2 * D_HEADS, D_DH), _pad_blocks(vd, D_HEADS, D_DV, to=2 * LANES)]
    c_segs = [jnp.zeros((1, w.shape[1]), F32) for w in w_segs]
    c_segs[5] = _ones_row(D_HEADS, range(D_HEADS), to=2 * LANES, lane=D_DV)
    dts = [F32, F32, F32, BF16, BF16, BF16]
    cq, ckv, kr, q_d, k_d, v_d = _proj(x2, g_pre, w_segs, c_segs, dts)
    q_m, k_m, v_m = _mla_prep(cq, ckv, kr, q_norm, kv_norm, w_uq, w_ukv, s)
    r3 = lambda a: a.reshape(b, s, a.shape[-1])
    o_c = _mla_attn(r3(q_m), r3(k_m), r3(v_m), b, s)
    lam32 = lam.astype(F32)
    lam_full = jnp.exp(jnp.sum(lam32[0] * lam32[1])) - jnp.exp(jnp.sum(lam32[2] * lam32[3])) + lam_init
    o_d = _diff_attn(r3(q_d), r3(k_d), r3(v_d), rel_tab[:, A_HEADS + B_HEADS:], lam_full, subln,
                     lam_init, b, s)
    nc = C_HEADS * C_DV
    wc = _pad_rows(w_out[:nc], C_HEADS, C_DV)
    return _out_proj(o_c.reshape(b * s, -1), o_d.reshape(b * s, -1), wc, w_out[nc:], x2, g_post)


def kernel(x, mem, rel_bias_table, norm_pre, norm_post, mem_norm, ffn_wg, ffn_wu, ffn_wd, ev_w_in,
           ev_cmp_pe, ev_cmp_w, ev_w_out, od_w_in, od_q_norm, od_kv_norm, od_w_uq, od_w_ukv, od_lambda,
           od_subln, od_w_out, xa_wq, xa_wkv, xa_wo):
    b, s, d = x.shape
    depth = norm_pre.shape[0]
    x2 = x.reshape(b * s, d)
    mem2 = mem.reshape(-1, d)
    for i in range(depth):
        g_pre, g_post = norm_pre[i], norm_post[i]
        x2 = _ffn(x2, g_pre[0], g_post[0], ffn_wg[i, 0], ffn_wu[i, 0], ffn_wd[i, 0])
        if i % 2 == 0:
            e = i // 2
            x2 = _even_mixer(x2, g_pre[1], g_post[1], ev_w_in[e], ev_w_out[e], ev_cmp_pe[e], ev_cmp_w[e],
                             rel_bias_table, b, s)
        else:
            o = i // 2
            lam_init = 0.8 - 0.6 * math.exp(-0.3 * i)
            x2 = _odd_mixer(x2, g_pre[1], g_post[1], od_w_in[o], od_w_out[o], od_q_norm[o], od_kv_norm[o],
                            od_w_uq[o], od_w_ukv[o], od_lambda[o], od_subln[o], rel_bias_table, lam_init, b, s)
        (kv,) = _proj(mem2, mem_norm[i], [xa_wkv[i]], [jnp.zeros((1, xa_wkv.shape[-1]), F32)], [BF16])
        x3 = _mem_xattn(x2.reshape(b, s, d), kv.reshape(b, -1, kv.shape[-1]), g_pre[2], g_post[2],
                        xa_wq[i], xa_wo[i])
        x2 = _ffn(x3.reshape(b * s, d), g_pre[3], g_post[3], ffn_wg[i, 1], ffn_wu[i, 1], ffn_wd[i, 1])
    return x2.reshape(b, s, d)
```

```python
import functools
import math

import numpy as np
import jax
import jax.numpy as jnp
from jax import lax
from jax.experimental import pallas as pl
from jax.experimental.pallas import tpu as pltpu

F32, BF16, I32 = jnp.float32, jnp.bfloat16, jnp.int32

N_BUCKETS = 32
MAX_DISTANCE = 128
RMS_EPS = 1e-6
ROPE_THETA = 10000.0
NEG_INF = -1e30
A_HEADS, A_KV, A_DH = 8, 2, 64
A_GRP = A_HEADS // A_KV
CMP_LEN, CMP_STRIDE, SLC_LEN, SLC_TOP, WIN = 32, 16, 64, 16, 512
FORCE_BONUS = 1e3
B_HEADS, B_DH, IDX_HEADS, IDX_DH, IDX_TOPK_MAX = 8, 64, 8, 32, 256
C_HEADS, C_Q_RANK, C_KV_RANK, C_DN, C_DR, C_DV = 8, 256, 128, 64, 32, 64
D_HEADS, D_DH, D_DV = 4, 64, 128
X_HEADS, X_DH = 4, 128
EV_SIZES = (A_HEADS * A_DH, 6 * A_KV * A_DH, 3 * A_HEADS,
            B_HEADS * B_DH, B_DH, B_DH, IDX_HEADS * IDX_DH, IDX_DH, IDX_HEADS)
OD_SIZES = (C_Q_RANK, C_KV_RANK, C_DR, D_HEADS * 2 * D_DH, D_HEADS * 2 * D_DH, D_HEADS * D_DV)

LANES = 128
VMEM_LIMIT_BYTES = 56 * 1024 * 1024
ONES_LANE = 64
INT_MIN = -2 ** 31
LOG2E = 1.4426950408889634

TOK_TILE = 512
FFN_CHUNK = 256
SPARSE_TQ = 128
WIDE_TILES = 4
DENSE_TQ = 512
MLA_HEADS_PER_STEP = 2


def _params(*sem):
    return pltpu.CompilerParams(dimension_semantics=sem, vmem_limit_bytes=VMEM_LIMIT_BYTES)


def _const_spec(shape):
    nd = len(shape)
    return pl.BlockSpec(shape, lambda *_: (0,) * nd, pipeline_mode=pl.Buffered(1))


def _dot(a, b):
    return jnp.dot(a, b, preferred_element_type=F32)


def _dot_t(a, b):
    return lax.dot_general(a, b, (((1,), (1,)), ((), ())), preferred_element_type=F32)


def _rms(x, g):
    return x * lax.rsqrt(jnp.mean(x * x, axis=-1, keepdims=True) + RMS_EPS) * g


def _sigmoid(x):
    return 1.0 / (1.0 + jnp.exp(-x))


def _online_step(q, k, v, carry, add=None, shared=None, heads=1):
    m, acc = carry
    s = _dot_t(q, k)
    if add is not None:
        s = s + add
    if shared is not None:
        s = (s.reshape(heads, shared.shape[0], shared.shape[1]) + shared[None]).reshape(s.shape)
    m_new = jnp.maximum(m, jnp.max(s, axis=-1, keepdims=True))
    p = jnp.exp2(s - m_new)
    acc = jnp.exp2(m - m_new) * acc + _dot(p.astype(BF16), v)
    return m_new, acc


def _flash_init(m_rows, n_lanes):
    return jnp.full((m_rows, 1), -jnp.inf, F32), jnp.zeros((m_rows, n_lanes), F32)


def _ffn_body(x_ref, gpre_ref, gpost_ref, wgu_ref, wd_ref, o_ref, *, n_chunks, fc):
    x = x_ref[...]
    h = _rms(x, gpre_ref[...]).astype(BF16)
    acc = jnp.zeros(x.shape, F32)
    for c in range(n_chunks):
        gu = _dot(h, wgu_ref[:, c * 2 * fc:(c + 1) * 2 * fc])
        g, u = gu[:, :fc], gu[:, fc:]
        a = g * _sigmoid(g) * u
        acc = acc + _dot(a.astype(BF16), wd_ref[c * fc:(c + 1) * fc, :])
    o_ref[...] = x + 0.5 * _rms(acc, gpost_ref[...])


def _ffn(x, g_pre, g_post, wg, wu, wd):
    n, d = x.shape
    f = wg.shape[1]
    fc = FFN_CHUNK
    n_chunks = f // fc
    assert n_chunks * fc == f and n % TOK_TILE == 0
    wgu = jnp.concatenate([wg.reshape(d, n_chunks, fc), wu.reshape(d, n_chunks, fc)], axis=-1)
    wgu = wgu.reshape(d, 2 * f).astype(BF16)
    return pl.pallas_call(
        functools.partial(_ffn_body, n_chunks=n_chunks, fc=fc),
        grid=(n // TOK_TILE,),
        in_specs=[pl.BlockSpec((TOK_TILE, d), lambda i: (i, 0)),
                  _const_spec((1, d)), _const_spec((1, d)),
                  _const_spec((d, 2 * f)), _const_spec((f, d))],
        out_specs=pl.BlockSpec((TOK_TILE, d), lambda i: (i, 0)),
        out_shape=jax.ShapeDtypeStruct((n, d), F32),
        compiler_params=_params("arbitrary"),
        name="ffn",
    )(x, g_pre.reshape(1, d), g_post.reshape(1, d), wgu, wd.astype(BF16))


def _proj_body(x_ref, g_ref, w_ref, c_ref, *o_refs, bounds):
    h = _rms(x_ref[...], g_ref[...]).astype(BF16)
    for o_ref, (a, b) in zip(o_refs, bounds):
        o_ref[...] = (_dot(h, w_ref[:, a:b]) + c_ref[:, a:b]).astype(o_ref.dtype)


def _proj(x, g, w_segs, c_segs, dtypes):
    n, d = x.shape
    widths = [w.shape[1] for w in w_segs]
    assert all(wd_ % LANES == 0 for wd_ in widths) and n % TOK_TILE == 0
    offs = np.concatenate([[0], np.cumsum(widths)])
    bounds = tuple((int(offs[i]), int(offs[i + 1])) for i in range(len(widths)))
    w = jnp.concatenate(w_segs, axis=1).astype(BF16)
    c = jnp.concatenate(c_segs, axis=1).astype(F32)
    tot = int(offs[-1])
    return pl.pallas_call(
        functools.partial(_proj_body, bounds=bounds),
        grid=(n // TOK_TILE,),
        in_specs=[pl.BlockSpec((TOK_TILE, d), lambda i: (i, 0)),
                  _const_spec((1, d)), _const_spec((d, tot)), _const_spec((1, tot))],
        out_specs=[pl.BlockSpec((TOK_TILE, wd_), lambda i: (i, 0)) for wd_ in widths],
        out_shape=[jax.ShapeDtypeStruct((n, wd_), dt) for wd_, dt in zip(widths, dtypes)],
        compiler_params=_params("arbitrary"),
        name="norm_proj",
    )(x, g.reshape(1, d), w, c)


def _out_body(a_ref, b_ref, wa_ref, wb_ref, x_ref, g_ref, o_ref):
    y = _dot(a_ref[...], wa_ref[...]) + _dot(b_ref[...], wb_ref[...])
    o_ref[...] = x_ref[...] + _rms(y, g_ref[...])


def _out_proj(oa, ob, wa, wb, x, g_post):
    n, d = x.shape
    ka, kb = oa.shape[1], ob.shape[1]
    return pl.pallas_call(
        _out_body,
        grid=(n // TOK_TILE,),
        in_specs=[pl.BlockSpec((TOK_TILE, ka), lambda i: (i, 0)),
                  pl.BlockSpec((TOK_TILE, kb), lambda i: (i, 0)),
                  _const_spec((ka, d)), _const_spec((kb, d)),
                  pl.BlockSpec((TOK_TILE, d), lambda i: (i, 0)),
                  _const_spec((1, d))],
        out_specs=pl.BlockSpec((TOK_TILE, d), lambda i: (i, 0)),
        out_shape=jax.ShapeDtypeStruct((n, d), F32),
        compiler_params=_params("arbitrary"),
        name="out_proj",
    )(oa, ob, wa.astype(BF16), wb.astype(BF16), x, g_post.reshape(1, d))


def _xattn_body(x_ref, gpre_ref, gpost_ref, wq_ref, kv_ref, wo_ref, o_ref, *, heads, dh):
    x = x_ref[0]
    h = _rms(x, gpre_ref[...]).astype(BF16)
    q = _dot(h, wq_ref[...]).astype(BF16)
    kv = kv_ref[0]
    scale = dh ** -0.5
    outs = []
    for hh in range(heads):
        k = kv[:, hh * dh:(hh + 1) * dh]
        v = kv[:, (heads + hh) * dh:(heads + hh + 1) * dh]
        s = _dot_t(q[:, hh * dh:(hh + 1) * dh], k) * scale
        p = jnp.exp(s - jnp.max(s, axis=-1, keepdims=True))
        p = p / jnp.sum(p, axis=-1, keepdims=True)
        outs.append(_dot(p.astype(BF16), v).astype(BF16))
    y = _dot(jnp.concatenate(outs, axis=-1), wo_ref[...])
    o_ref[0] = x + _rms(y, gpost_ref[...])


def _mem_xattn(x, kv, g_pre, g_post, wq, wo):
    b, s, d = x.shape
    mlen, kvw = kv.shape[1:]
    hd = X_HEADS * X_DH
    return pl.pallas_call(
        functools.partial(_xattn_body, heads=X_HEADS, dh=X_DH),
        grid=(b, s // TOK_TILE),
        in_specs=[pl.BlockSpec((1, TOK_TILE, d), lambda i, j: (i, j, 0)),
                  _const_spec((1, d)), _const_spec((1, d)), _const_spec((d, hd)),
                  pl.BlockSpec((1, mlen, kvw), lambda i, j: (i, 0, 0)),
                  _const_spec((hd, d))],
        out_specs=pl.BlockSpec((1, TOK_TILE, d), lambda i, j: (i, j, 0)),
        out_shape=jax.ShapeDtypeStruct((b, s, d), F32),
        compiler_params=_params("arbitrary", "arbitrary"),
        name="mem_xattn",
    )(x, g_pre.reshape(1, d), g_post.reshape(1, d), wq.astype(BF16), kv, wo.astype(BF16))


def _bucket_np(dist):
    max_exact = N_BUCKETS // 2
    d = np.maximum(dist, 0)
    ratio = np.log(np.maximum(d, 1).astype(np.float32) / np.float32(max_exact)) / np.float32(
        math.log(MAX_DISTANCE / max_exact))
    large = np.minimum(max_exact + (ratio * np.float32(N_BUCKETS - max_exact)).astype(np.int32),
                       N_BUCKETS - 1)
    return np.where(d < max_exact, d, large).astype(np.int32)


def _rel_bias(tab, dist, visible, rel_to_far):
    dist = np.maximum(np.asarray(dist), 0).astype(np.int32)
    bk = _bucket_np(np.arange(max(int(dist.max()), 4 * MAX_DISTANCE) + 1))
    assert (np.diff(bk) >= 0).all() and bk[-1] == N_BUCKETS - 1
    tab = tab.astype(F32)
    col = lambda b: tab[b].reshape((-1,) + (1,) * dist.ndim)
    d = jnp.asarray(dist)[None]
    out = jnp.broadcast_to(col(0), (tab.shape[1],) + dist.shape)
    for b in range(1, N_BUCKETS):
        first = np.nonzero(bk == b)[0]
        if first.size:
            out = jnp.where(d >= int(first[0]), col(b), out)
    if rel_to_far:
        out = out - col(N_BUCKETS - 1)
    return jnp.where(jnp.asarray(np.asarray(visible))[None], out * LOG2E, NEG_INF)


def _far_distance():
    bk = _bucket_np(np.arange(4 * MAX_DISTANCE))
    return int(np.nonzero(bk == N_BUCKETS - 1)[0][0])


def _bias_tiles(tab, t):
    assert t + 1 >= _far_distance()
    r = np.arange(t)[:, None]
    c = np.arange(t)[None, :]
    dist = np.stack([r - c, t + r - c, 2 * t + r - c])
    everywhere = np.ones((t, t), bool)
    return _rel_bias(tab, dist, np.stack([r >= c, everywhere, everywhere]), True)


def _key_loops(qt, wide, wide_step, narrow_step, init):
    n_wide = jnp.maximum(qt - 1, 0) // wide
    carry = lax.fori_loop(0, n_wide, wide_step, init)
    return lax.fori_loop(n_wide * wide, qt + 1, narrow_step, carry)


def _cmp_body(x_ref, pe_ref, w_ref, o_ref):
    x = x_ref[0]
    xn = pltpu.roll(x, x.shape[0] - 1, 0)
    lo = (x + pe_ref[0:1, :]).astype(BF16)
    hi = (xn + pe_ref[1:2, :]).astype(BF16)
    o_ref[0] = (_dot(lo, w_ref[0]) + _dot(hi, w_ref[1])).astype(BF16)


def _nsa_compress(kcvc, cmp_pe, cmp_w, b, s):
    nb = s // CMP_STRIDE
    halves = CMP_LEN // CMP_STRIDE
    n_grp = 2 * A_KV
    kdim = CMP_STRIDE * n_grp * A_DH
    cw = cmp_w.reshape(2, halves, CMP_STRIDE, A_DH, A_DH)
    same = jnp.eye(n_grp, dtype=F32).reshape(2, A_KV, 2, A_KV)
    w = jnp.einsum("khlde,kgKG->hlkgdKGe", cw, same)
    w = jnp.pad(w, [(0, 0)] * 7 + [(0, LANES - A_DH)]).reshape(halves, kdim, n_grp * LANES).astype(BF16)
    pe = cmp_pe.reshape(2, halves, CMP_STRIDE, 1, A_DH).transpose(1, 2, 0, 3, 4)
    pe = jnp.broadcast_to(pe, (halves, CMP_STRIDE, 2, A_KV, A_DH)).reshape(halves, kdim)
    return pl.pallas_call(
        _cmp_body,
        grid=(b,),
        in_specs=[pl.BlockSpec((1, nb, kdim), lambda i: (i, 0, 0)),
                  _const_spec((halves, kdim)), _const_spec((halves, kdim, n_grp * LANES))],
        out_specs=pl.BlockSpec((1, nb, n_grp * LANES), lambda i: (i, 0, 0)),
        out_shape=jax.ShapeDtypeStruct((b, nb, n_grp * LANES), BF16),
        compiler_params=_params("arbitrary"),
        name="nsa_compress",
    )(kcvc.reshape(b, nb, kdim), pe, w)


def _nsa_body(q_ref, kc_ref, vc_ref, bc_ref, ks_ref, vs_ref, kw_ref, vw_ref, ga_ref, bias_ref,
              wbias_ref, ovl_ref, exp_ref, o_ref, mask_ref, *, tq, seq, n_sb):
    qt = pl.program_id(2)
    rr = A_GRP
    m_rows = rr * tq
    ncp = kc_ref.shape[1]
    q = q_ref[0]
    qs = jnp.concatenate([q[:, r * LANES:(r + 1) * LANES] for r in range(rr)], axis=0)

    s_c = _dot_t(qs, kc_ref[0]) + bc_ref[...].reshape(m_rows, ncp)
    p = jnp.where(s_c > 0.5 * NEG_INF, jnp.exp2(s_c - jnp.max(s_c, axis=-1, keepdims=True)), 0.0)
    l = jnp.sum(p, axis=-1, keepdims=True)
    pn = p * jnp.where(l > 0.0, 1.0 / l, 0.0)
    o_c = _dot(pn.astype(BF16), vc_ref[0])

    imp = jnp.sum(pn.reshape(rr, tq, ncp), axis=0)
    imp_t = lax.dot_general(ovl_ref[...], imp, (((1,), (1,)), ((), ())),
                            precision=lax.Precision.HIGHEST, preferred_element_type=F32)
    blk = lax.broadcasted_iota(I32, (LANES, tq), 0)
    cur = (qt * tq + lax.broadcasted_iota(I32, (LANES, tq), 1)) // SLC_LEN
    forced = (blk == 0) | (blk == cur) | (blk == cur - 1)
    score = jnp.where(blk <= cur, imp_t + jnp.where(forced, FORCE_BONUS, 0.0), NEG_INF)
    groups = [score[8 * v:8 * v + 8] for v in range(-(-n_sb // 8))]
    ranks = [jnp.zeros((8, tq), F32) for _ in groups]
    sub = lax.broadcasted_iota(I32, (8, tq), 0)
    for i in range(n_sb):
        ci = score[i:i + 1, :]
        for v, sv in enumerate(groups):
            ge = jnp.where(ci >= sv, 1.0, 0.0)
            gt = jnp.where(ci > sv, 1.0, 0.0)
            if 8 * v > i:
                ranks[v] = ranks[v] + ge
            elif 8 * v + 7 < i:
                ranks[v] = ranks[v] + gt
            else:
                ranks[v] = ranks[v] + jnp.where(sub + 8 * v > i, ge, gt)
    rank = jnp.concatenate(ranks + [jnp.full((LANES - 8 * len(groups), tq), float(LANES), F32)], axis=0)
    sel_t = jnp.where(rank < float(min(SLC_TOP, n_sb)), jnp.where(blk <= cur, 1.0, 0.0), 0.0)
    selb = sel_t.T.astype(BF16)

    def expand(c, _):
        cs = pl.multiple_of(c * tw, tw)
        mask_ref[:, pl.ds(cs, tw)] = (_dot(selb, exp_ref[:, pl.ds(cs, tw)]) - 1.0) * (-NEG_INF)
        return 0

    tw = WIDE_TILES * tq
    lax.fori_loop(0, qt // WIDE_TILES + 1, expand, 0)

    def slc_wide(kw, carry):
        ks = pl.multiple_of(kw * tw, tw)
        return _online_step(qs, ks_ref[0, pl.ds(ks, tw), :], vs_ref[0, pl.ds(ks, tw), :], carry,
                            shared=mask_ref[:, pl.ds(ks, tw)], heads=rr)

    def slc(kt, carry):
        ks = pl.multiple_of(kt * tq, tq)
        return _online_step(qs, ks_ref[0, pl.ds(ks, tq), :], vs_ref[0, pl.ds(ks, tq), :], carry,
                            add=bias_ref[0, jnp.minimum(qt - kt, 2)], shared=mask_ref[:, pl.ds(ks, tq)],
                            heads=rr)

    _, acc_s = _key_loops(qt, WIDE_TILES, slc_wide, slc, _flash_init(m_rows, LANES))

    n_woff = WIN // tq

    def win(kt, carry):
        ks = pl.multiple_of(kt * tq, tq)
        return _online_step(qs, kw_ref[0, pl.ds(ks, tq), :], vw_ref[0, pl.ds(ks, tq), :], carry,
                            add=bias_ref[0, jnp.minimum(qt - kt, 2)])

    def win_full():
        ks = pl.multiple_of((qt - n_woff) * tq, tq)
        return _online_step(qs, kw_ref[0, pl.ds(ks, WIN), :], vw_ref[0, pl.ds(ks, WIN), :],
                            _flash_init(m_rows, LANES), add=wbias_ref[0])

    def win_head():
        return lax.fori_loop(0, qt, win, _flash_init(m_rows, LANES))

    _, acc_w = win(qt, lax.cond(qt >= n_woff, win_full, win_head))

    gs = _sigmoid(ga_ref[0])
    for r in range(rr):
        a_s = acc_s[r * tq:(r + 1) * tq]
        a_w = acc_w[r * tq:(r + 1) * tq]
        o = (gs[:, 3 * r:3 * r + 1] * o_c[r * tq:(r + 1) * tq]
             + gs[:, 3 * r + 1:3 * r + 2] * (a_s / a_s[:, ONES_LANE:ONES_LANE + 1])
             + gs[:, 3 * r + 2:3 * r + 3] * (a_w / a_w[:, ONES_LANE:ONES_LANE + 1]))
        o_ref[0, :, r * LANES:(r + 1) * LANES] = o.astype(BF16)


def _nsa(q, kvc, kvsw, ga, tab, b, s):
    tq = SPARSE_TQ
    ncp = s // CMP_STRIDE
    n_sb = s // SLC_LEN
    assert n_sb <= LANES and WIN % tq == 0 and s % (WIDE_TILES * tq) == 0
    t = np.arange(s)[:, None]
    cmp_end = np.arange(ncp)[None, :] * CMP_STRIDE + CMP_LEN - 1
    bias_c = _rel_bias(tab, t - cmp_end, cmp_end <= t, False)
    tiles = _bias_tiles(tab, tq)
    r = np.arange(tq)[:, None]
    c = np.arange(tq)[None, :]
    oldest = jnp.broadcast_to(jnp.asarray(np.where(r < c, 0.0, NEG_INF).astype(np.float32)), (A_HEADS, tq, tq))
    n_woff = WIN // tq
    wbias = jnp.concatenate([oldest] + [tiles[:, 2]] * (n_woff - 2) + [tiles[:, 1]], axis=-1)
    wbias = wbias.reshape(A_KV, A_GRP * tq, WIN)
    tiles = tiles.reshape(A_KV, A_GRP, 3, tq, tq).transpose(0, 2, 1, 3, 4).reshape(A_KV, 3, A_GRP * tq, tq)
    cmp_start = np.arange(ncp)[None, :] * CMP_STRIDE
    sb_start = np.arange(LANES)[:, None] * SLC_LEN
    ovl = ((cmp_start < sb_start + SLC_LEN) & (cmp_start + CMP_LEN - 1 >= sb_start)
           & (np.arange(ncp)[None, :] < ncp - 1) & (np.arange(LANES)[:, None] < n_sb))
    ovl = jnp.asarray(ovl.astype(np.float32))
    expand = jnp.asarray((np.arange(LANES)[:, None] == np.arange(s)[None, :] // SLC_LEN).astype(np.float32), BF16)
    kv_spec = lambda blk: pl.BlockSpec((1, s, LANES), lambda i, g, j: (i, 0, 2 * blk + g))
    return pl.pallas_call(
        functools.partial(_nsa_body, tq=tq, seq=s, n_sb=n_sb),
        grid=(b, A_KV, s // tq),
        in_specs=[pl.BlockSpec((1, tq, A_GRP * LANES), lambda i, g, j: (i, j, g)),
                  pl.BlockSpec((1, ncp, LANES), lambda i, g, j: (i, 0, g)),
                  pl.BlockSpec((1, ncp, LANES), lambda i, g, j: (i, 0, A_KV + g)),
                  pl.BlockSpec((A_GRP, tq, ncp), lambda i, g, j: (g, j, 0)),
                  kv_spec(0), kv_spec(1), kv_spec(2), kv_spec(3),
                  pl.BlockSpec((1, tq, LANES), lambda i, g, j: (i, j, g)),
                  pl.BlockSpec((1, 3, A_GRP * tq, tq), lambda i, g, j: (g, 0, 0, 0)),
                  pl.BlockSpec((1, A_GRP * tq, WIN), lambda i, g, j: (g, 0, 0)),
                  _const_spec((LANES, ncp)), _const_spec((LANES, s))],
        out_specs=pl.BlockSpec((1, tq, A_GRP * LANES), lambda i, g, j: (i, j, g)),
        out_shape=jax.ShapeDtypeStruct((b, s, A_HEADS * LANES), BF16),
        scratch_shapes=[pltpu.VMEM((tq, s), F32)],
        compiler_params=_params("arbitrary", "arbitrary", "arbitrary"),
        name="nsa_attn",
    )(q, kvc, kvc, bias_c, kvsw, kvsw, kvsw, kvsw, ga, tiles, wbias, ovl, expand)


def _dsa_body(q_ref, k_ref, v_ref, iq_ref, ik_ref, iw_ref, bias_ref, o_ref, keys_ref, wb_ref, *,
              tq, seq, k_top):
    qt = pl.program_id(1)
    nh = B_HEADS
    m_rows = nh * tq
    tw = WIDE_TILES * tq
    n_wide = qt // WIDE_TILES + 1
    col = lax.broadcasted_iota(I32, (tq, tq), 1)
    row_w = lax.broadcasted_iota(I32, (tq, tw), 0)
    col_w = lax.broadcasted_iota(I32, (tq, tw), 1)

    iq = iq_ref[0]
    iqs = jnp.concatenate([iq[:, h * LANES:(h + 1) * LANES] for h in range(IDX_HEADS)], axis=0)
    iw = iw_ref[0]
    for h in range(IDX_HEADS):
        wb_ref[h * tq:(h + 1) * tq, :] = jnp.broadcast_to(iw[:, h:h + 1], (tq, LANES))

    def scores(kw, _):
        ks = pl.multiple_of(kw * tw, tw)
        lg = _dot_t(iqs, ik_ref[0, pl.ds(ks, tw), :])
        w = wb_ref[...]
        parts = []
        for j in range(tw // LANES):
            part = jnp.maximum(lg[:, j * LANES:(j + 1) * LANES], 0.0) * w
            parts.append(jnp.sum(part.reshape(IDX_HEADS, tq, LANES), axis=0))
        sc = jnp.concatenate(parts, axis=1)
        bits = lax.bitcast_convert_type(sc, I32)
        key = jnp.where(bits < 0, bits ^ 0x7FFFFFFF, bits)
        vis = (col_w + kw * tw) <= (row_w + qt * tq)
        keys_ref[:, pl.ds(ks, tw)] = jnp.where(vis, key, INT_MIN)
        return 0

    lax.fori_loop(0, n_wide, scores, 0)

    def count(ind):
        def body(kw, acc):
            ks = pl.multiple_of(kw * tw, tw)
            x = ind(keys_ref[:, pl.ds(ks, tw)], kw)
            for j in range(tw // LANES):
                acc = acc + x[:, j * LANES:(j + 1) * LANES]
            return acc
        acc = lax.fori_loop(0, n_wide, body, jnp.zeros((tq, LANES), F32))
        return jnp.sum(acc, axis=-1, keepdims=True)

    kf = float(k_top)
    c0 = count(lambda t, kw: jnp.where(t >= 0, 1.0, 0.0))
    thr = jnp.where(c0 >= kf, 0, INT_MIN).astype(I32)
    c_ge = jnp.where(c0 >= kf, c0, float(seq))

    def bisect(i, state):
        thr, c_ge = state
        cand = thr | (jnp.int32(1) << (30 - i))
        c = count(lambda t, kw: jnp.where(t >= cand, 1.0, 0.0))
        return jnp.where(c >= kf, cand, thr), jnp.where(c >= kf, c, c_ge)

    thr, c_ge = lax.fori_loop(0, 31, bisect, (thr, c_ge))

    n_bits = int(math.log2(seq))
    assert 2 ** n_bits == seq

    def cutoff():
        need = kf - count(lambda t, kw: jnp.where(t > thr, 1.0, 0.0))

        def step(i, cut):
            cand = cut | (jnp.int32(1) << (n_bits - 1 - i))
            c = count(lambda t, kw: jnp.where(t == thr, jnp.where(col_w + kw * tw < cand, 1.0, 0.0), 0.0))
            return jnp.where(c < need, cand, cut)
        return lax.fori_loop(0, n_bits, step, jnp.zeros((tq, 1), I32))

    excess = jnp.max(jnp.where(c_ge > kf, 1.0, 0.0)) > 0.5
    cut = lax.cond(excess, cutoff, lambda: jnp.full((tq, 1), seq, I32))

    q = q_ref[0]
    qs = jnp.concatenate([q[:, h * LANES:(h + 1) * LANES] for h in range(nh)], axis=0)

    def sel_add(key, idx):
        tie = jnp.where(idx <= cut, 0.0, NEG_INF)
        return jnp.where(key > thr, 0.0, jnp.where(key == thr, tie, NEG_INF))

    def attend_wide(kw, carry):
        ks = pl.multiple_of(kw * tw, tw)
        sel = sel_add(keys_ref[:, pl.ds(ks, tw)], col_w + kw * tw)
        return _online_step(qs, k_ref[0, pl.ds(ks, tw), :], v_ref[0, pl.ds(ks, tw), :], carry,
                            shared=sel, heads=nh)

    def attend(kt, carry):
        ks = pl.multiple_of(kt * tq, tq)
        sel = sel_add(keys_ref[:, pl.ds(ks, tq)], col + kt * tq)
        return _online_step(qs, k_ref[0, pl.ds(ks, tq), :], v_ref[0, pl.ds(ks, tq), :], carry,
                            add=bias_ref[jnp.minimum(qt - kt, 2)], shared=sel, heads=nh)

    _, acc = _key_loops(qt, WIDE_TILES, attend_wide, attend, _flash_init(m_rows, LANES))
    o = acc / acc[:, ONES_LANE:ONES_LANE + 1]
    for h in range(nh):
        o_ref[0, :, h * LANES:(h + 1) * LANES] = o[h * tq:(h + 1) * tq].astype(BF16)


def _dsa(q, kbvb, iq, ik, iw, tab, b, s):
    tq = SPARSE_TQ
    k_top = min(IDX_TOPK_MAX, s // 4)
    assert tq == LANES and s % (WIDE_TILES * tq) == 0
    tiles = _bias_tiles(tab, tq).transpose(1, 0, 2, 3).reshape(3, B_HEADS * tq, tq)
    return pl.pallas_call(
        functools.partial(_dsa_body, tq=tq, seq=s, k_top=k_top),
        grid=(b, s // tq),
        in_specs=[pl.BlockSpec((1, tq, B_HEADS * LANES), lambda i, j: (i, j, 0)),
                  pl.BlockSpec((1, s, LANES), lambda i, j: (i, 0, 0)),
                  pl.BlockSpec((1, s, LANES), lambda i, j: (i, 0, 1)),
                  pl.BlockSpec((1, tq, IDX_HEADS * LANES), lambda i, j: (i, j, 0)),
                  pl.BlockSpec((1, s, LANES), lambda i, j: (i, 0, 0)),
                  pl.BlockSpec((1, tq, LANES), lambda i, j: (i, j, 0)),
                  _const_spec((3, B_HEADS * tq, tq))],
        out_specs=pl.BlockSpec((1, tq, B_HEADS * LANES), lambda i, j: (i, j, 0)),
        out_shape=jax.ShapeDtypeStruct((b, s, B_HEADS * LANES), BF16),
        scratch_shapes=[pltpu.VMEM((tq, s), I32), pltpu.VMEM((IDX_HEADS * tq, LANES), F32)],
        compiler_params=_params("arbitrary", "arbitrary"),
        name="dsa_attn",
    )(q, kbvb, kbvb, iq, ik, iw, tiles)


def _mla_prep_body(cq_ref, ckv_ref, kr_ref, qn_ref, kvn_ref, wq1_ref, wq2_ref, wk_ref, wv_ref,
                   cos_ref, sin_ref, ones_ref, q_o, k_o, v_o):
    cqn = _rms(cq_ref[...], qn_ref[...]).astype(BF16)
    ckn = _rms(ckv_ref[...], kvn_ref[...]).astype(BF16)
    cos, sin = cos_ref[...], sin_ref[...]
    cos_h = jnp.concatenate([cos] * C_HEADS, axis=1)
    sin_h = jnp.concatenate([sin] * C_HEADS, axis=1)
    q_o[...] = (_dot(cqn, wq1_ref[...]) * cos_h + _dot(cqn, wq2_ref[...]) * sin_h).astype(BF16)
    kr = kr_ref[...]
    kr_rot = kr[:, :LANES] * cos + kr[:, LANES:] * sin
    k_o[...] = (_dot(ckn, wk_ref[...]) + jnp.concatenate([kr_rot] * C_HEADS, axis=1)).astype(BF16)
    v_o[...] = (_dot(ckn, wv_ref[...]) + ones_ref[...]).astype(BF16)


def _rot_cols(w):
    half = C_DR // 2
    return jnp.concatenate([-w[..., half:], w[..., :half]], axis=-1)


def _mla_prep(cq, ckv, kr, q_norm, kv_norm, w_uq, w_ukv, s):
    n = cq.shape[0]
    hw = C_HEADS * LANES
    padq = LANES - C_DN - C_DR
    wq = w_uq.reshape(C_Q_RANK, C_HEADS, C_DN + C_DR) * ((C_DN + C_DR) ** -0.5 * LOG2E)
    wq1 = jnp.pad(wq, ((0, 0), (0, 0), (0, padq))).reshape(C_Q_RANK, hw)
    wq2 = jnp.pad(_rot_cols(wq[..., C_DN:]), ((0, 0), (0, 0), (C_DN, padq))).reshape(C_Q_RANK, hw)
    wkv = w_ukv.reshape(C_KV_RANK, C_HEADS, C_DN + C_DV)
    wk = jnp.pad(wkv[..., :C_DN], ((0, 0), (0, 0), (0, LANES - C_DN))).reshape(C_KV_RANK, hw)
    wv = jnp.pad(wkv[..., C_DN:], ((0, 0), (0, 0), (0, LANES - C_DV))).reshape(C_KV_RANK, hw)
    half = C_DR // 2
    inv = ROPE_THETA ** (-jnp.arange(half, dtype=F32) / half)
    ang = jnp.arange(s, dtype=F32)[:, None] * inv[None, :]
    cos, sin = jnp.cos(ang), jnp.sin(ang)
    cos_t = jnp.concatenate([jnp.ones((s, C_DN), F32), cos, cos, jnp.zeros((s, padq), F32)], axis=1)
    sin_t = jnp.concatenate([jnp.zeros((s, C_DN), F32), sin, sin, jnp.zeros((s, padq), F32)], axis=1)
    ones = jnp.tile(jnp.asarray(np.arange(LANES) == ONES_LANE, F32), C_HEADS).reshape(1, hw)
    tile_tok = lambda w_: pl.BlockSpec((TOK_TILE, w_), lambda i: (i, 0))
    n_pos = s // TOK_TILE
    pos_spec = pl.BlockSpec((TOK_TILE, LANES), lambda i: (i % n_pos, 0))
    return pl.pallas_call(
        _mla_prep_body,
        grid=(n // TOK_TILE,),
        in_specs=[tile_tok(C_Q_RANK), tile_tok(C_KV_RANK), tile_tok(2 * LANES),
                  _const_spec((1, C_Q_RANK)), _const_spec((1, C_KV_RANK)),
                  _const_spec((C_Q_RANK, hw)), _const_spec((C_Q_RANK, hw)),
                  _const_spec((C_KV_RANK, hw)), _const_spec((C_KV_RANK, hw)),
                  pos_spec, pos_spec, _const_spec((1, hw))],
        out_specs=[tile_tok(hw)] * 3,
        out_shape=[jax.ShapeDtypeStruct((n, hw), BF16)] * 3,
        compiler_params=_params("arbitrary"),
        name="mla_prep",
    )(cq, ckv, kr, q_norm.reshape(1, -1), kv_norm.reshape(1, -1), wq1.astype(BF16), wq2.astype(BF16),
      wk.astype(BF16), wv.astype(BF16), cos_t, sin_t, ones)


def _mla_attn_body(q_ref, k_ref, v_ref, o_ref, *, tq, nh):
    qt = pl.program_id(2)
    q = q_ref[0]
    lanes = lambda a, h: a[:, h * LANES:(h + 1) * LANES]

    def step(kt, carry, add):
        ks = pl.multiple_of(kt * tq, tq)
        k = k_ref[0, pl.ds(ks, tq), :]
        v = v_ref[0, pl.ds(ks, tq), :]
        return tuple(_online_step(lanes(q, h), lanes(k, h), lanes(v, h), carry[h], add=add)
                     for h in range(nh))

    init = tuple(_flash_init(tq, LANES) for _ in range(nh))
    carry = lax.fori_loop(0, qt, lambda kt, c: step(kt, c, None), init)
    row = lax.broadcasted_iota(I32, (tq, tq), 0)
    col = lax.broadcasted_iota(I32, (tq, tq), 1)
    carry = step(qt, carry, jnp.where(row >= col, 0.0, NEG_INF))
    for h in range(nh):
        acc = carry[h][1]
        o_ref[0, :, h * LANES:(h + 1) * LANES] = (acc / acc[:, ONES_LANE:ONES_LANE + 1]).astype(BF16)


def _mla_attn(q, k, v, b, s):
    tq = DENSE_TQ
    nh = MLA_HEADS_PER_STEP
    spec_q = pl.BlockSpec((1, tq, nh * LANES), lambda i, h, j: (i, j, h))
    spec_kv = pl.BlockSpec((1, s, nh * LANES), lambda i, h, j: (i, 0, h))
    return pl.pallas_call(
        functools.partial(_mla_attn_body, tq=tq, nh=nh),
        grid=(b, C_HEADS // nh, s // tq),
        in_specs=[spec_q, spec_kv, spec_kv],
        out_specs=spec_q,
        out_shape=jax.ShapeDtypeStruct((b, s, C_HEADS * LANES), BF16),
        compiler_params=_params("arbitrary", "arbitrary", "arbitrary"),
        name="mla_attn",
    )(q, k, v)


def _diff_body(lam_ref, q_ref, k_ref, v_ref, bias_ref, g_ref, o_ref, *, tq, out_scale):
    qt = pl.program_id(2)
    q = q_ref[0]
    lanes = lambda a, e: a[:, e * LANES:(e + 1) * LANES]

    def step(kt, carry, near):
        ks = pl.multiple_of(kt * tq, tq)
        k = k_ref[0, pl.ds(ks, tq), :]
        v = v_ref[0, pl.ds(ks, tq), :]
        add = bias_ref[0, qt - kt] if near else None
        return tuple(_online_step(lanes(q, e), lanes(k, e), v, carry[e], add=add) for e in range(2))

    n_far = jnp.maximum(qt - 1, 0)
    init = tuple(_flash_init(tq, 2 * LANES) for _ in range(2))
    carry = lax.fori_loop(0, n_far, functools.partial(step, near=False), init)
    carry = lax.fori_loop(n_far, qt + 1, functools.partial(step, near=True), carry)
    outs = [carry[e][1][:, :D_DV] / carry[e][1][:, D_DV:D_DV + 1] for e in range(2)]
    o = outs[0] - lam_ref[0] * outs[1]
    o_ref[0] = (_rms(o, g_ref[...]) * out_scale).astype(BF16)


def _diff_attn(q, k, v, tab, lam_full, subln, lam_init, b, s):
    tq = DENSE_TQ
    tiles = _bias_tiles(tab, tq)[:, :2]
    spec_q = pl.BlockSpec((1, tq, 2 * LANES), lambda i, h, j: (i, j, h))
    spec_kv = pl.BlockSpec((1, s, 2 * LANES), lambda i, h, j: (i, 0, h))
    return pl.pallas_call(
        functools.partial(_diff_body, tq=tq, out_scale=1.0 - lam_init),
        grid=(b, D_HEADS, s // tq),
        in_specs=[pl.BlockSpec(memory_space=pltpu.SMEM),
                  spec_q, spec_kv, spec_kv,
                  pl.BlockSpec((1, 2, tq, tq), lambda i, h, j: (h, 0, 0, 0)),
                  _const_spec((1, D_DV))],
        out_specs=pl.BlockSpec((1, tq, D_DV), lambda i, h, j: (i, j, h)),
        out_shape=jax.ShapeDtypeStruct((b, s, D_HEADS * D_DV), BF16),
        compiler_params=_params("arbitrary", "arbitrary", "arbitrary"),
        name="diff_attn",
    )(lam_full.reshape(1), q, k, v, tiles, subln.reshape(1, D_DV))


def _pad_blocks(w, n, width, to=LANES, at=0):
    k = w.shape[0]
    return jnp.pad(w.reshape(k, n, width), ((0, 0), (0, 0), (at, to - width - at))).reshape(k, n * to)


def _ones_row(n_blocks, which, to=LANES, lane=ONES_LANE):
    row = np.zeros((n_blocks, to), np.float32)
    row[list(which), lane] = 1.0
    return jnp.asarray(row.reshape(1, n_blocks * to))


def _split(w, sizes):
    return jnp.split(w, [int(c) for c in np.cumsum(sizes)[:-1]], axis=-1)


def _pad_rows(w, n, width):
    d = w.shape[1]
    return jnp.pad(w.reshape(n, width, d), ((0, 0), (0, LANES - width), (0, 0))).reshape(n * LANES, d)


def _even_mixer(x2, g_pre, g_post, w_in, w_out, cmp_pe, cmp_w, rel_tab, b, s):
    qa, kva, ga, qb, kb, vb, iq, ik, iw = _split(w_in, EV_SIZES)
    half = 2 * A_KV * A_DH
    w_segs = [_pad_blocks(qa * (A_DH ** -0.5 * LOG2E), A_HEADS, A_DH), kva[:, :half],
              _pad_blocks(kva[:, half:], 4 * A_KV, A_DH), _pad_blocks(ga, A_KV, 3 * A_GRP),
              _pad_blocks(qb * (B_DH ** -0.5 * LOG2E), B_HEADS, B_DH),
              _pad_blocks(jnp.concatenate([kb, vb], axis=1), 2, B_DH),
              _pad_blocks(iq, IDX_HEADS, IDX_DH), _pad_blocks(ik, 1, IDX_DH), _pad_blocks(iw, 1, IDX_HEADS)]
    zeros = lambda w: jnp.zeros((1, w.shape[1]), F32)
    c_segs = [zeros(w) for w in w_segs]
    c_segs[2] = _ones_row(4 * A_KV, (2, 3, 6, 7))
    c_segs[5] = _ones_row(2, (1,))
    dts = [BF16, F32, BF16, F32, BF16, BF16, BF16, BF16, F32]
    q_a, kcvc, kvsw, g_a, q_b, kbvb, i_q, i_k, i_w = _proj(x2, g_pre, w_segs, c_segs, dts)
    r3 = lambda a: a.reshape(b, s, a.shape[-1])
    kvc = _nsa_compress(kcvc, cmp_pe, cmp_w, b, s)
    o_a = _nsa(r3(q_a), kvc, r3(kvsw), r3(g_a), rel_tab[:, :A_HEADS], b, s)
    o_b = _dsa(r3(q_b), r3(kbvb), r3(i_q), r3(i_k), r3(i_w), rel_tab[:, A_HEADS:A_HEADS + B_HEADS], b, s)
    na = A_HEADS * A_DH
    wa = _pad_rows(w_out[:na], A_HEADS, A_DH)
    wb = _pad_rows(w_out[na:], B_HEADS, B_DH)
    return _out_proj(o_a.reshape(b * s, -1), o_b.reshape(b * s, -1), wa, wb, x2, g_post)


def _odd_mixer(x2, g_pre, g_post, w_in, w_out, q_norm, kv_norm, w_uq, w_ukv, lam, subln, rel_tab,
               lam_init, b, s):
    c_q, c_kv, k_rope, qd, kd, vd = _split(w_in, OD_SIZES)
    w_kr = jnp.concatenate([_pad_blocks(k_rope, 1, C_DR, at=C_DN),
                            _pad_blocks(_rot_cols(k_rope), 1, C_DR, at=C_DN)], axis=1)
    w_segs = [c_q, c_kv, w_kr, _pad_blocks(qd * (D_DH ** -0.5 * LOG2E), 2 * D_HEADS, D_DH),
              _pad_blocks(kd, 2 * D_HEADS, D_DH), _pad_blocks(vd, D_HEADS, D_DV, to=2 * LANES)]
    c_segs = [jnp.zeros((1, w.shape[1]), F32) for w in w_segs]
    c_segs[5] = _ones_row(D_HEADS, range(D_HEADS), to=2 * LANES, lane=D_DV)
    dts = [F32, F32, F32, BF16, BF16, BF16]
    cq, ckv, kr, q_d, k_d, v_d = _proj(x2, g_pre, w_segs, c_segs, dts)
    q_m, k_m, v_m = _mla_prep(cq, ckv, kr, q_norm, kv_norm, w_uq, w_ukv, s)
    r3 = lambda a: a.reshape(b, s, a.shape[-1])
    o_c = _mla_attn(r3(q_m), r3(k_m), r3(v_m), b, s)
    lam32 = lam.astype(F32)
    lam_full = jnp.exp(jnp.sum(lam32[0] * lam32[1])) - jnp.exp(jnp.sum(lam32[2] * lam32[3])) + lam_init
    o_d = _diff_attn(r3(q_d), r3(k_d), r3(v_d), rel_tab[:, A_HEADS + B_HEADS:], lam_full, subln,
                     lam_init, b, s)
    nc = C_HEADS * C_DV
    wc = _pad_rows(w_out[:nc], C_HEADS, C_DV)
    return _out_proj(o_c.reshape(b * s, -1), o_d.reshape(b * s, -1), wc, w_out[nc:], x2, g_post)


def kernel(x, mem, rel_bias_table, norm_pre, norm_post, mem_norm, ffn_wg, ffn_wu, ffn_wd, ev_w_in,
           ev_cmp_pe, ev_cmp_w, ev_w_out, od_w_in, od_q_norm, od_kv_norm, od_w_uq, od_w_ukv, od_lambda,
           od_subln, od_w_out, xa_wq, xa_wkv, xa_wo):
    b, s, d = x.shape
    depth = norm_pre.shape[0]
    x2 = x.reshape(b * s, d)
    mem2 = mem.reshape(-1, d)
    for i in range(depth):
        g_pre, g_post = norm_pre[i], norm_post[i]
        x2 = _ffn(x2, g_pre[0], g_post[0], ffn_wg[i, 0], ffn_wu[i, 0], ffn_wd[i, 0])
        if i % 2 == 0:
            e = i // 2
            x2 = _even_mixer(x2, g_pre[1], g_post[1], ev_w_in[e], ev_w_out[e], ev_cmp_pe[e], ev_cmp_w[e],
                             rel_bias_table, b, s)
        else:
            o = i // 2
            lam_init = 0.8 - 0.6 * math.exp(-0.3 * i)
            x2 = _odd_mixer(x2, g_pre[1], g_post[1], od_w_in[o], od_w_out[o], od_q_norm[o], od_kv_norm[o],
                            od_w_uq[o], od_w_ukv[o], od_lambda[o], od_subln[o], rel_bias_table, lam_init, b, s)
        (kv,) = _proj(mem2, mem_norm[i], [xa_wkv[i]], [jnp.zeros((1, xa_wkv.shape[-1]), F32)], [BF16])
        x3 = _mem_xattn(x2.reshape(b, s, d), kv.reshape(b, -1, kv.shape[-1]), g_pre[2], g_post[2],
                        xa_wq[i], xa_wo[i])
        x2 = _ffn(x3.reshape(b * s, d), g_pre[3], g_post[3], ffn_wg[i, 1], ffn_wu[i, 1], ffn_wd[i, 1])
    return x2.reshape(b, s, d)
```

```python
import functools
import math

import numpy as np
import jax
import jax.numpy as jnp
from jax import lax
from jax.experimental import pallas as pl
from jax.experimental.pallas import tpu as pltpu

F32, BF16, I32 = jnp.float32, jnp.bfloat16, jnp.int32

N_BUCKETS = 32
MAX_DISTANCE = 128
RMS_EPS = 1e-6
ROPE_THETA = 10000.0
NEG_INF = -1e30
A_HEADS, A_KV, A_DH = 8, 2, 64
A_GRP = A_HEADS // A_KV
CMP_LEN, CMP_STRIDE, SLC_LEN, SLC_TOP, WIN = 32, 16, 64, 16, 512
FORCE_BONUS = 1e3
B_HEADS, B_DH, IDX_HEADS, IDX_DH, IDX_TOPK_MAX = 8, 64, 8, 32, 256
C_HEADS, C_Q_RANK, C_KV_RANK, C_DN, C_DR, C_DV = 8, 256, 128, 64, 32, 64
D_HEADS, D_DH, D_DV = 4, 64, 128
X_HEADS, X_DH = 4, 128
EV_SIZES = (A_HEADS * A_DH, 6 * A_KV * A_DH, 3 * A_HEADS,
            B_HEADS * B_DH, B_DH, B_DH, IDX_HEADS * IDX_DH, IDX_DH, IDX_HEADS)
OD_SIZES = (C_Q_RANK, C_KV_RANK, C_DR, D_HEADS * 2 * D_DH, D_HEADS * 2 * D_DH, D_HEADS * D_DV)

LANES = 128
VMEM_LIMIT_BYTES = 56 * 1024 * 1024
ONES_LANE = 64
INT_MIN = -2 ** 31
LOG2E = 1.4426950408889634

TOK_TILE = 512
FFN_CHUNK = 256
SPARSE_TQ = 128
WIDE_TILES = 4
DSA_TILES_PER_STEP = 1
DENSE_TQ = 512
MLA_HEADS_PER_STEP = 4
DIFF_HEADS_PER_STEP = 2


def _params(*sem):
    return pltpu.CompilerParams(dimension_semantics=sem, vmem_limit_bytes=VMEM_LIMIT_BYTES)


def _const_spec(shape):
    nd = len(shape)
    return pl.BlockSpec(shape, lambda *_: (0,) * nd, pipeline_mode=pl.Buffered(1))


def _dot(a, b):
    return jnp.dot(a, b, preferred_element_type=F32)


def _dot_t(a, b):
    return lax.dot_general(a, b, (((1,), (1,)), ((), ())), preferred_element_type=F32)


def _rms(x, g):
    return x * lax.rsqrt(jnp.mean(x * x, axis=-1, keepdims=True) + RMS_EPS) * g


def _sigmoid(x):
    return 1.0 / (1.0 + jnp.exp(-x))


def _online_step(q, k, v, carry, add=None, shared=None, heads=1):
    m, acc = carry
    s = _dot_t(q, k)
    if add is not None:
        s = s + add
    if shared is not None:
        s = (s.reshape(heads, shared.shape[0], shared.shape[1]) + shared[None]).reshape(s.shape)
    m_new = jnp.maximum(m, jnp.max(s, axis=-1, keepdims=True))
    p = jnp.exp2(s - m_new)
    acc = jnp.exp2(m - m_new) * acc + _dot(p.astype(BF16), v)
    return m_new, acc


def _flash_init(m_rows, n_lanes):
    return jnp.full((m_rows, 1), -jnp.inf, F32), jnp.zeros((m_rows, n_lanes), F32)


def _ffn_body(x_ref, gpre_ref, gpost_ref, wgu_ref, wd_ref, o_ref, *, n_chunks, fc):
    x = x_ref[...]
    h = _rms(x, gpre_ref[...]).astype(BF16)
    acc = jnp.zeros(x.shape, F32)
    for c in range(n_chunks):
        gu = _dot(h, wgu_ref[:, c * 2 * fc:(c + 1) * 2 * fc])
        g, u = gu[:, :fc], gu[:, fc:]
        a = g * _sigmoid(g) * u
        acc = acc + _dot(a.astype(BF16), wd_ref[c * fc:(c + 1) * fc, :])
    o_ref[...] = x + 0.5 * _rms(acc, gpost_ref[...])


def _ffn(x, g_pre, g_post, wg, wu, wd):
    n, d = x.shape
    f = wg.shape[1]
    fc = FFN_CHUNK
    n_chunks = f // fc
    assert n_chunks * fc == f and n % TOK_TILE == 0
    wgu = jnp.concatenate([wg.reshape(d, n_chunks, fc), wu.reshape(d, n_chunks, fc)], axis=-1)
    wgu = wgu.reshape(d, 2 * f).astype(BF16)
    return pl.pallas_call(
        functools.partial(_ffn_body, n_chunks=n_chunks, fc=fc),
        grid=(n // TOK_TILE,),
        in_specs=[pl.BlockSpec((TOK_TILE, d), lambda i: (i, 0)),
                  _const_spec((1, d)), _const_spec((1, d)),
                  _const_spec((d, 2 * f)), _const_spec((f, d))],
        out_specs=pl.BlockSpec((TOK_TILE, d), lambda i: (i, 0)),
        out_shape=jax.ShapeDtypeStruct((n, d), F32),
        compiler_params=_params("arbitrary"),
        name="ffn",
    )(x, g_pre.reshape(1, d), g_post.reshape(1, d), wgu, wd.astype(BF16))


def _proj_body(x_ref, g_ref, w_ref, c_ref, *o_refs, bounds):
    h = _rms(x_ref[...], g_ref[...]).astype(BF16)
    for o_ref, (a, b) in zip(o_refs, bounds):
        o_ref[...] = (_dot(h, w_ref[:, a:b]) + c_ref[:, a:b]).astype(o_ref.dtype)


def _proj(x, g, w_segs, c_segs, dtypes):
    n, d = x.shape
    widths = [w.shape[1] for w in w_segs]
    assert all(wd_ % LANES == 0 for wd_ in widths) and n % TOK_TILE == 0
    offs = np.concatenate([[0], np.cumsum(widths)])
    bounds = tuple((int(offs[i]), int(offs[i + 1])) for i in range(len(widths)))
    w = jnp.concatenate(w_segs, axis=1).astype(BF16)
    c = jnp.concatenate(c_segs, axis=1).astype(F32)
    tot = int(offs[-1])
    return pl.pallas_call(
        functools.partial(_proj_body, bounds=bounds),
        grid=(n // TOK_TILE,),
        in_specs=[pl.BlockSpec((TOK_TILE, d), lambda i: (i, 0)),
                  _const_spec((1, d)), _const_spec((d, tot)), _const_spec((1, tot))],
        out_specs=[pl.BlockSpec((TOK_TILE, wd_), lambda i: (i, 0)) for wd_ in widths],
        out_shape=[jax.ShapeDtypeStruct((n, wd_), dt) for wd_, dt in zip(widths, dtypes)],
        compiler_params=_params("arbitrary"),
        name="norm_proj",
    )(x, g.reshape(1, d), w, c)


def _out_body(a_ref, b_ref, wa_ref, wb_ref, x_ref, g_ref, o_ref):
    y = _dot(a_ref[...], wa_ref[...]) + _dot(b_ref[...], wb_ref[...])
    o_ref[...] = x_ref[...] + _rms(y, g_ref[...])


def _out_proj(oa, ob, wa, wb, x, g_post):
    n, d = x.shape
    ka, kb = oa.shape[1], ob.shape[1]
    return pl.pallas_call(
        _out_body,
        grid=(n // TOK_TILE,),
        in_specs=[pl.BlockSpec((TOK_TILE, ka), lambda i: (i, 0)),
                  pl.BlockSpec((TOK_TILE, kb), lambda i: (i, 0)),
                  _const_spec((ka, d)), _const_spec((kb, d)),
                  pl.BlockSpec((TOK_TILE, d), lambda i: (i, 0)),
                  _const_spec((1, d))],
        out_specs=pl.BlockSpec((TOK_TILE, d), lambda i: (i, 0)),
        out_shape=jax.ShapeDtypeStruct((n, d), F32),
        compiler_params=_params("arbitrary"),
        name="out_proj",
    )(oa, ob, wa.astype(BF16), wb.astype(BF16), x, g_post.reshape(1, d))


def _xattn_body(x_ref, gpre_ref, gpost_ref, wq_ref, kv_ref, wo_ref, o_ref, *, heads, dh):
    x = x_ref[0]
    h = _rms(x, gpre_ref[...]).astype(BF16)
    q = _dot(h, wq_ref[...]).astype(BF16)
    kv = kv_ref[0]
    scale = dh ** -0.5
    outs = []
    for hh in range(heads):
        k = kv[:, hh * dh:(hh + 1) * dh]
        v = kv[:, (heads + hh) * dh:(heads + hh + 1) * dh]
        s = _dot_t(q[:, hh * dh:(hh + 1) * dh], k) * scale
        p = jnp.exp(s - jnp.max(s, axis=-1, keepdims=True))
        p = p / jnp.sum(p, axis=-1, keepdims=True)
        outs.append(_dot(p.astype(BF16), v).astype(BF16))
    y = _dot(jnp.concatenate(outs, axis=-1), wo_ref[...])
    o_ref[0] = x + _rms(y, gpost_ref[...])


def _mem_xattn(x, kv, g_pre, g_post, wq, wo):
    b, s, d = x.shape
    mlen, kvw = kv.shape[1:]
    hd = X_HEADS * X_DH
    return pl.pallas_call(
        functools.partial(_xattn_body, heads=X_HEADS, dh=X_DH),
        grid=(b, s // TOK_TILE),
        in_specs=[pl.BlockSpec((1, TOK_TILE, d), lambda i, j: (i, j, 0)),
                  _const_spec((1, d)), _const_spec((1, d)), _const_spec((d, hd)),
                  pl.BlockSpec((1, mlen, kvw), lambda i, j: (i, 0, 0)),
                  _const_spec((hd, d))],
        out_specs=pl.BlockSpec((1, TOK_TILE, d), lambda i, j: (i, j, 0)),
        out_shape=jax.ShapeDtypeStruct((b, s, d), F32),
        compiler_params=_params("arbitrary", "arbitrary"),
        name="mem_xattn",
    )(x, g_pre.reshape(1, d), g_post.reshape(1, d), wq.astype(BF16), kv, wo.astype(BF16))


def _bucket_np(dist):
    max_exact = N_BUCKETS // 2
    d = np.maximum(dist, 0)
    ratio = np.log(np.maximum(d, 1).astype(np.float32) / np.float32(max_exact)) / np.float32(
        math.log(MAX_DISTANCE / max_exact))
    large = np.minimum(max_exact + (ratio * np.float32(N_BUCKETS - max_exact)).astype(np.int32),
                       N_BUCKETS - 1)
    return np.where(d < max_exact, d, large).astype(np.int32)


def _rel_bias(tab, dist, visible, rel_to_far):
    dist = np.maximum(np.asarray(dist), 0).astype(np.int32)
    bk = _bucket_np(np.arange(max(int(dist.max()), 4 * MAX_DISTANCE) + 1))
    assert (np.diff(bk) >= 0).all() and bk[-1] == N_BUCKETS - 1
    tab = tab.astype(F32)
    col = lambda b: tab[b].reshape((-1,) + (1,) * dist.ndim)
    d = jnp.asarray(dist)[None]
    out = jnp.broadcast_to(col(0), (tab.shape[1],) + dist.shape)
    for b in range(1, N_BUCKETS):
        first = np.nonzero(bk == b)[0]
        if first.size:
            out = jnp.where(d >= int(first[0]), col(b), out)
    if rel_to_far:
        out = out - col(N_BUCKETS - 1)
    return jnp.where(jnp.asarray(np.asarray(visible))[None], out * LOG2E, NEG_INF)


def _far_distance():
    bk = _bucket_np(np.arange(4 * MAX_DISTANCE))
    return int(np.nonzero(bk == N_BUCKETS - 1)[0][0])


def _bias_tiles(tab, t):
    assert t + 1 >= _far_distance()
    r = np.arange(t)[:, None]
    c = np.arange(t)[None, :]
    dist = np.stack([r - c, t + r - c, 2 * t + r - c])
    everywhere = np.ones((t, t), bool)
    return _rel_bias(tab, dist, np.stack([r >= c, everywhere, everywhere]), True)


def _near_tiles(tab, t):
    tiles = _bias_tiles(tab, t)
    hidden = jnp.full_like(tiles[:, 0], NEG_INF)
    return jnp.stack([jnp.concatenate([tiles[:, 1], tiles[:, 0]], axis=-1),
                      jnp.concatenate([tiles[:, 0], hidden], axis=-1)], axis=1)


def _key_loops(qt, tq, wide, wide_step, near_step, init):
    far = jnp.maximum(qt - 1, 0) * tq
    tw = wide * tq
    n_wide = (far + tw - 1) // tw

    def body(j, carry):
        start = pl.multiple_of(jnp.maximum(far - (n_wide - j) * tw, 0), tq)
        return wide_step(start, jnp.where(j == 0, far - (n_wide - 1) * tw, tw), carry)

    carry = lax.fori_loop(0, n_wide, body, init)
    return near_step(pl.multiple_of(far, tq), jnp.where(qt == 0, 1, 0), carry)


def _cmp_body(x_ref, pe_ref, w_ref, o_ref):
    x = x_ref[0]
    xn = pltpu.roll(x, x.shape[0] - 1, 0)
    lo = (x + pe_ref[0:1, :]).astype(BF16)
    hi = (xn + pe_ref[1:2, :]).astype(BF16)
    o_ref[0] = (_dot(lo, w_ref[0]) + _dot(hi, w_ref[1])).astype(BF16)


def _nsa_compress(kcvc, cmp_pe, cmp_w, b, s):
    nb = s // CMP_STRIDE
    halves = CMP_LEN // CMP_STRIDE
    n_grp = 2 * A_KV
    kdim = CMP_STRIDE * n_grp * A_DH
    cw = cmp_w.reshape(2, halves, CMP_STRIDE, A_DH, A_DH)
    same = jnp.eye(n_grp, dtype=F32).reshape(2, A_KV, 2, A_KV)
    w = jnp.einsum("khlde,kgKG->hlkgdKGe", cw, same)
    w = jnp.pad(w, [(0, 0)] * 7 + [(0, LANES - A_DH)]).reshape(halves, kdim, n_grp * LANES).astype(BF16)
    pe = cmp_pe.reshape(2, halves, CMP_STRIDE, 1, A_DH).transpose(1, 2, 0, 3, 4)
    pe = jnp.broadcast_to(pe, (halves, CMP_STRIDE, 2, A_KV, A_DH)).reshape(halves, kdim)
    return pl.pallas_call(
        _cmp_body,
        grid=(b,),
        in_specs=[pl.BlockSpec((1, nb, kdim), lambda i: (i, 0, 0)),
                  _const_spec((halves, kdim)), _const_spec((halves, kdim, n_grp * LANES))],
        out_specs=pl.BlockSpec((1, nb, n_grp * LANES), lambda i: (i, 0, 0)),
        out_shape=jax.ShapeDtypeStruct((b, nb, n_grp * LANES), BF16),
        compiler_params=_params("arbitrary"),
        name="nsa_compress",
    )(kcvc.reshape(b, nb, kdim), pe, w)


def _nsa_body(q_ref, kc_ref, vc_ref, bc_ref, ks_ref, vs_ref, kw_ref, vw_ref, ga_ref, bias_ref,
              near_ref, wbias_ref, ovl_ref, exp_ref, o_ref, mask_ref, *, tq, seq, n_sb):
    qt = pl.program_id(2)
    rr = A_GRP
    m_rows = rr * tq
    ncp = kc_ref.shape[1]
    q = q_ref[0]
    qs = jnp.concatenate([q[:, r * LANES:(r + 1) * LANES] for r in range(rr)], axis=0)

    s_c = _dot_t(qs, kc_ref[0]) + bc_ref[...].reshape(m_rows, ncp)
    p = jnp.where(s_c > 0.5 * NEG_INF, jnp.exp2(s_c - jnp.max(s_c, axis=-1, keepdims=True)), 0.0)
    l = jnp.sum(p, axis=-1, keepdims=True)
    pn = p * jnp.where(l > 0.0, 1.0 / l, 0.0)
    o_c = _dot(pn.astype(BF16), vc_ref[0])

    imp = jnp.sum(pn.reshape(rr, tq, ncp), axis=0)
    imp_t = lax.dot_general(ovl_ref[...], imp, (((1,), (1,)), ((), ())),
                            precision=lax.Precision.HIGHEST, preferred_element_type=F32)
    blk = lax.broadcasted_iota(I32, (LANES, tq), 0)
    cur = (qt * tq + lax.broadcasted_iota(I32, (LANES, tq), 1)) // SLC_LEN
    forced = (blk == 0) | (blk == cur) | (blk == cur - 1)
    score = jnp.where(blk <= cur, imp_t + jnp.where(forced, FORCE_BONUS, 0.0), NEG_INF)
    groups = [score[8 * v:8 * v + 8] for v in range(-(-n_sb // 8))]
    ranks = [jnp.zeros((8, tq), F32) for _ in groups]
    sub = lax.broadcasted_iota(I32, (8, tq), 0)
    for i in range(n_sb):
        ci = score[i:i + 1, :]
        for v, sv in enumerate(groups):
            ge = jnp.where(ci >= sv, 1.0, 0.0)
            gt = jnp.where(ci > sv, 1.0, 0.0)
            if 8 * v > i:
                ranks[v] = ranks[v] + ge
            elif 8 * v + 7 < i:
                ranks[v] = ranks[v] + gt
            else:
                ranks[v] = ranks[v] + jnp.where(sub + 8 * v > i, ge, gt)
    rank = jnp.concatenate(ranks + [jnp.full((LANES - 8 * len(groups), tq), float(LANES), F32)], axis=0)
    sel_t = jnp.where(rank < float(min(SLC_TOP, n_sb)), jnp.where(blk <= cur, 1.0, 0.0), 0.0)
    selb = sel_t.T.astype(BF16)

    def expand(c, _):
        cs = pl.multiple_of(c * tw, tw)
        mask_ref[:, pl.ds(cs, tw)] = (_dot(selb, exp_ref[:, pl.ds(cs, tw)]) - 1.0) * (-NEG_INF)
        return 0

    tw = WIDE_TILES * tq
    lax.fori_loop(0, qt // WIDE_TILES + 1, expand, 0)

    col_w = lax.broadcasted_iota(I32, (tq, tw), 1)

    def slc_wide(ks, limit, carry):
        own = jnp.where(col_w < limit, mask_ref[:, pl.ds(ks, tw)], NEG_INF)
        return _online_step(qs, ks_ref[0, pl.ds(ks, tw), :], vs_ref[0, pl.ds(ks, tw), :], carry,
                            shared=own, heads=rr)

    def slc_near(ks, which, carry):
        return _online_step(qs, ks_ref[0, pl.ds(ks, 2 * tq), :], vs_ref[0, pl.ds(ks, 2 * tq), :], carry,
                            add=near_ref[0, which], shared=mask_ref[:, pl.ds(ks, 2 * tq)], heads=rr)

    _, acc_s = _key_loops(qt, tq, WIDE_TILES, slc_wide, slc_near, _flash_init(m_rows, LANES))

    n_woff = WIN // tq

    def win(kt, carry):
        ks = pl.multiple_of(kt * tq, tq)
        return _online_step(qs, kw_ref[0, pl.ds(ks, tq), :], vw_ref[0, pl.ds(ks, tq), :], carry,
                            add=bias_ref[0, jnp.minimum(qt - kt, 2)])

    def win_full():
        ks = pl.multiple_of((qt - n_woff) * tq, tq)
        return _online_step(qs, kw_ref[0, pl.ds(ks, WIN), :], vw_ref[0, pl.ds(ks, WIN), :],
                            _flash_init(m_rows, LANES), add=wbias_ref[0])

    def win_head():
        return lax.fori_loop(0, qt, win, _flash_init(m_rows, LANES))

    _, acc_w = win(qt, lax.cond(qt >= n_woff, win_full, win_head))

    gs = _sigmoid(ga_ref[0])
    for r in range(rr):
        a_s = acc_s[r * tq:(r + 1) * tq]
        a_w = acc_w[r * tq:(r + 1) * tq]
        o = (gs[:, 3 * r:3 * r + 1] * o_c[r * tq:(r + 1) * tq]
             + gs[:, 3 * r + 1:3 * r + 2] * (a_s / a_s[:, ONES_LANE:ONES_LANE + 1])
             + gs[:, 3 * r + 2:3 * r + 3] * (a_w / a_w[:, ONES_LANE:ONES_LANE + 1]))
        o_ref[0, :, r * LANES:(r + 1) * LANES] = o.astype(BF16)


def _nsa(q, kvc, kvsw, ga, tab, b, s):
    tq = SPARSE_TQ
    ncp = s // CMP_STRIDE
    n_sb = s // SLC_LEN
    assert n_sb <= LANES and WIN % tq == 0 and s % (WIDE_TILES * tq) == 0
    t = np.arange(s)[:, None]
    cmp_end = np.arange(ncp)[None, :] * CMP_STRIDE + CMP_LEN - 1
    bias_c = _rel_bias(tab, t - cmp_end, cmp_end <= t, False)
    tiles = _bias_tiles(tab, tq)
    r = np.arange(tq)[:, None]
    c = np.arange(tq)[None, :]
    oldest = jnp.broadcast_to(jnp.asarray(np.where(r < c, 0.0, NEG_INF).astype(np.float32)), (A_HEADS, tq, tq))
    n_woff = WIN // tq
    wbias = jnp.concatenate([oldest] + [tiles[:, 2]] * (n_woff - 2) + [tiles[:, 1]], axis=-1)
    wbias = wbias.reshape(A_KV, A_GRP * tq, WIN)
    by_group = lambda a: a.reshape((A_KV, A_GRP) + a.shape[1:]).transpose(0, 2, 1, 3, 4).reshape(
        A_KV, a.shape[1], A_GRP * tq, a.shape[3])
    near = by_group(_near_tiles(tab, tq))
    tiles = by_group(tiles)
    cmp_start = np.arange(ncp)[None, :] * CMP_STRIDE
    sb_start = np.arange(LANES)[:, None] * SLC_LEN
    ovl = ((cmp_start < sb_start + SLC_LEN) & (cmp_start + CMP_LEN - 1 >= sb_start)
           & (np.arange(ncp)[None, :] < ncp - 1) & (np.arange(LANES)[:, None] < n_sb))
    ovl = jnp.asarray(ovl.astype(np.float32))
    expand = jnp.asarray((np.arange(LANES)[:, None] == np.arange(s)[None, :] // SLC_LEN).astype(np.float32), BF16)
    kv_spec = lambda blk: pl.BlockSpec((1, s, LANES), lambda i, g, j: (i, 0, 2 * blk + g))
    return pl.pallas_call(
        functools.partial(_nsa_body, tq=tq, seq=s, n_sb=n_sb),
        grid=(b, A_KV, s // tq),
        in_specs=[pl.BlockSpec((1, tq, A_GRP * LANES), lambda i, g, j: (i, j, g)),
                  pl.BlockSpec((1, ncp, LANES), lambda i, g, j: (i, 0, g)),
                  pl.BlockSpec((1, ncp, LANES), lambda i, g, j: (i, 0, A_KV + g)),
                  pl.BlockSpec((A_GRP, tq, ncp), lambda i, g, j: (g, j, 0)),
                  kv_spec(0), kv_spec(1), kv_spec(2), kv_spec(3),
                  pl.BlockSpec((1, tq, LANES), lambda i, g, j: (i, j, g)),
                  pl.BlockSpec((1, 3, A_GRP * tq, tq), lambda i, g, j: (g, 0, 0, 0)),
                  pl.BlockSpec((1, 2, A_GRP * tq, 2 * tq), lambda i, g, j: (g, 0, 0, 0)),
                  pl.BlockSpec((1, A_GRP * tq, WIN), lambda i, g, j: (g, 0, 0)),
                  _const_spec((LANES, ncp)), _const_spec((LANES, s))],
        out_specs=pl.BlockSpec((1, tq, A_GRP * LANES), lambda i, g, j: (i, j, g)),
        out_shape=jax.ShapeDtypeStruct((b, s, A_HEADS * LANES), BF16),
        scratch_shapes=[pltpu.VMEM((tq, s), F32)],
        compiler_params=_params("arbitrary", "arbitrary", "arbitrary"),
        name="nsa_attn",
    )(q, kvc, kvc, bias_c, kvsw, kvsw, kvsw, kvsw, ga, tiles, near, wbias, ovl, expand)


def _dsa_body(q_ref, k_ref, v_ref, iq_ref, ik_ref, iw_ref, bias_ref, o_ref, keys_ref, wb_ref, *,
              tq, n_sub, seq, k_top):
    qb = pl.program_id(1)
    nh = B_HEADS
    m_rows = nh * tq
    rows = n_sub * tq
    tw = WIDE_TILES * tq
    n_wide = (qb * n_sub + n_sub - 1) // WIDE_TILES + 1
    col_n = lax.broadcasted_iota(I32, (tq, 2 * tq), 1)
    row_w = lax.broadcasted_iota(I32, (tq, tw), 0)
    col_w = lax.broadcasted_iota(I32, (tq, tw), 1)

    for sub in range(n_sub):
        r0 = sub * tq
        iq = iq_ref[0, r0:r0 + tq, :]
        iqs = jnp.concatenate([iq[:, h * LANES:(h + 1) * LANES] for h in range(IDX_HEADS)], axis=0)
        iw = iw_ref[0, r0:r0 + tq, :]
        for h in range(IDX_HEADS):
            wb_ref[h * tq:(h + 1) * tq, :] = jnp.broadcast_to(iw[:, h:h + 1], (tq, LANES))

        def scores(kw, _, r0=r0, iqs=iqs):
            ks = pl.multiple_of(kw * tw, tw)
            lg = _dot_t(iqs, ik_ref[0, pl.ds(ks, tw), :])
            w = wb_ref[...]
            parts = []
            for j in range(tw // LANES):
                part = jnp.maximum(lg[:, j * LANES:(j + 1) * LANES], 0.0) * w
                parts.append(jnp.sum(part.reshape(IDX_HEADS, tq, LANES), axis=0))
            sc = jnp.concatenate(parts, axis=1)
            bits = lax.bitcast_convert_type(sc, I32)
            key = jnp.where(bits < 0, bits ^ 0x7FFFFFFF, bits)
            vis = (col_w + kw * tw) <= (row_w + (qb * rows + r0))
            keys_ref[r0:r0 + tq, pl.ds(ks, tw)] = jnp.where(vis, key, INT_MIN)
            return 0

        lax.fori_loop(0, n_wide, scores, 0)

    subs = range(n_sub)

    def count(ind, *per_row):
        def body(kw, accs):
            ks = pl.multiple_of(kw * tw, tw)
            out = []
            for sub in subs:
                x = ind(keys_ref[sub * tq:(sub + 1) * tq, pl.ds(ks, tw)], *[a[sub] for a in per_row], kw)
                acc = accs[sub]
                for j in range(tw // LANES):
                    acc = acc + x[:, j * LANES:(j + 1) * LANES]
                out.append(acc)
            return tuple(out)
        accs = lax.fori_loop(0, n_wide, body, tuple(jnp.zeros((tq, LANES), F32) for _ in subs))
        return tuple(jnp.sum(a, axis=-1, keepdims=True) for a in accs)

    kf = float(k_top)
    c0 = count(lambda t, kw: jnp.where(t >= 0, 1.0, 0.0))
    thr = tuple(jnp.where(c >= kf, 0, INT_MIN).astype(I32) for c in c0)
    c_ge = tuple(jnp.where(c >= kf, c, float(seq)) for c in c0)

    def bisect(i, state):
        thr, c_ge = state
        cand = tuple(t | (jnp.int32(1) << (30 - i)) for t in thr)
        c = count(lambda t, c_, kw: jnp.where(t >= c_, 1.0, 0.0), cand)
        return (tuple(jnp.where(c[s] >= kf, cand[s], thr[s]) for s in subs),
                tuple(jnp.where(c[s] >= kf, c[s], c_ge[s]) for s in subs))

    thr, c_ge = lax.fori_loop(0, 31, bisect, (thr, c_ge))

    n_bits = int(math.log2(seq))
    assert 2 ** n_bits == seq

    def cutoff():
        c_gt = count(lambda t, th, kw: jnp.where(t > th, 1.0, 0.0), thr)
        need = tuple(kf - c for c in c_gt)

        def step(i, cut):
            cand = tuple(c | (jnp.int32(1) << (n_bits - 1 - i)) for c in cut)
            c = count(lambda t, th, c_, kw: jnp.where(
                t == th, jnp.where(col_w + kw * tw < c_, 1.0, 0.0), 0.0), thr, cand)
            return tuple(jnp.where(c[s] < need[s], cand[s], cut[s]) for s in subs)
        return lax.fori_loop(0, n_bits, step, tuple(jnp.zeros((tq, 1), I32) for _ in subs))

    excess = jnp.max(jnp.concatenate([jnp.where(c > kf, 1.0, 0.0) for c in c_ge], axis=0)) > 0.5
    cut_all = lax.cond(excess, cutoff, lambda: tuple(jnp.full((tq, 1), seq, I32) for _ in subs))

    for sub in subs:
        r0 = sub * tq
        qt = qb * n_sub + sub
        thr_s = thr[sub]
        cut = cut_all[sub]
        q = q_ref[0, r0:r0 + tq, :]
        qs = jnp.concatenate([q[:, h * LANES:(h + 1) * LANES] for h in range(nh)], axis=0)

        def sel_add(key, idx, thr_s=thr_s, cut=cut):
            tie = jnp.where(idx <= cut, 0.0, NEG_INF)
            return jnp.where(key > thr_s, 0.0, jnp.where(key == thr_s, tie, NEG_INF))

        def attend_wide(ks, limit, carry, r0=r0, qs=qs, sel_add=sel_add):
            sel = sel_add(keys_ref[r0:r0 + tq, pl.ds(ks, tw)], col_w + ks)
            sel = jnp.where(col_w < limit, sel, NEG_INF)
            return _online_step(qs, k_ref[0, pl.ds(ks, tw), :], v_ref[0, pl.ds(ks, tw), :], carry,
                                shared=sel, heads=nh)

        def attend_near(ks, which, carry, r0=r0, qs=qs, sel_add=sel_add):
            sel = sel_add(keys_ref[r0:r0 + tq, pl.ds(ks, 2 * tq)], col_n + ks)
            return _online_step(qs, k_ref[0, pl.ds(ks, 2 * tq), :], v_ref[0, pl.ds(ks, 2 * tq), :], carry,
                                add=bias_ref[which], shared=sel, heads=nh)

        _, acc = _key_loops(qt, tq, WIDE_TILES, attend_wide, attend_near, _flash_init(m_rows, LANES))
        o = acc / acc[:, ONES_LANE:ONES_LANE + 1]
        for h in range(nh):
            o_ref[0, r0:r0 + tq, h * LANES:(h + 1) * LANES] = o[h * tq:(h + 1) * tq].astype(BF16)


def _dsa(q, kbvb, iq, ik, iw, tab, b, s):
    tq = SPARSE_TQ
    n_sub = DSA_TILES_PER_STEP
    rows = n_sub * tq
    k_top = min(IDX_TOPK_MAX, s // 4)
    assert tq == LANES and s % (WIDE_TILES * tq) == 0 and s % rows == 0
    tiles = _near_tiles(tab, tq).transpose(1, 0, 2, 3).reshape(2, B_HEADS * tq, 2 * tq)
    return pl.pallas_call(
        functools.partial(_dsa_body, tq=tq, n_sub=n_sub, seq=s, k_top=k_top),
        grid=(b, s // rows),
        in_specs=[pl.BlockSpec((1, rows, B_HEADS * LANES), lambda i, j: (i, j, 0)),
                  pl.BlockSpec((1, s, LANES), lambda i, j: (i, 0, 0)),
                  pl.BlockSpec((1, s, LANES), lambda i, j: (i, 0, 1)),
                  pl.BlockSpec((1, rows, IDX_HEADS * LANES), lambda i, j: (i, j, 0)),
                  pl.BlockSpec((1, s, LANES), lambda i, j: (i, 0, 0)),
                  pl.BlockSpec((1, rows, LANES), lambda i, j: (i, j, 0)),
                  _const_spec((2, B_HEADS * tq, 2 * tq))],
        out_specs=pl.BlockSpec((1, rows, B_HEADS * LANES), lambda i, j: (i, j, 0)),
        out_shape=jax.ShapeDtypeStruct((b, s, B_HEADS * LANES), BF16),
        scratch_shapes=[pltpu.VMEM((rows, s), I32), pltpu.VMEM((IDX_HEADS * tq, LANES), F32)],
        compiler_params=_params("arbitrary", "arbitrary"),
        name="dsa_attn",
    )(q, kbvb, kbvb, iq, ik, iw, tiles)


def _mla_prep_body(cq_ref, ckv_ref, kr_ref, qn_ref, kvn_ref, wq1_ref, wq2_ref, wk_ref, wv_ref,
                   cos_ref, sin_ref, ones_ref, q_o, k_o, v_o):
    cqn = _rms(cq_ref[...], qn_ref[...]).astype(BF16)
    ckn = _rms(ckv_ref[...], kvn_ref[...]).astype(BF16)
    cos, sin = cos_ref[...], sin_ref[...]
    cos_h = jnp.concatenate([cos] * C_HEADS, axis=1)
    sin_h = jnp.concatenate([sin] * C_HEADS, axis=1)
    q_o[...] = (_dot(cqn, wq1_ref[...]) * cos_h + _dot(cqn, wq2_ref[...]) * sin_h).astype(BF16)
    kr = kr_ref[...]
    kr_rot = kr[:, :LANES] * cos + kr[:, LANES:] * sin
    k_o[...] = (_dot(ckn, wk_ref[...]) + jnp.concatenate([kr_rot] * C_HEADS, axis=1)).astype(BF16)
    v_o[...] = (_dot(ckn, wv_ref[...]) + ones_ref[...]).astype(BF16)


def _rot_cols(w):
    half = C_DR // 2
    return jnp.concatenate([-w[..., half:], w[..., :half]], axis=-1)


def _mla_prep(cq, ckv, kr, q_norm, kv_norm, w_uq, w_ukv, s):
    n = cq.shape[0]
    hw = C_HEADS * LANES
    padq = LANES - C_DN - C_DR
    wq = w_uq.reshape(C_Q_RANK, C_HEADS, C_DN + C_DR) * ((C_DN + C_DR) ** -0.5 * LOG2E)
    wq1 = jnp.pad(wq, ((0, 0), (0, 0), (0, padq))).reshape(C_Q_RANK, hw)
    wq2 = jnp.pad(_rot_cols(wq[..., C_DN:]), ((0, 0), (0, 0), (C_DN, padq))).reshape(C_Q_RANK, hw)
    wkv = w_ukv.reshape(C_KV_RANK, C_HEADS, C_DN + C_DV)
    wk = jnp.pad(wkv[..., :C_DN], ((0, 0), (0, 0), (0, LANES - C_DN))).reshape(C_KV_RANK, hw)
    wv = jnp.pad(wkv[..., C_DN:], ((0, 0), (0, 0), (0, LANES - C_DV))).reshape(C_KV_RANK, hw)
    half = C_DR // 2
    inv = ROPE_THETA ** (-jnp.arange(half, dtype=F32) / half)
    ang = jnp.arange(s, dtype=F32)[:, None] * inv[None, :]
    cos, sin = jnp.cos(ang), jnp.sin(ang)
    cos_t = jnp.concatenate([jnp.ones((s, C_DN), F32), cos, cos, jnp.zeros((s, padq), F32)], axis=1)
    sin_t = jnp.concatenate([jnp.zeros((s, C_DN), F32), sin, sin, jnp.zeros((s, padq), F32)], axis=1)
    ones = jnp.tile(jnp.asarray(np.arange(LANES) == ONES_LANE, F32), C_HEADS).reshape(1, hw)
    tile_tok = lambda w_: pl.BlockSpec((TOK_TILE, w_), lambda i: (i, 0))
    n_pos = s // TOK_TILE
    pos_spec = pl.BlockSpec((TOK_TILE, LANES), lambda i: (i % n_pos, 0))
    return pl.pallas_call(
        _mla_prep_body,
        grid=(n // TOK_TILE,),
        in_specs=[tile_tok(C_Q_RANK), tile_tok(C_KV_RANK), tile_tok(2 * LANES),
                  _const_spec((1, C_Q_RANK)), _const_spec((1, C_KV_RANK)),
                  _const_spec((C_Q_RANK, hw)), _const_spec((C_Q_RANK, hw)),
                  _const_spec((C_KV_RANK, hw)), _const_spec((C_KV_RANK, hw)),
                  pos_spec, pos_spec, _const_spec((1, hw))],
        out_specs=[tile_tok(hw)] * 3,
        out_shape=[jax.ShapeDtypeStruct((n, hw), BF16)] * 3,
        compiler_params=_params("arbitrary"),
        name="mla_prep",
    )(cq, ckv, kr, q_norm.reshape(1, -1), kv_norm.reshape(1, -1), wq1.astype(BF16), wq2.astype(BF16),
      wk.astype(BF16), wv.astype(BF16), cos_t, sin_t, ones)


def _mla_attn_body(q_ref, k_ref, v_ref, o_ref, *, tq, nh):
    qt = pl.program_id(2)
    q = q_ref[0]
    lanes = lambda a, h: a[:, h * LANES:(h + 1) * LANES]

    def step(kt, carry, add):
        ks = pl.multiple_of(kt * tq, tq)
        k = k_ref[0, pl.ds(ks, tq), :]
        v = v_ref[0, pl.ds(ks, tq), :]
        return tuple(_online_step(lanes(q, h), lanes(k, h), lanes(v, h), carry[h], add=add)
                     for h in range(nh))

    init = tuple(_flash_init(tq, LANES) for _ in range(nh))
    carry = lax.fori_loop(0, qt, lambda kt, c: step(kt, c, None), init)
    row = lax.broadcasted_iota(I32, (tq, tq), 0)
    col = lax.broadcasted_iota(I32, (tq, tq), 1)
    carry = step(qt, carry, jnp.where(row >= col, 0.0, NEG_INF))
    for h in range(nh):
        acc = carry[h][1]
        o_ref[0, :, h * LANES:(h + 1) * LANES] = (acc / acc[:, ONES_LANE:ONES_LANE + 1]).astype(BF16)


def _mla_attn(q, k, v, b, s):
    tq = DENSE_TQ
    nh = MLA_HEADS_PER_STEP
    spec_q = pl.BlockSpec((1, tq, nh * LANES), lambda i, h, j: (i, j, h))
    spec_kv = pl.BlockSpec((1, s, nh * LANES), lambda i, h, j: (i, 0, h))
    return pl.pallas_call(
        functools.partial(_mla_attn_body, tq=tq, nh=nh),
        grid=(b, C_HEADS // nh, s // tq),
        in_specs=[spec_q, spec_kv, spec_kv],
        out_specs=spec_q,
        out_shape=jax.ShapeDtypeStruct((b, s, C_HEADS * LANES), BF16),
        compiler_params=_params("arbitrary", "arbitrary", "arbitrary"),
        name="mla_attn",
    )(q, k, v)


def _diff_body(lam_ref, q_ref, k_ref, v_ref, bias_ref, g_ref, o_ref, *, tq, nh, out_scale):
    qt = pl.program_id(2)
    q = q_ref[0]
    lanes = lambda a, c, w=LANES: a[:, c * w:(c + 1) * w]

    def step(kt, carry, near):
        ks = pl.multiple_of(kt * tq, tq)
        k = k_ref[0, pl.ds(ks, tq), :]
        v = v_ref[0, pl.ds(ks, tq), :]
        return tuple(_online_step(lanes(q, c), lanes(k, c), lanes(v, c // 2, 2 * LANES), carry[c],
                                  add=bias_ref[c // 2, qt - kt] if near else None)
                     for c in range(2 * nh))

    n_far = jnp.maximum(qt - 1, 0)
    init = tuple(_flash_init(tq, 2 * LANES) for _ in range(2 * nh))
    carry = lax.fori_loop(0, n_far, functools.partial(step, near=False), init)
    carry = lax.fori_loop(n_far, qt + 1, functools.partial(step, near=True), carry)
    outs = [carry[c][1][:, :D_DV] / carry[c][1][:, D_DV:D_DV + 1] for c in range(2 * nh)]
    for h in range(nh):
        o = outs[2 * h] - lam_ref[0] * outs[2 * h + 1]
        o_ref[0, :, h * D_DV:(h + 1) * D_DV] = (_rms(o, g_ref[...]) * out_scale).astype(BF16)


def _diff_attn(q, k, v, tab, lam_full, subln, lam_init, b, s):
    tq = DENSE_TQ
    nh = DIFF_HEADS_PER_STEP
    tiles = _bias_tiles(tab, tq)[:, :2]
    spec_q = pl.BlockSpec((1, tq, nh * 2 * LANES), lambda i, h, j: (i, j, h))
    spec_kv = pl.BlockSpec((1, s, nh * 2 * LANES), lambda i, h, j: (i, 0, h))
    return pl.pallas_call(
        functools.partial(_diff_body, tq=tq, nh=nh, out_scale=1.0 - lam_init),
        grid=(b, D_HEADS // nh, s // tq),
        in_specs=[pl.BlockSpec(memory_space=pltpu.SMEM),
                  spec_q, spec_kv, spec_kv,
                  pl.BlockSpec((nh, 2, tq, tq), lambda i, h, j: (h, 0, 0, 0)),
                  _const_spec((1, D_DV))],
        out_specs=pl.BlockSpec((1, tq, nh * D_DV), lambda i, h, j: (i, j, h)),
        out_shape=jax.ShapeDtypeStruct((b, s, D_HEADS * D_DV), BF16),
        compiler_params=_params("arbitrary", "arbitrary", "arbitrary"),
        name="diff_attn",
    )(lam_full.reshape(1), q, k, v, tiles, subln.reshape(1, D_DV))


def _pad_blocks(w, n, width, to=LANES, at=0):
    k = w.shape[0]
    return jnp.pad(w.reshape(k, n, width), ((0, 0), (0, 0), (at, to - width - at))).reshape(k, n * to)


def _ones_row(n_blocks, which, to=LANES, lane=ONES_LANE):
    row = np.zeros((n_blocks, to), np.float32)
    row[list(which), lane] = 1.0
    return jnp.asarray(row.reshape(1, n_blocks * to))


def _split(w, sizes):
    return jnp.split(w, [int(c) for c in np.cumsum(sizes)[:-1]], axis=-1)


def _pad_rows(w, n, width):
    d = w.shape[1]
    return jnp.pad(w.reshape(n, width, d), ((0, 0), (0, LANES - width), (0, 0))).reshape(n * LANES, d)


def _even_mixer(x2, g_pre, g_post, w_in, w_out, cmp_pe, cmp_w, rel_tab, b, s):
    qa, kva, ga, qb, kb, vb, iq, ik, iw = _split(w_in, EV_SIZES)
    half = 2 * A_KV * A_DH
    w_segs = [_pad_blocks(qa * (A_DH ** -0.5 * LOG2E), A_HEADS, A_DH), kva[:, :half],
              _pad_blocks(kva[:, half:], 4 * A_KV, A_DH), _pad_blocks(ga, A_KV, 3 * A_GRP),
              _pad_blocks(qb * (B_DH ** -0.5 * LOG2E), B_HEADS, B_DH),
              _pad_blocks(jnp.concatenate([kb, vb], axis=1), 2, B_DH),
              _pad_blocks(iq, IDX_HEADS, IDX_DH), _pad_blocks(ik, 1, IDX_DH), _pad_blocks(iw, 1, IDX_HEADS)]
    zeros = lambda w: jnp.zeros((1, w.shape[1]), F32)
    c_segs = [zeros(w) for w in w_segs]
    c_segs[2] = _ones_row(4 * A_KV, (2, 3, 6, 7))
    c_segs[5] = _ones_row(2, (1,))
    dts = [BF16, F32, BF16, F32, BF16, BF16, BF16, BF16, F32]
    q_a, kcvc, kvsw, g_a, q_b, kbvb, i_q, i_k, i_w = _proj(x2, g_pre, w_segs, c_segs, dts)
    r3 = lambda a: a.reshape(b, s, a.shape[-1])
    kvc = _nsa_compress(kcvc, cmp_pe, cmp_w, b, s)
    o_a = _nsa(r3(q_a), kvc, r3(kvsw), r3(g_a), rel_tab[:, :A_HEADS], b, s)
    o_b = _dsa(r3(q_b), r3(kbvb), r3(i_q), r3(i_k), r3(i_w), rel_tab[:, A_HEADS:A_HEADS + B_HEADS], b, s)
    na = A_HEADS * A_DH
    wa = _pad_rows(w_out[:na], A_HEADS, A_DH)
    wb = _pad_rows(w_out[na:], B_HEADS, B_DH)
    return _out_proj(o_a.reshape(b * s, -1), o_b.reshape(b * s, -1), wa, wb, x2, g_post)


def _odd_mixer(x2, g_pre, g_post, w_in, w_out, q_norm, kv_norm, w_uq, w_ukv, lam, subln, rel_tab,
               lam_init, b, s):
    c_q, c_kv, k_rope, qd, kd, vd = _split(w_in, OD_SIZES)
    w_kr = jnp.concatenate([_pad_blocks(k_rope, 1, C_DR, at=C_DN),
                            _pad_blocks(_rot_cols(k_rope), 1, C_DR, at=C_DN)], axis=1)
    w_segs = [c_q, c_kv, w_kr, _pad_blocks(qd * (D_DH ** -0.5 * LOG2E), 2 * D_HEADS, D_DH),
              _pad_blocks(kd, 2 * D_HEADS, D_DH), _pad_blocks(vd, D_HEADS, D_DV, to=2 * LANES)]
    c_segs = [jnp.zeros((1, w.shape[1]), F32) for w in w_segs]
    c_segs[5] = _ones_row(D_HEADS, range(D_HEADS), to=2 * LANES, lane=D_DV)
    dts = [F32, F32, F32, BF16, BF16, BF16]
    cq, ckv, kr, q_d, k_d, v_d = _proj(x2, g_pre, w_segs, c_segs, dts)
    q_m, k_m, v_m = _mla_prep(cq, ckv, kr, q_norm, kv_norm, w_uq, w_ukv, s)
    r3 = lambda a: a.reshape(b, s, a.shape[-1])
    o_c = _mla_attn(r3(q_m), r3(k_m), r3(v_m), b, s)
    lam32 = lam.astype(F32)
    lam_full = jnp.exp(jnp.sum(lam32[0] * lam32[1])) - jnp.exp(jnp.sum(lam32[2] * lam32[3])) + lam_init
    o_d = _diff_attn(r3(q_d), r3(k_d), r3(v_d), rel_tab[:, A_HEADS + B_HEADS:], lam_full, subln,
                     lam_init, b, s)
    nc = C_HEADS * C_DV
    wc = _pad_rows(w_out[:nc], C_HEADS, C_DV)
    return _out_proj(o_c.reshape(b * s, -1), o_d.reshape(b * s, -1), wc, w_out[nc:], x2, g_post)


def kernel(x, mem, rel_bias_table, norm_pre, norm_post, mem_norm, ffn_wg, ffn_wu, ffn_wd, ev_w_in,
           ev_cmp_pe, ev_cmp_w, ev_w_out, od_w_in, od_q_norm, od_kv_norm, od_w_uq, od_w_ukv, od_lambda,
           od_subln, od_w_out, xa_wq, xa_wkv, xa_wo):
    b, s, d = x.shape
    depth = norm_pre.shape[0]
    x2 = x.reshape(b * s, d)
    mem2 = mem.reshape(-1, d)
    for i in range(depth):
        g_pre, g_post = norm_pre[i], norm_post[i]
        x2 = _ffn(x2, g_pre[0], g_post[0], ffn_wg[i, 0], ffn_wu[i, 0], ffn_wd[i, 0])
        if i % 2 == 0:
            e = i // 2
            x2 = _even_mixer(x2, g_pre[1], g_post[1], ev_w_in[e], ev_w_out[e], ev_cmp_pe[e], ev_cmp_w[e],
                             rel_bias_table, b, s)
        else:
            o = i // 2
            lam_init = 0.8 - 0.6 * math.exp(-0.3 * i)
            x2 = _odd_mixer(x2, g_pre[1], g_post[1], od_w_in[o], od_w_out[o], od_q_norm[o], od_kv_norm[o],
                            od_w_uq[o], od_w_ukv[o], od_lambda[o], od_subln[o], rel_bias_table, lam_init, b, s)
        (kv,) = _proj(mem2, mem_norm[i], [xa_wkv[i]], [jnp.zeros((1, xa_wkv.shape[-1]), F32)], [BF16])
        x3 = _mem_xattn(x2.reshape(b, s, d), kv.reshape(b, -1, kv.shape[-1]), g_pre[2], g_post[2],
                        xa_wq[i], xa_wo[i])
        x2 = _ffn(x3.reshape(b * s, d), g_pre[3], g_post[3], ffn_wg[i, 1], ffn_wu[i, 1], ffn_wd[i, 1])
    return x2.reshape(b, s, d)
```

```python
import functools
import math

import numpy as np
import jax
import jax.numpy as jnp
from jax import lax
from jax.experimental import pallas as pl
from jax.experimental.pallas import tpu as pltpu

F32, BF16, I32 = jnp.float32, jnp.bfloat16, jnp.int32

N_BUCKETS = 32
MAX_DISTANCE = 128
RMS_EPS = 1e-6
ROPE_THETA = 10000.0
NEG_INF = -1e30
A_HEADS, A_KV, A_DH = 8, 2, 64
A_GRP = A_HEADS // A_KV
CMP_LEN, CMP_STRIDE, SLC_LEN, SLC_TOP, WIN = 32, 16, 64, 16, 512
FORCE_BONUS = 1e3
B_HEADS, B_DH, IDX_HEADS, IDX_DH, IDX_TOPK_MAX = 8, 64, 8, 32, 256
C_HEADS, C_Q_RANK, C_KV_RANK, C_DN, C_DR, C_DV = 8, 256, 128, 64, 32, 64
D_HEADS, D_DH, D_DV = 4, 64, 128
X_HEADS, X_DH = 4, 128
EV_SIZES = (A_HEADS * A_DH, 6 * A_KV * A_DH, 3 * A_HEADS,
            B_HEADS * B_DH, B_DH, B_DH, IDX_HEADS * IDX_DH, IDX_DH, IDX_HEADS)
OD_SIZES = (C_Q_RANK, C_KV_RANK, C_DR, D_HEADS * 2 * D_DH, D_HEADS * 2 * D_DH, D_HEADS * D_DV)

LANES = 128
VMEM_LIMIT_BYTES = 56 * 1024 * 1024
ONES_LANE = 64
INT_MIN = -2 ** 31
LOG2E = 1.4426950408889634

TOK_TILE = 512
FFN_CHUNK = 256
SPARSE_TQ = 128
WIDE_TILES = 4
DSA_TQ = 256
ACC_ROWS = 32
DENSE_TQ = 512
MLA_HEADS_PER_STEP = 4
DIFF_HEADS_PER_STEP = 2


def _params(*sem):
    return pltpu.CompilerParams(dimension_semantics=sem, vmem_limit_bytes=VMEM_LIMIT_BYTES)


def _const_spec(shape):
    nd = len(shape)
    return pl.BlockSpec(shape, lambda *_: (0,) * nd, pipeline_mode=pl.Buffered(1))


def _dot(a, b):
    return jnp.dot(a, b, preferred_element_type=F32)


def _dot_t(a, b):
    return lax.dot_general(a, b, (((1,), (1,)), ((), ())), preferred_element_type=F32)


def _rms(x, g):
    return x * lax.rsqrt(jnp.mean(x * x, axis=-1, keepdims=True) + RMS_EPS) * g


def _sigmoid(x):
    return 1.0 / (1.0 + jnp.exp(-x))


def _online_step(q, k, v, carry, add=None, shared=None, heads=1):
    m, acc = carry
    s = _dot_t(q, k)
    if add is not None:
        s = s + add
    if shared is not None:
        s = (s.reshape(heads, shared.shape[0], shared.shape[1]) + shared[None]).reshape(s.shape)
    m_new = jnp.maximum(m, jnp.max(s, axis=-1, keepdims=True))
    p = jnp.exp2(s - m_new)
    acc = jnp.exp2(m - m_new) * acc + _dot(p.astype(BF16), v)
    return m_new, acc


def _flash_init(m_rows, n_lanes):
    return jnp.full((m_rows, 1), -jnp.inf, F32), jnp.zeros((m_rows, n_lanes), F32)


def _ffn_body(x_ref, gpre_ref, gpost_ref, wgu_ref, wd_ref, o_ref, *, n_chunks, fc):
    x = x_ref[...]
    h = _rms(x, gpre_ref[...]).astype(BF16)
    acc = jnp.zeros(x.shape, F32)
    for c in range(n_chunks):
        gu = _dot(h, wgu_ref[:, c * 2 * fc:(c + 1) * 2 * fc])
        g, u = gu[:, :fc], gu[:, fc:]
        a = g * _sigmoid(g) * u
        acc = acc + _dot(a.astype(BF16), wd_ref[c * fc:(c + 1) * fc, :])
    o_ref[...] = x + 0.5 * _rms(acc, gpost_ref[...])


def _ffn(x, g_pre, g_post, wg, wu, wd):
    n, d = x.shape
    f = wg.shape[1]
    fc = FFN_CHUNK
    n_chunks = f // fc
    assert n_chunks * fc == f and n % TOK_TILE == 0
    wgu = jnp.concatenate([wg.reshape(d, n_chunks, fc), wu.reshape(d, n_chunks, fc)], axis=-1)
    wgu = wgu.reshape(d, 2 * f).astype(BF16)
    return pl.pallas_call(
        functools.partial(_ffn_body, n_chunks=n_chunks, fc=fc),
        grid=(n // TOK_TILE,),
        in_specs=[pl.BlockSpec((TOK_TILE, d), lambda i: (i, 0)),
                  _const_spec((1, d)), _const_spec((1, d)),
                  _const_spec((d, 2 * f)), _const_spec((f, d))],
        out_specs=pl.BlockSpec((TOK_TILE, d), lambda i: (i, 0)),
        out_shape=jax.ShapeDtypeStruct((n, d), F32),
        compiler_params=_params("arbitrary"),
        name="ffn",
    )(x, g_pre.reshape(1, d), g_post.reshape(1, d), wgu, wd.astype(BF16))


def _proj_body(x_ref, g_ref, w_ref, c_ref, *o_refs, bounds):
    h = _rms(x_ref[...], g_ref[...]).astype(BF16)
    for o_ref, (a, b) in zip(o_refs, bounds):
        o_ref[...] = (_dot(h, w_ref[:, a:b]) + c_ref[:, a:b]).astype(o_ref.dtype)


def _proj(x, g, w_segs, c_segs, dtypes):
    n, d = x.shape
    widths = [w.shape[1] for w in w_segs]
    assert all(wd_ % LANES == 0 for wd_ in widths) and n % TOK_TILE == 0
    offs = np.concatenate([[0], np.cumsum(widths)])
    bounds = tuple((int(offs[i]), int(offs[i + 1])) for i in range(len(widths)))
    w = jnp.concatenate(w_segs, axis=1).astype(BF16)
    c = jnp.concatenate(c_segs, axis=1).astype(F32)
    tot = int(offs[-1])
    return pl.pallas_call(
        functools.partial(_proj_body, bounds=bounds),
        grid=(n // TOK_TILE,),
        in_specs=[pl.BlockSpec((TOK_TILE, d), lambda i: (i, 0)),
                  _const_spec((1, d)), _const_spec((d, tot)), _const_spec((1, tot))],
        out_specs=[pl.BlockSpec((TOK_TILE, wd_), lambda i: (i, 0)) for wd_ in widths],
        out_shape=[jax.ShapeDtypeStruct((n, wd_), dt) for wd_, dt in zip(widths, dtypes)],
        compiler_params=_params("arbitrary"),
        name="norm_proj",
    )(x, g.reshape(1, d), w, c)


def _out_body(a_ref, b_ref, wa_ref, wb_ref, x_ref, g_ref, o_ref):
    y = _dot(a_ref[...], wa_ref[...]) + _dot(b_ref[...], wb_ref[...])
    o_ref[...] = x_ref[...] + _rms(y, g_ref[...])


def _out_proj(oa, ob, wa, wb, x, g_post):
    n, d = x.shape
    ka, kb = oa.shape[1], ob.shape[1]
    return pl.pallas_call(
        _out_body,
        grid=(n // TOK_TILE,),
        in_specs=[pl.BlockSpec((TOK_TILE, ka), lambda i: (i, 0)),
                  pl.BlockSpec((TOK_TILE, kb), lambda i: (i, 0)),
                  _const_spec((ka, d)), _const_spec((kb, d)),
                  pl.BlockSpec((TOK_TILE, d), lambda i: (i, 0)),
                  _const_spec((1, d))],
        out_specs=pl.BlockSpec((TOK_TILE, d), lambda i: (i, 0)),
        out_shape=jax.ShapeDtypeStruct((n, d), F32),
        compiler_params=_params("arbitrary"),
        name="out_proj",
    )(oa, ob, wa.astype(BF16), wb.astype(BF16), x, g_post.reshape(1, d))


def _xattn_body(x_ref, gpre_ref, gpost_ref, wq_ref, kv_ref, wo_ref, o_ref, *, heads, dh):
    x = x_ref[0]
    h = _rms(x, gpre_ref[...]).astype(BF16)
    q = _dot(h, wq_ref[...]).astype(BF16)
    kv = kv_ref[0]
    scale = dh ** -0.5
    outs = []
    for hh in range(heads):
        k = kv[:, hh * dh:(hh + 1) * dh]
        v = kv[:, (heads + hh) * dh:(heads + hh + 1) * dh]
        s = _dot_t(q[:, hh * dh:(hh + 1) * dh], k) * scale
        p = jnp.exp(s - jnp.max(s, axis=-1, keepdims=True))
        p = p / jnp.sum(p, axis=-1, keepdims=True)
        outs.append(_dot(p.astype(BF16), v).astype(BF16))
    y = _dot(jnp.concatenate(outs, axis=-1), wo_ref[...])
    o_ref[0] = x + _rms(y, gpost_ref[...])


def _mem_xattn(x, kv, g_pre, g_post, wq, wo):
    b, s, d = x.shape
    mlen, kvw = kv.shape[1:]
    hd = X_HEADS * X_DH
    return pl.pallas_call(
        functools.partial(_xattn_body, heads=X_HEADS, dh=X_DH),
        grid=(b, s // TOK_TILE),
        in_specs=[pl.BlockSpec((1, TOK_TILE, d), lambda i, j: (i, j, 0)),
                  _const_spec((1, d)), _const_spec((1, d)), _const_spec((d, hd)),
                  pl.BlockSpec((1, mlen, kvw), lambda i, j: (i, 0, 0)),
                  _const_spec((hd, d))],
        out_specs=pl.BlockSpec((1, TOK_TILE, d), lambda i, j: (i, j, 0)),
        out_shape=jax.ShapeDtypeStruct((b, s, d), F32),
        compiler_params=_params("arbitrary", "arbitrary"),
        name="mem_xattn",
    )(x, g_pre.reshape(1, d), g_post.reshape(1, d), wq.astype(BF16), kv, wo.astype(BF16))


def _bucket_np(dist):
    max_exact = N_BUCKETS // 2
    d = np.maximum(dist, 0)
    ratio = np.log(np.maximum(d, 1).astype(np.float32) / np.float32(max_exact)) / np.float32(
        math.log(MAX_DISTANCE / max_exact))
    large = np.minimum(max_exact + (ratio * np.float32(N_BUCKETS - max_exact)).astype(np.int32),
                       N_BUCKETS - 1)
    return np.where(d < max_exact, d, large).astype(np.int32)


def _rel_bias(tab, dist, visible, rel_to_far):
    dist = np.maximum(np.asarray(dist), 0).astype(np.int32)
    bk = _bucket_np(np.arange(max(int(dist.max()), 4 * MAX_DISTANCE) + 1))
    assert (np.diff(bk) >= 0).all() and bk[-1] == N_BUCKETS - 1
    tab = tab.astype(F32)
    col = lambda b: tab[b].reshape((-1,) + (1,) * dist.ndim)
    d = jnp.asarray(dist)[None]
    out = jnp.broadcast_to(col(0), (tab.shape[1],) + dist.shape)
    for b in range(1, N_BUCKETS):
        first = np.nonzero(bk == b)[0]
        if first.size:
            out = jnp.where(d >= int(first[0]), col(b), out)
    if rel_to_far:
        out = out - col(N_BUCKETS - 1)
    return jnp.where(jnp.asarray(np.asarray(visible))[None], out * LOG2E, NEG_INF)


def _far_distance():
    bk = _bucket_np(np.arange(4 * MAX_DISTANCE))
    return int(np.nonzero(bk == N_BUCKETS - 1)[0][0])


def _bias_tiles(tab, t):
    assert t + 1 >= _far_distance()
    r = np.arange(t)[:, None]
    c = np.arange(t)[None, :]
    dist = np.stack([r - c, t + r - c, 2 * t + r - c])
    everywhere = np.ones((t, t), bool)
    return _rel_bias(tab, dist, np.stack([r >= c, everywhere, everywhere]), True)


def _near_tiles(tab, t):
    tiles = _bias_tiles(tab, t)
    hidden = jnp.full_like(tiles[:, 0], NEG_INF)
    return jnp.stack([jnp.concatenate([tiles[:, 1], tiles[:, 0]], axis=-1),
                      jnp.concatenate([tiles[:, 0], hidden], axis=-1)], axis=1)


def _key_loops(qt, tq, wide, wide_step, near_step, init):
    far = jnp.maximum(qt - 1, 0) * tq
    tw = wide * tq
    n_wide = (far + tw - 1) // tw

    def body(j, carry):
        start = pl.multiple_of(jnp.maximum(far - (n_wide - j) * tw, 0), tq)
        return wide_step(start, jnp.where(j == 0, far - (n_wide - 1) * tw, tw), carry)

    carry = lax.fori_loop(0, n_wide, body, init)
    return near_step(pl.multiple_of(far, tq), jnp.where(qt == 0, 1, 0), carry)


def _cmp_body(x_ref, pe_ref, w_ref, o_ref):
    x = x_ref[0]
    xn = pltpu.roll(x, x.shape[0] - 1, 0)
    lo = (x + pe_ref[0:1, :]).astype(BF16)
    hi = (xn + pe_ref[1:2, :]).astype(BF16)
    o_ref[0] = (_dot(lo, w_ref[0]) + _dot(hi, w_ref[1])).astype(BF16)


def _nsa_compress(kcvc, cmp_pe, cmp_w, b, s):
    nb = s // CMP_STRIDE
    halves = CMP_LEN // CMP_STRIDE
    n_grp = 2 * A_KV
    kdim = CMP_STRIDE * n_grp * A_DH
    cw = cmp_w.reshape(2, halves, CMP_STRIDE, A_DH, A_DH)
    same = jnp.eye(n_grp, dtype=F32).reshape(2, A_KV, 2, A_KV)
    w = jnp.einsum("khlde,kgKG->hlkgdKGe", cw, same)
    w = jnp.pad(w, [(0, 0)] * 7 + [(0, LANES - A_DH)]).reshape(halves, kdim, n_grp * LANES).astype(BF16)
    pe = cmp_pe.reshape(2, halves, CMP_STRIDE, 1, A_DH).transpose(1, 2, 0, 3, 4)
    pe = jnp.broadcast_to(pe, (halves, CMP_STRIDE, 2, A_KV, A_DH)).reshape(halves, kdim)
    return pl.pallas_call(
        _cmp_body,
        grid=(b,),
        in_specs=[pl.BlockSpec((1, nb, kdim), lambda i: (i, 0, 0)),
                  _const_spec((halves, kdim)), _const_spec((halves, kdim, n_grp * LANES))],
        out_specs=pl.BlockSpec((1, nb, n_grp * LANES), lambda i: (i, 0, 0)),
        out_shape=jax.ShapeDtypeStruct((b, nb, n_grp * LANES), BF16),
        compiler_params=_params("arbitrary"),
        name="nsa_compress",
    )(kcvc.reshape(b, nb, kdim), pe, w)


def _nsa_body(q_ref, kc_ref, vc_ref, bc_ref, ks_ref, vs_ref, kw_ref, vw_ref, ga_ref, bias_ref,
              near_ref, wbias_ref, ovl_ref, exp_ref, o_ref, mask_ref, *, tq, seq, n_sb):
    qt = pl.program_id(2)
    rr = A_GRP
    m_rows = rr * tq
    ncp = kc_ref.shape[1]
    q = q_ref[0]
    qs = jnp.concatenate([q[:, r * LANES:(r + 1) * LANES] for r in range(rr)], axis=0)

    s_c = _dot_t(qs, kc_ref[0]) + bc_ref[...].reshape(m_rows, ncp)
    p = jnp.where(s_c > 0.5 * NEG_INF, jnp.exp2(s_c - jnp.max(s_c, axis=-1, keepdims=True)), 0.0)
    l = jnp.sum(p, axis=-1, keepdims=True)
    pn = p * jnp.where(l > 0.0, 1.0 / l, 0.0)
    o_c = _dot(pn.astype(BF16), vc_ref[0])

    imp = jnp.sum(pn.reshape(rr, tq, ncp), axis=0)
    imp_t = lax.dot_general(ovl_ref[...], imp, (((1,), (1,)), ((), ())),
                            precision=lax.Precision.HIGHEST, preferred_element_type=F32)
    blk = lax.broadcasted_iota(I32, (LANES, tq), 0)
    cur = (qt * tq + lax.broadcasted_iota(I32, (LANES, tq), 1)) // SLC_LEN
    forced = (blk == 0) | (blk == cur) | (blk == cur - 1)
    score = jnp.where(blk <= cur, imp_t + jnp.where(forced, FORCE_BONUS, 0.0), NEG_INF)
    groups = [score[8 * v:8 * v + 8] for v in range(-(-n_sb // 8))]
    ranks = [jnp.zeros((8, tq), F32) for _ in groups]
    sub = lax.broadcasted_iota(I32, (8, tq), 0)
    for i in range(n_sb):
        ci = score[i:i + 1, :]
        for v, sv in enumerate(groups):
            ge = jnp.where(ci >= sv, 1.0, 0.0)
            gt = jnp.where(ci > sv, 1.0, 0.0)
            if 8 * v > i:
                ranks[v] = ranks[v] + ge
            elif 8 * v + 7 < i:
                ranks[v] = ranks[v] + gt
            else:
                ranks[v] = ranks[v] + jnp.where(sub + 8 * v > i, ge, gt)
    rank = jnp.concatenate(ranks + [jnp.full((LANES - 8 * len(groups), tq), float(LANES), F32)], axis=0)
    sel_t = jnp.where(rank < float(min(SLC_TOP, n_sb)), jnp.where(blk <= cur, 1.0, 0.0), 0.0)
    selb = sel_t.T.astype(BF16)

    def expand(c, _):
        cs = pl.multiple_of(c * tw, tw)
        mask_ref[:, pl.ds(cs, tw)] = (_dot(selb, exp_ref[:, pl.ds(cs, tw)]) - 1.0) * (-NEG_INF)
        return 0

    tw = WIDE_TILES * tq
    lax.fori_loop(0, qt // WIDE_TILES + 1, expand, 0)

    col_w = lax.broadcasted_iota(I32, (tq, tw), 1)

    def slc_wide(ks, limit, carry):
        own = jnp.where(col_w < limit, mask_ref[:, pl.ds(ks, tw)], NEG_INF)
        return _online_step(qs, ks_ref[0, pl.ds(ks, tw), :], vs_ref[0, pl.ds(ks, tw), :], carry,
                            shared=own, heads=rr)

    def slc_near(ks, which, carry):
        return _online_step(qs, ks_ref[0, pl.ds(ks, 2 * tq), :], vs_ref[0, pl.ds(ks, 2 * tq), :], carry,
                            add=near_ref[0, which], shared=mask_ref[:, pl.ds(ks, 2 * tq)], heads=rr)

    _, acc_s = _key_loops(qt, tq, WIDE_TILES, slc_wide, slc_near, _flash_init(m_rows, LANES))

    n_woff = WIN // tq

    def win(kt, carry):
        ks = pl.multiple_of(kt * tq, tq)
        return _online_step(qs, kw_ref[0, pl.ds(ks, tq), :], vw_ref[0, pl.ds(ks, tq), :], carry,
                            add=bias_ref[0, jnp.minimum(qt - kt, 2)])

    def win_full():
        ks = pl.multiple_of((qt - n_woff) * tq, tq)
        return _online_step(qs, kw_ref[0, pl.ds(ks, WIN), :], vw_ref[0, pl.ds(ks, WIN), :],
                            _flash_init(m_rows, LANES), add=wbias_ref[0])

    def win_head():
        return lax.fori_loop(0, qt, win, _flash_init(m_rows, LANES))

    _, acc_w = win(qt, lax.cond(qt >= n_woff, win_full, win_head))

    gs = _sigmoid(ga_ref[0])
    for r in range(rr):
        a_s = acc_s[r * tq:(r + 1) * tq]
        a_w = acc_w[r * tq:(r + 1) * tq]
        o = (gs[:, 3 * r:3 * r + 1] * o_c[r * tq:(r + 1) * tq]
             + gs[:, 3 * r + 1:3 * r + 2] * (a_s / a_s[:, ONES_LANE:ONES_LANE + 1])
             + gs[:, 3 * r + 2:3 * r + 3] * (a_w / a_w[:, ONES_LANE:ONES_LANE + 1]))
        o_ref[0, :, r * LANES:(r + 1) * LANES] = o.astype(BF16)


def _nsa(q, kvc, kvsw, ga, tab, b, s):
    tq = SPARSE_TQ
    ncp = s // CMP_STRIDE
    n_sb = s // SLC_LEN
    assert n_sb <= LANES and WIN % tq == 0 and s % (WIDE_TILES * tq) == 0
    t = np.arange(s)[:, None]
    cmp_end = np.arange(ncp)[None, :] * CMP_STRIDE + CMP_LEN - 1
    bias_c = _rel_bias(tab, t - cmp_end, cmp_end <= t, False)
    tiles = _bias_tiles(tab, tq)
    r = np.arange(tq)[:, None]
    c = np.arange(tq)[None, :]
    oldest = jnp.broadcast_to(jnp.asarray(np.where(r < c, 0.0, NEG_INF).astype(np.float32)), (A_HEADS, tq, tq))
    n_woff = WIN // tq
    wbias = jnp.concatenate([oldest] + [tiles[:, 2]] * (n_woff - 2) + [tiles[:, 1]], axis=-1)
    wbias = wbias.reshape(A_KV, A_GRP * tq, WIN)
    by_group = lambda a: a.reshape((A_KV, A_GRP) + a.shape[1:]).transpose(0, 2, 1, 3, 4).reshape(
        A_KV, a.shape[1], A_GRP * tq, a.shape[3])
    near = by_group(_near_tiles(tab, tq))
    tiles = by_group(tiles)
    cmp_start = np.arange(ncp)[None, :] * CMP_STRIDE
    sb_start = np.arange(LANES)[:, None] * SLC_LEN
    ovl = ((cmp_start < sb_start + SLC_LEN) & (cmp_start + CMP_LEN - 1 >= sb_start)
           & (np.arange(ncp)[None, :] < ncp - 1) & (np.arange(LANES)[:, None] < n_sb))
    ovl = jnp.asarray(ovl.astype(np.float32))
    expand = jnp.asarray((np.arange(LANES)[:, None] == np.arange(s)[None, :] // SLC_LEN).astype(np.float32), BF16)
    kv_spec = lambda blk: pl.BlockSpec((1, s, LANES), lambda i, g, j: (i, 0, 2 * blk + g))
    return pl.pallas_call(
        functools.partial(_nsa_body, tq=tq, seq=s, n_sb=n_sb),
        grid=(b, A_KV, s // tq),
        in_specs=[pl.BlockSpec((1, tq, A_GRP * LANES), lambda i, g, j: (i, j, g)),
                  pl.BlockSpec((1, ncp, LANES), lambda i, g, j: (i, 0, g)),
                  pl.BlockSpec((1, ncp, LANES), lambda i, g, j: (i, 0, A_KV + g)),
                  pl.BlockSpec((A_GRP, tq, ncp), lambda i, g, j: (g, j, 0)),
                  kv_spec(0), kv_spec(1), kv_spec(2), kv_spec(3),
                  pl.BlockSpec((1, tq, LANES), lambda i, g, j: (i, j, g)),
                  pl.BlockSpec((1, 3, A_GRP * tq, tq), lambda i, g, j: (g, 0, 0, 0)),
                  pl.BlockSpec((1, 2, A_GRP * tq, 2 * tq), lambda i, g, j: (g, 0, 0, 0)),
                  pl.BlockSpec((1, A_GRP * tq, WIN), lambda i, g, j: (g, 0, 0)),
                  _const_spec((LANES, ncp)), _const_spec((LANES, s))],
        out_specs=pl.BlockSpec((1, tq, A_GRP * LANES), lambda i, g, j: (i, j, g)),
        out_shape=jax.ShapeDtypeStruct((b, s, A_HEADS * LANES), BF16),
        scratch_shapes=[pltpu.VMEM((tq, s), F32)],
        compiler_params=_params("arbitrary", "arbitrary", "arbitrary"),
        name="nsa_attn",
    )(q, kvc, kvc, bias_c, kvsw, kvsw, kvsw, kvsw, ga, tiles, near, wbias, ovl, expand)


def _dsat_body(qt_ref, k_ref, vt_ref, iqt_ref, ik_ref, iwt_ref, bias_ref, o_ref, keys_ref, *,
               tq, seq, k_top):
    qi = pl.program_id(1)
    nh = B_HEADS
    tw = 2 * tq
    n_wide = qi // 2 + 1
    key_i = lax.broadcasted_iota(I32, (tw, tq), 0)
    qry_i = lax.broadcasted_iota(I32, (tw, tq), 1)

    iwt = iwt_ref[0]

    def scores(kw, _):
        ks = pl.multiple_of(kw * tw, tw)
        ik = ik_ref[0, pl.ds(ks, tw), :]
        sc = jnp.zeros((tw, tq), F32)
        for h in range(IDX_HEADS):
            lg = _dot(ik, iqt_ref[0, h * LANES:(h + 1) * LANES, :])
            sc = sc + jnp.maximum(lg, 0.0) * iwt[h:h + 1, :]
        bits = lax.bitcast_convert_type(sc, I32)
        key = jnp.where(bits < 0, bits ^ 0x7FFFFFFF, bits)
        vis = (key_i + ks) <= (qry_i + qi * tq)
        keys_ref[pl.ds(ks, tw), :] = jnp.where(vis, key, INT_MIN)
        return 0

    lax.fori_loop(0, n_wide, scores, 0)

    def count(ind):
        def body(kw, acc):
            ks = pl.multiple_of(kw * tw, tw)
            x = ind(keys_ref[pl.ds(ks, tw), :], kw)
            return acc + jnp.sum(x.reshape(tw // ACC_ROWS, ACC_ROWS, tq), axis=0)
        acc = lax.fori_loop(0, n_wide, body, jnp.zeros((ACC_ROWS, tq), F32))
        return jnp.sum(acc, axis=0, keepdims=True)

    kf = float(k_top)
    c0 = count(lambda t, kw: jnp.where(t >= 0, 1.0, 0.0))
    thr = jnp.where(c0 >= kf, 0, INT_MIN).astype(I32)
    c_ge = jnp.where(c0 >= kf, c0, float(seq))

    def bisect(i, state):
        thr, c_ge = state
        cand = thr | (jnp.int32(1) << (30 - i))
        c = count(lambda t, kw: jnp.where(t >= cand, 1.0, 0.0))
        return jnp.where(c >= kf, cand, thr), jnp.where(c >= kf, c, c_ge)

    thr, c_ge = lax.fori_loop(0, 31, bisect, (thr, c_ge))

    n_bits = int(math.log2(seq))
    assert 2 ** n_bits == seq

    def cutoff():
        need = kf - count(lambda t, kw: jnp.where(t > thr, 1.0, 0.0))

        def step(i, cut):
            cand = cut | (jnp.int32(1) << (n_bits - 1 - i))
            c = count(lambda t, kw: jnp.where(t == thr, jnp.where(key_i + kw * tw < cand, 1.0, 0.0), 0.0))
            return jnp.where(c < need, cand, cut)
        return lax.fori_loop(0, n_bits, step, jnp.zeros((1, tq), I32))

    excess = jnp.max(jnp.where(c_ge > kf, 1.0, 0.0)) > 0.5
    cut = lax.cond(excess, cutoff, lambda: jnp.full((1, tq), seq, I32))

    q_all = jnp.concatenate([qt_ref[0, h * LANES:(h + 1) * LANES, :] for h in range(nh)], axis=1)

    def step(ks, own, bias, carry):
        m, acc = carry
        key = keys_ref[pl.ds(ks, tw), :]
        tie = jnp.where(key_i + ks <= cut, 0.0, NEG_INF)
        sel = jnp.where(key > thr, 0.0, jnp.where(key == thr, tie, NEG_INF))
        if own is not None:
            sel = jnp.where(own, sel, NEG_INF)
        s = _dot(k_ref[0, pl.ds(ks, tw), :], q_all) + jnp.concatenate([sel] * nh, axis=1)
        if bias is not None:
            s = s + bias
        m_new = jnp.maximum(m, jnp.max(s, axis=0, keepdims=True))
        p = jnp.exp2(s - m_new)
        return m_new, jnp.exp2(m - m_new) * acc + _dot(vt_ref[0, :, pl.ds(ks, tw)], p.astype(BF16))

    def wide_step(ks, limit, carry):
        return step(ks, key_i < limit, None, carry)

    def near_step(ks, which, carry):
        return step(ks, None, bias_ref[which], carry)

    init = jnp.full((1, nh * tq), -jnp.inf, F32), jnp.zeros((vt_ref.shape[1], nh * tq), F32)
    _, acc = _key_loops(qi, tq, 2, wide_step, near_step, init)
    for h in range(nh):
        a = acc[:, h * tq:(h + 1) * tq]
        o_ref[0, h * B_DH:(h + 1) * B_DH, :] = (a[:B_DH] / a[B_DH:B_DH + 1]).astype(BF16)


def _dsat(q, kbvb, iq, ik, iw, tab, b, s):
    tq = DSA_TQ
    k_top = min(IDX_TOPK_MAX, s // 4)
    assert s % (2 * tq) == 0
    qt = jnp.swapaxes(q, 1, 2)
    iqt = jnp.swapaxes(iq, 1, 2)
    iwt = jnp.swapaxes(iw[:, :, :IDX_HEADS], 1, 2)
    n_vt = B_DH + 16
    vt = jnp.swapaxes(kbvb[:, :, LANES:LANES + n_vt], 1, 2)
    near = _near_tiles(tab, tq).transpose(1, 3, 0, 2).reshape(2, 2 * tq, B_HEADS * tq)
    o_t = pl.pallas_call(
        functools.partial(_dsat_body, tq=tq, seq=s, k_top=k_top),
        grid=(b, s // tq),
        in_specs=[pl.BlockSpec((1, B_HEADS * LANES, tq), lambda i, j: (i, 0, j)),
                  pl.BlockSpec((1, s, LANES), lambda i, j: (i, 0, 0)),
                  pl.BlockSpec((1, n_vt, s), lambda i, j: (i, 0, 0)),
                  pl.BlockSpec((1, IDX_HEADS * LANES, tq), lambda i, j: (i, 0, j)),
                  pl.BlockSpec((1, s, LANES), lambda i, j: (i, 0, 0)),
                  pl.BlockSpec((1, IDX_HEADS, tq), lambda i, j: (i, 0, j)),
                  _const_spec((2, 2 * tq, B_HEADS * tq))],
        out_specs=pl.BlockSpec((1, B_HEADS * B_DH, tq), lambda i, j: (i, 0, j)),
        out_shape=jax.ShapeDtypeStruct((b, B_HEADS * B_DH, s), BF16),
        scratch_shapes=[pltpu.VMEM((s, tq), I32)],
        compiler_params=_params("arbitrary", "arbitrary"),
        name="dsa_attn",
    )(qt, kbvb, vt, iqt, ik, iwt, near)
    return jnp.swapaxes(o_t, 1, 2)


def _mla_prep_body(cq_ref, ckv_ref, kr_ref, qn_ref, kvn_ref, wq1_ref, wq2_ref, wk_ref, wv_ref,
                   cos_ref, sin_ref, ones_ref, q_o, k_o, v_o):
    cqn = _rms(cq_ref[...], qn_ref[...]).astype(BF16)
    ckn = _rms(ckv_ref[...], kvn_ref[...]).astype(BF16)
    cos, sin = cos_ref[...], sin_ref[...]
    cos_h = jnp.concatenate([cos] * C_HEADS, axis=1)
    sin_h = jnp.concatenate([sin] * C_HEADS, axis=1)
    q_o[...] = (_dot(cqn, wq1_ref[...]) * cos_h + _dot(cqn, wq2_ref[...]) * sin_h).astype(BF16)
    kr = kr_ref[...]
    kr_rot = kr[:, :LANES] * cos + kr[:, LANES:] * sin
    k_o[...] = (_dot(ckn, wk_ref[...]) + jnp.concatenate([kr_rot] * C_HEADS, axis=1)).astype(BF16)
    v_o[...] = (_dot(ckn, wv_ref[...]) + ones_ref[...]).astype(BF16)


def _rot_cols(w):
    half = C_DR // 2
    return jnp.concatenate([-w[..., half:], w[..., :half]], axis=-1)


def _mla_prep(cq, ckv, kr, q_norm, kv_norm, w_uq, w_ukv, s):
    n = cq.shape[0]
    hw = C_HEADS * LANES
    padq = LANES - C_DN - C_DR
    wq = w_uq.reshape(C_Q_RANK, C_HEADS, C_DN + C_DR) * ((C_DN + C_DR) ** -0.5 * LOG2E)
    wq1 = jnp.pad(wq, ((0, 0), (0, 0), (0, padq))).reshape(C_Q_RANK, hw)
    wq2 = jnp.pad(_rot_cols(wq[..., C_DN:]), ((0, 0), (0, 0), (C_DN, padq))).reshape(C_Q_RANK, hw)
    wkv = w_ukv.reshape(C_KV_RANK, C_HEADS, C_DN + C_DV)
    wk = jnp.pad(wkv[..., :C_DN], ((0, 0), (0, 0), (0, LANES - C_DN))).reshape(C_KV_RANK, hw)
    wv = jnp.pad(wkv[..., C_DN:], ((0, 0), (0, 0), (0, LANES - C_DV))).reshape(C_KV_RANK, hw)
    half = C_DR // 2
    inv = ROPE_THETA ** (-jnp.arange(half, dtype=F32) / half)
    ang = jnp.arange(s, dtype=F32)[:, None] * inv[None, :]
    cos, sin = jnp.cos(ang), jnp.sin(ang)
    cos_t = jnp.concatenate([jnp.ones((s, C_DN), F32), cos, cos, jnp.zeros((s, padq), F32)], axis=1)
    sin_t = jnp.concatenate([jnp.zeros((s, C_DN), F32), sin, sin, jnp.zeros((s, padq), F32)], axis=1)
    ones = jnp.tile(jnp.asarray(np.arange(LANES) == ONES_LANE, F32), C_HEADS).reshape(1, hw)
    tile_tok = lambda w_: pl.BlockSpec((TOK_TILE, w_), lambda i: (i, 0))
    n_pos = s // TOK_TILE
    pos_spec = pl.BlockSpec((TOK_TILE, LANES), lambda i: (i % n_pos, 0))
    return pl.pallas_call(
        _mla_prep_body,
        grid=(n // TOK_TILE,),
        in_specs=[tile_tok(C_Q_RANK), tile_tok(C_KV_RANK), tile_tok(2 * LANES),
                  _const_spec((1, C_Q_RANK)), _const_spec((1, C_KV_RANK)),
                  _const_spec((C_Q_RANK, hw)), _const_spec((C_Q_RANK, hw)),
                  _const_spec((C_KV_RANK, hw)), _const_spec((C_KV_RANK, hw)),
                  pos_spec, pos_spec, _const_spec((1, hw))],
        out_specs=[tile_tok(hw)] * 3,
        out_shape=[jax.ShapeDtypeStruct((n, hw), BF16)] * 3,
        compiler_params=_params("arbitrary"),
        name="mla_prep",
    )(cq, ckv, kr, q_norm.reshape(1, -1), kv_norm.reshape(1, -1), wq1.astype(BF16), wq2.astype(BF16),
      wk.astype(BF16), wv.astype(BF16), cos_t, sin_t, ones)


def _mla_attn_body(q_ref, k_ref, v_ref, o_ref, *, tq, nh):
    qt = pl.program_id(2)
    q = q_ref[0]
    lanes = lambda a, h: a[:, h * LANES:(h + 1) * LANES]

    def step(kt, carry, add):
        ks = pl.multiple_of(kt * tq, tq)
        k = k_ref[0, pl.ds(ks, tq), :]
        v = v_ref[0, pl.ds(ks, tq), :]
        return tuple(_online_step(lanes(q, h), lanes(k, h), lanes(v, h), carry[h], add=add)
                     for h in range(nh))

    init = tuple(_flash_init(tq, LANES) for _ in range(nh))
    carry = lax.fori_loop(0, qt, lambda kt, c: step(kt, c, None), init)
    row = lax.broadcasted_iota(I32, (tq, tq), 0)
    col = lax.broadcasted_iota(I32, (tq, tq), 1)
    carry = step(qt, carry, jnp.where(row >= col, 0.0, NEG_INF))
    for h in range(nh):
        acc = carry[h][1]
        o_ref[0, :, h * LANES:(h + 1) * LANES] = (acc / acc[:, ONES_LANE:ONES_LANE + 1]).astype(BF16)


def _mla_attn(q, k, v, b, s):
    tq = DENSE_TQ
    nh = MLA_HEADS_PER_STEP
    spec_q = pl.BlockSpec((1, tq, nh * LANES), lambda i, h, j: (i, j, h))
    spec_kv = pl.BlockSpec((1, s, nh * LANES), lambda i, h, j: (i, 0, h))
    return pl.pallas_call(
        functools.partial(_mla_attn_body, tq=tq, nh=nh),
        grid=(b, C_HEADS // nh, s // tq),
        in_specs=[spec_q, spec_kv, spec_kv],
        out_specs=spec_q,
        out_shape=jax.ShapeDtypeStruct((b, s, C_HEADS * LANES), BF16),
        compiler_params=_params("arbitrary", "arbitrary", "arbitrary"),
        name="mla_attn",
    )(q, k, v)


def _diff_body(lam_ref, q_ref, k_ref, v_ref, bias_ref, g_ref, o_ref, *, tq, nh, out_scale):
    qt = pl.program_id(2)
    q = q_ref[0]
    lanes = lambda a, c, w=LANES: a[:, c * w:(c + 1) * w]

    def step(kt, carry, near):
        ks = pl.multiple_of(kt * tq, tq)
        k = k_ref[0, pl.ds(ks, tq), :]
        v = v_ref[0, pl.ds(ks, tq), :]
        return tuple(_online_step(lanes(q, c), lanes(k, c), lanes(v, c // 2, 2 * LANES), carry[c],
                                  add=bias_ref[c // 2, qt - kt] if near else None)
                     for c in range(2 * nh))

    n_far = jnp.maximum(qt - 1, 0)
    init = tuple(_flash_init(tq, 2 * LANES) for _ in range(2 * nh))
    carry = lax.fori_loop(0, n_far, functools.partial(step, near=False), init)
    carry = lax.fori_loop(n_far, qt + 1, functools.partial(step, near=True), carry)
    outs = [carry[c][1][:, :D_DV] / carry[c][1][:, D_DV:D_DV + 1] for c in range(2 * nh)]
    for h in range(nh):
        o = outs[2 * h] - lam_ref[0] * outs[2 * h + 1]
        o_ref[0, :, h * D_DV:(h + 1) * D_DV] = (_rms(o, g_ref[...]) * out_scale).astype(BF16)


def _diff_attn(q, k, v, tab, lam_full, subln, lam_init, b, s):
    tq = DENSE_TQ
    nh = DIFF_HEADS_PER_STEP
    tiles = _bias_tiles(tab, tq)[:, :2]
    spec_q = pl.BlockSpec((1, tq, nh * 2 * LANES), lambda i, h, j: (i, j, h))
    spec_kv = pl.BlockSpec((1, s, nh * 2 * LANES), lambda i, h, j: (i, 0, h))
    return pl.pallas_call(
        functools.partial(_diff_body, tq=tq, nh=nh, out_scale=1.0 - lam_init),
        grid=(b, D_HEADS // nh, s // tq),
        in_specs=[pl.BlockSpec(memory_space=pltpu.SMEM),
                  spec_q, spec_kv, spec_kv,
                  pl.BlockSpec((nh, 2, tq, tq), lambda i, h, j: (h, 0, 0, 0)),
                  _const_spec((1, D_DV))],
        out_specs=pl.BlockSpec((1, tq, nh * D_DV), lambda i, h, j: (i, j, h)),
        out_shape=jax.ShapeDtypeStruct((b, s, D_HEADS * D_DV), BF16),
        compiler_params=_params("arbitrary", "arbitrary", "arbitrary"),
        name="diff_attn",
    )(lam_full.reshape(1), q, k, v, tiles, subln.reshape(1, D_DV))


def _pad_blocks(w, n, width, to=LANES, at=0):
    k = w.shape[0]
    return jnp.pad(w.reshape(k, n, width), ((0, 0), (0, 0), (at, to - width - at))).reshape(k, n * to)


def _ones_row(n_blocks, which, to=LANES, lane=ONES_LANE):
    row = np.zeros((n_blocks, to), np.float32)
    row[list(which), lane] = 1.0
    return jnp.asarray(row.reshape(1, n_blocks * to))


def _split(w, sizes):
    return jnp.split(w, [int(c) for c in np.cumsum(sizes)[:-1]], axis=-1)


def _pad_rows(w, n, width):
    d = w.shape[1]
    return jnp.pad(w.reshape(n, width, d), ((0, 0), (0, LANES - width), (0, 0))).reshape(n * LANES, d)


def _even_mixer(x2, g_pre, g_post, w_in, w_out, cmp_pe, cmp_w, rel_tab, b, s):
    qa, kva, ga, qb, kb, vb, iq, ik, iw = _split(w_in, EV_SIZES)
    half = 2 * A_KV * A_DH
    w_segs = [_pad_blocks(qa * (A_DH ** -0.5 * LOG2E), A_HEADS, A_DH), kva[:, :half],
              _pad_blocks(kva[:, half:], 4 * A_KV, A_DH), _pad_blocks(ga, A_KV, 3 * A_GRP),
              _pad_blocks(qb * (B_DH ** -0.5 * LOG2E), B_HEADS, B_DH),
              _pad_blocks(jnp.concatenate([kb, vb], axis=1), 2, B_DH),
              _pad_blocks(iq, IDX_HEADS, IDX_DH), _pad_blocks(ik, 1, IDX_DH), _pad_blocks(iw, 1, IDX_HEADS)]
    zeros = lambda w: jnp.zeros((1, w.shape[1]), F32)
    c_segs = [zeros(w) for w in w_segs]
    c_segs[2] = _ones_row(4 * A_KV, (2, 3, 6, 7))
    c_segs[5] = _ones_row(2, (1,))
    dts = [BF16, F32, BF16, F32, BF16, BF16, BF16, BF16, F32]
    q_a, kcvc, kvsw, g_a, q_b, kbvb, i_q, i_k, i_w = _proj(x2, g_pre, w_segs, c_segs, dts)
    r3 = lambda a: a.reshape(b, s, a.shape[-1])
    kvc = _nsa_compress(kcvc, cmp_pe, cmp_w, b, s)
    o_a = _nsa(r3(q_a), kvc, r3(kvsw), r3(g_a), rel_tab[:, :A_HEADS], b, s)
    o_b = _dsat(r3(q_b), r3(kbvb), r3(i_q), r3(i_k), r3(i_w), rel_tab[:, A_HEADS:A_HEADS + B_HEADS], b, s)
    na = A_HEADS * A_DH
    wa = _pad_rows(w_out[:na], A_HEADS, A_DH)
    return _out_proj(o_a.reshape(b * s, -1), o_b.reshape(b * s, -1), wa, w_out[na:], x2, g_post)


def _odd_mixer(x2, g_pre, g_post, w_in, w_out, q_norm, kv_norm, w_uq, w_ukv, lam, subln, rel_tab,
               lam_init, b, s):
    c_q, c_kv, k_rope, qd, kd, vd = _split(w_in, OD_SIZES)
    w_kr = jnp.concatenate([_pad_blocks(k_rope, 1, C_DR, at=C_DN),
                            _pad_blocks(_rot_cols(k_rope), 1, C_DR, at=C_DN)], axis=1)
    w_segs = [c_q, c_kv, w_kr, _pad_blocks(qd * (D_DH ** -0.5 * LOG2E), 2 * D_HEADS, D_DH),
              _pad_blocks(kd, 2 * D_HEADS, D_DH), _pad_blocks(vd, D_HEADS, D_DV, to=2 * LANES)]
    c_segs = [jnp.zeros((1, w.shape[1]), F32) for w in w_segs]
    c_segs[5] = _ones_row(D_HEADS, range(D_HEADS), to=2 * LANES, lane=D_DV)
    dts = [F32, F32, F32, BF16, BF16, BF16]
    cq, ckv, kr, q_d, k_d, v_d = _proj(x2, g_pre, w_segs, c_segs, dts)
    q_m, k_m, v_m = _mla_prep(cq, ckv, kr, q_norm, kv_norm, w_uq, w_ukv, s)
    r3 = lambda a: a.reshape(b, s, a.shape[-1])
    o_c = _mla_attn(r3(q_m), r3(k_m), r3(v_m), b, s)
    lam32 = lam.astype(F32)
    lam_full = jnp.exp(jnp.sum(lam32[0] * lam32[1])) - jnp.exp(jnp.sum(lam32[2] * lam32[3])) + lam_init
    o_d = _diff_attn(r3(q_d), r3(k_d), r3(v_d), rel_tab[:, A_HEADS + B_HEADS:], lam_full, subln,
                     lam_init, b, s)
    nc = C_HEADS * C_DV
    wc = _pad_rows(w_out[:nc], C_HEADS, C_DV)
    return _out_proj(o_c.reshape(b * s, -1), o_d.reshape(b * s, -1), wc, w_out[nc:], x2, g_post)


def kernel(x, mem, rel_bias_table, norm_pre, norm_post, mem_norm, ffn_wg, ffn_wu, ffn_wd, ev_w_in,
           ev_cmp_pe, ev_cmp_w, ev_w_out, od_w_in, od_q_norm, od_kv_norm, od_w_uq, od_w_ukv, od_lambda,
           od_subln, od_w_out, xa_wq, xa_wkv, xa_wo):
    b, s, d = x.shape
    depth = norm_pre.shape[0]
    x2 = x.reshape(b * s, d)
    mem2 = mem.reshape(-1, d)
    for i in range(depth):
        g_pre, g_post = norm_pre[i], norm_post[i]
        x2 = _ffn(x2, g_pre[0], g_post[0], ffn_wg[i, 0], ffn_wu[i, 0], ffn_wd[i, 0])
        if i % 2 == 0:
            e = i // 2
            x2 = _even_mixer(x2, g_pre[1], g_post[1], ev_w_in[e], ev_w_out[e], ev_cmp_pe[e], ev_cmp_w[e],
                             rel_bias_table, b, s)
        else:
            o = i // 2
            lam_init = 0.8 - 0.6 * math.exp(-0.3 * i)
            x2 = _odd_mixer(x2, g_pre[1], g_post[1], od_w_in[o], od_w_out[o], od_q_norm[o], od_kv_norm[o],
                            od_w_uq[o], od_w_ukv[o], od_lambda[o], od_subln[o], rel_bias_table, lam_init, b, s)
        (kv,) = _proj(mem2, mem_norm[i], [xa_wkv[i]], [jnp.zeros((1, xa_wkv.shape[-1]), F32)], [BF16])
        x3 = _mem_xattn(x2.reshape(b, s, d), kv.reshape(b, -1, kv.shape[-1]), g_pre[2], g_post[2],
                        xa_wq[i], xa_wo[i])
        x2 = _ffn(x3.reshape(b * s, d), g_pre[3], g_post[3], ffn_wg[i, 1], ffn_wu[i, 1], ffn_wd[i, 1])
    return x2.reshape(b, s, d)
```

```python
import functools
import math

import numpy as np
import jax
import jax.numpy as jnp
from jax import lax
from jax.experimental import pallas as pl
from jax.experimental.pallas import tpu as pltpu

F32, BF16, I32 = jnp.float32, jnp.bfloat16, jnp.int32

N_BUCKETS = 32
MAX_DISTANCE = 128
RMS_EPS = 1e-6
ROPE_THETA = 10000.0
NEG_INF = -1e30
A_HEADS, A_KV, A_DH = 8, 2, 64
A_GRP = A_HEADS // A_KV
CMP_LEN, CMP_STRIDE, SLC_LEN, SLC_TOP, WIN = 32, 16, 64, 16, 512
FORCE_BONUS = 1e3
B_HEADS, B_DH, IDX_HEADS, IDX_DH, IDX_TOPK_MAX = 8, 64, 8, 32, 256
C_HEADS, C_Q_RANK, C_KV_RANK, C_DN, C_DR, C_DV = 8, 256, 128, 64, 32, 64
D_HEADS, D_DH, D_DV = 4, 64, 128
X_HEADS, X_DH = 4, 128
EV_SIZES = (A_HEADS * A_DH, 6 * A_KV * A_DH, 3 * A_HEADS,
            B_HEADS * B_DH, B_DH, B_DH, IDX_HEADS * IDX_DH, IDX_DH, IDX_HEADS)
OD_SIZES = (C_Q_RANK, C_KV_RANK, C_DR, D_HEADS * 2 * D_DH, D_HEADS * 2 * D_DH, D_HEADS * D_DV)

LANES = 128
VMEM_LIMIT_BYTES = 56 * 1024 * 1024
ONES_LANE = 64
INT_MIN = -2 ** 31
LOG2E = 1.4426950408889634

TOK_TILE = 512
FFN_CHUNK = 256
LANE_TQ = 256
ACC_ROWS = 32
DENSE_TQ = 512
MLA_HEADS_PER_STEP = 4
DIFF_HEADS_PER_STEP = 2


def _params(*sem):
    return pltpu.CompilerParams(dimension_semantics=sem, vmem_limit_bytes=VMEM_LIMIT_BYTES)


def _const_spec(shape):
    nd = len(shape)
    return pl.BlockSpec(shape, lambda *_: (0,) * nd, pipeline_mode=pl.Buffered(1))


def _dot(a, b):
    return jnp.dot(a, b, preferred_element_type=F32)


def _dot_t(a, b):
    return lax.dot_general(a, b, (((1,), (1,)), ((), ())), preferred_element_type=F32)


def _rms(x, g):
    return x * lax.rsqrt(jnp.mean(x * x, axis=-1, keepdims=True) + RMS_EPS) * g


def _sigmoid(x):
    return 1.0 / (1.0 + jnp.exp(-x))


def _online_step(q, k, v, carry, add=None):
    m, acc = carry
    s = _dot_t(q, k)
    if add is not None:
        s = s + add
    m_new = jnp.maximum(m, jnp.max(s, axis=-1, keepdims=True))
    p = jnp.exp2(s - m_new)
    acc = jnp.exp2(m - m_new) * acc + _dot(p.astype(BF16), v)
    return m_new, acc


def _flash_init(m_rows, n_lanes):
    return jnp.full((m_rows, 1), -jnp.inf, F32), jnp.zeros((m_rows, n_lanes), F32)


def _ffn_body(x_ref, gpre_ref, gpost_ref, wgu_ref, wd_ref, o_ref, *, n_chunks, fc):
    x = x_ref[...]
    h = _rms(x, gpre_ref[...]).astype(BF16)
    acc = jnp.zeros(x.shape, F32)
    for c in range(n_chunks):
        gu = _dot(h, wgu_ref[:, c * 2 * fc:(c + 1) * 2 * fc])
        g, u = gu[:, :fc], gu[:, fc:]
        a = g * _sigmoid(g) * u
        acc = acc + _dot(a.astype(BF16), wd_ref[c * fc:(c + 1) * fc, :])
    o_ref[...] = x + 0.5 * _rms(acc, gpost_ref[...])


def _ffn(x, g_pre, g_post, wg, wu, wd):
    n, d = x.shape
    f = wg.shape[1]
    fc = FFN_CHUNK
    n_chunks = f // fc
    assert n_chunks * fc == f and n % TOK_TILE == 0
    wgu = jnp.concatenate([wg.reshape(d, n_chunks, fc), wu.reshape(d, n_chunks, fc)], axis=-1)
    wgu = wgu.reshape(d, 2 * f).astype(BF16)
    return pl.pallas_call(
        functools.partial(_ffn_body, n_chunks=n_chunks, fc=fc),
        grid=(n // TOK_TILE,),
        in_specs=[pl.BlockSpec((TOK_TILE, d), lambda i: (i, 0)),
                  _const_spec((1, d)), _const_spec((1, d)),
                  _const_spec((d, 2 * f)), _const_spec((f, d))],
        out_specs=pl.BlockSpec((TOK_TILE, d), lambda i: (i, 0)),
        out_shape=jax.ShapeDtypeStruct((n, d), F32),
        compiler_params=_params("arbitrary"),
        name="ffn",
    )(x, g_pre.reshape(1, d), g_post.reshape(1, d), wgu, wd.astype(BF16))


def _proj_body(x_ref, g_ref, w_ref, c_ref, *o_refs, bounds):
    h = _rms(x_ref[...], g_ref[...]).astype(BF16)
    for o_ref, (a, b) in zip(o_refs, bounds):
        o_ref[...] = (_dot(h, w_ref[:, a:b]) + c_ref[:, a:b]).astype(o_ref.dtype)


def _proj(x, g, w_segs, c_segs, dtypes):
    n, d = x.shape
    widths = [w.shape[1] for w in w_segs]
    assert all(wd_ % LANES == 0 for wd_ in widths) and n % TOK_TILE == 0
    offs = np.concatenate([[0], np.cumsum(widths)])
    bounds = tuple((int(offs[i]), int(offs[i + 1])) for i in range(len(widths)))
    w = jnp.concatenate(w_segs, axis=1).astype(BF16)
    c = jnp.concatenate(c_segs, axis=1).astype(F32)
    tot = int(offs[-1])
    return pl.pallas_call(
        functools.partial(_proj_body, bounds=bounds),
        grid=(n // TOK_TILE,),
        in_specs=[pl.BlockSpec((TOK_TILE, d), lambda i: (i, 0)),
                  _const_spec((1, d)), _const_spec((d, tot)), _const_spec((1, tot))],
        out_specs=[pl.BlockSpec((TOK_TILE, wd_), lambda i: (i, 0)) for wd_ in widths],
        out_shape=[jax.ShapeDtypeStruct((n, wd_), dt) for wd_, dt in zip(widths, dtypes)],
        compiler_params=_params("arbitrary"),
        name="norm_proj",
    )(x, g.reshape(1, d), w, c)


def _out_body(a_ref, b_ref, wa_ref, wb_ref, x_ref, g_ref, o_ref):
    y = _dot(a_ref[...], wa_ref[...]) + _dot(b_ref[...], wb_ref[...])
    o_ref[...] = x_ref[...] + _rms(y, g_ref[...])


def _out_proj(oa, ob, wa, wb, x, g_post):
    n, d = x.shape
    ka, kb = oa.shape[1], ob.shape[1]
    return pl.pallas_call(
        _out_body,
        grid=(n // TOK_TILE,),
        in_specs=[pl.BlockSpec((TOK_TILE, ka), lambda i: (i, 0)),
                  pl.BlockSpec((TOK_TILE, kb), lambda i: (i, 0)),
                  _const_spec((ka, d)), _const_spec((kb, d)),
                  pl.BlockSpec((TOK_TILE, d), lambda i: (i, 0)),
                  _const_spec((1, d))],
        out_specs=pl.BlockSpec((TOK_TILE, d), lambda i: (i, 0)),
        out_shape=jax.ShapeDtypeStruct((n, d), F32),
        compiler_params=_params("arbitrary"),
        name="out_proj",
    )(oa, ob, wa.astype(BF16), wb.astype(BF16), x, g_post.reshape(1, d))


def _xattn_body(x_ref, gpre_ref, gpost_ref, wq_ref, kv_ref, wo_ref, o_ref, *, heads, dh):
    x = x_ref[0]
    h = _rms(x, gpre_ref[...]).astype(BF16)
    q = _dot(h, wq_ref[...]).astype(BF16)
    kv = kv_ref[0]
    scale = dh ** -0.5
    outs = []
    for hh in range(heads):
        k = kv[:, hh * dh:(hh + 1) * dh]
        v = kv[:, (heads + hh) * dh:(heads + hh + 1) * dh]
        s = _dot_t(q[:, hh * dh:(hh + 1) * dh], k) * scale
        p = jnp.exp(s - jnp.max(s, axis=-1, keepdims=True))
        p = p / jnp.sum(p, axis=-1, keepdims=True)
        outs.append(_dot(p.astype(BF16), v).astype(BF16))
    y = _dot(jnp.concatenate(outs, axis=-1), wo_ref[...])
    o_ref[0] = x + _rms(y, gpost_ref[...])


def _mem_xattn(x, kv, g_pre, g_post, wq, wo):
    b, s, d = x.shape
    mlen, kvw = kv.shape[1:]
    hd = X_HEADS * X_DH
    return pl.pallas_call(
        functools.partial(_xattn_body, heads=X_HEADS, dh=X_DH),
        grid=(b, s // TOK_TILE),
        in_specs=[pl.BlockSpec((1, TOK_TILE, d), lambda i, j: (i, j, 0)),
                  _const_spec((1, d)), _const_spec((1, d)), _const_spec((d, hd)),
                  pl.BlockSpec((1, mlen, kvw), lambda i, j: (i, 0, 0)),
                  _const_spec((hd, d))],
        out_specs=pl.BlockSpec((1, TOK_TILE, d), lambda i, j: (i, j, 0)),
        out_shape=jax.ShapeDtypeStruct((b, s, d), F32),
        compiler_params=_params("arbitrary", "arbitrary"),
        name="mem_xattn",
    )(x, g_pre.reshape(1, d), g_post.reshape(1, d), wq.astype(BF16), kv, wo.astype(BF16))


def _bucket_np(dist):
    max_exact = N_BUCKETS // 2
    d = np.maximum(dist, 0)
    ratio = np.log(np.maximum(d, 1).astype(np.float32) / np.float32(max_exact)) / np.float32(
        math.log(MAX_DISTANCE / max_exact))
    large = np.minimum(max_exact + (ratio * np.float32(N_BUCKETS - max_exact)).astype(np.int32),
                       N_BUCKETS - 1)
    return np.where(d < max_exact, d, large).astype(np.int32)


def _rel_bias(tab, dist, visible, rel_to_far):
    dist = np.maximum(np.asarray(dist), 0).astype(np.int32)
    bk = _bucket_np(np.arange(max(int(dist.max()), 4 * MAX_DISTANCE) + 1))
    assert (np.diff(bk) >= 0).all() and bk[-1] == N_BUCKETS - 1
    tab = tab.astype(F32)
    col = lambda b: tab[b].reshape((-1,) + (1,) * dist.ndim)
    d = jnp.asarray(dist)[None]
    out = jnp.broadcast_to(col(0), (tab.shape[1],) + dist.shape)
    for b in range(1, N_BUCKETS):
        first = np.nonzero(bk == b)[0]
        if first.size:
            out = jnp.where(d >= int(first[0]), col(b), out)
    if rel_to_far:
        out = out - col(N_BUCKETS - 1)
    return jnp.where(jnp.asarray(np.asarray(visible))[None], out * LOG2E, NEG_INF)


def _far_distance():
    bk = _bucket_np(np.arange(4 * MAX_DISTANCE))
    return int(np.nonzero(bk == N_BUCKETS - 1)[0][0])


def _bias_tiles(tab, t):
    assert t + 1 >= _far_distance()
    r = np.arange(t)[:, None]
    c = np.arange(t)[None, :]
    dist = np.stack([r - c, t + r - c, 2 * t + r - c])
    everywhere = np.ones((t, t), bool)
    return _rel_bias(tab, dist, np.stack([r >= c, everywhere, everywhere]), True)


def _near_tiles(tab, t):
    tiles = _bias_tiles(tab, t)
    hidden = jnp.full_like(tiles[:, 0], NEG_INF)
    return jnp.stack([jnp.concatenate([tiles[:, 1], tiles[:, 0]], axis=-1),
                      jnp.concatenate([tiles[:, 0], hidden], axis=-1)], axis=1)


def _key_loops(qt, tq, wide, wide_step, near_step, init):
    far = jnp.maximum(qt - 1, 0) * tq
    tw = wide * tq
    n_wide = (far + tw - 1) // tw

    def body(j, carry):
        start = pl.multiple_of(jnp.maximum(far - (n_wide - j) * tw, 0), tq)
        return wide_step(start, jnp.where(j == 0, far - (n_wide - 1) * tw, tw), carry)

    carry = lax.fori_loop(0, n_wide, body, init)
    return near_step(pl.multiple_of(far, tq), jnp.where(qt == 0, 1, 0), carry)


def _cmp_body(x_ref, pe_ref, w_ref, o_ref):
    x = x_ref[0]
    xn = pltpu.roll(x, x.shape[0] - 1, 0)
    lo = (x + pe_ref[0:1, :]).astype(BF16)
    hi = (xn + pe_ref[1:2, :]).astype(BF16)
    o_ref[0] = (_dot(lo, w_ref[0]) + _dot(hi, w_ref[1])).astype(BF16)


def _nsa_compress(kcvc, cmp_pe, cmp_w, b, s):
    nb = s // CMP_STRIDE
    halves = CMP_LEN // CMP_STRIDE
    n_grp = 2 * A_KV
    kdim = CMP_STRIDE * n_grp * A_DH
    cw = cmp_w.reshape(2, halves, CMP_STRIDE, A_DH, A_DH)
    same = jnp.eye(n_grp, dtype=F32).reshape(2, A_KV, 2, A_KV)
    w = jnp.einsum("khlde,kgKG->hlkgdKGe", cw, same)
    w = jnp.pad(w, [(0, 0)] * 7 + [(0, LANES - A_DH)]).reshape(halves, kdim, n_grp * LANES).astype(BF16)
    pe = cmp_pe.reshape(2, halves, CMP_STRIDE, 1, A_DH).transpose(1, 2, 0, 3, 4)
    pe = jnp.broadcast_to(pe, (halves, CMP_STRIDE, 2, A_KV, A_DH)).reshape(halves, kdim)
    return pl.pallas_call(
        _cmp_body,
        grid=(b,),
        in_specs=[pl.BlockSpec((1, nb, kdim), lambda i: (i, 0, 0)),
                  _const_spec((halves, kdim)), _const_spec((halves, kdim, n_grp * LANES))],
        out_specs=pl.BlockSpec((1, nb, n_grp * LANES), lambda i: (i, 0, 0)),
        out_shape=jax.ShapeDtypeStruct((b, nb, n_grp * LANES), BF16),
        compiler_params=_params("arbitrary"),
        name="nsa_compress",
    )(kcvc.reshape(b, nb, kdim), pe, w)


def _nsat_body(qt_ref, kc_ref, vct_ref, bc_ref, ks_ref, vst_ref, kw_ref, vwt_ref, ga_ref, near_ref,
               ovl_ref, exp_ref, o_ref, mask_ref, *, tq, seq, n_sb):
    qi = pl.program_id(2)
    rr = A_GRP
    tw = 2 * tq
    q_all = jnp.concatenate([qt_ref[0, r * LANES:(r + 1) * LANES, :] for r in range(rr)], axis=1)
    head = lambda a, r: a[:, r * tq:(r + 1) * tq]
    heads = lambda a: jnp.concatenate([a] * rr, axis=1)

    s_c = _dot(kc_ref[0], q_all) + bc_ref[0, 0]
    p = jnp.where(s_c > 0.5 * NEG_INF, jnp.exp2(s_c - jnp.max(s_c, axis=0, keepdims=True)), 0.0)
    l = jnp.sum(p, axis=0, keepdims=True)
    pn = p * jnp.where(l > 0.0, 1.0 / l, 0.0)
    o_c = _dot(vct_ref[0, 0], pn.astype(BF16))

    imp = head(pn, 0)
    for r in range(1, rr):
        imp = imp + head(pn, r)
    imp_t = jnp.dot(ovl_ref[...], imp, precision=lax.Precision.HIGHEST, preferred_element_type=F32)
    blk = lax.broadcasted_iota(I32, (LANES, tq), 0)
    cur = (qi * tq + lax.broadcasted_iota(I32, (LANES, tq), 1)) // SLC_LEN
    forced = (blk == 0) | (blk == cur) | (blk == cur - 1)
    score = jnp.where(blk <= cur, imp_t + jnp.where(forced, FORCE_BONUS, 0.0), NEG_INF)
    groups = [score[8 * v:8 * v + 8] for v in range(-(-n_sb // 8))]
    ranks = [jnp.zeros((8, tq), F32) for _ in groups]
    sub = lax.broadcasted_iota(I32, (8, tq), 0)
    for i in range(n_sb):
        ci = score[i:i + 1, :]
        for v, sv in enumerate(groups):
            ge = jnp.where(ci >= sv, 1.0, 0.0)
            gt = jnp.where(ci > sv, 1.0, 0.0)
            if 8 * v > i:
                ranks[v] = ranks[v] + ge
            elif 8 * v + 7 < i:
                ranks[v] = ranks[v] + gt
            else:
                ranks[v] = ranks[v] + jnp.where(sub + 8 * v > i, ge, gt)
    rank = jnp.concatenate(ranks + [jnp.full((LANES - 8 * len(groups), tq), float(LANES), F32)], axis=0)
    sel = jnp.where(rank < float(min(SLC_TOP, n_sb)), jnp.where(blk <= cur, 1.0, 0.0), 0.0).astype(BF16)

    def expand(c, _):
        cs = pl.multiple_of(c * tw, tw)
        mask_ref[pl.ds(cs, tw), :] = (_dot(exp_ref[pl.ds(cs, tw), :], sel) - 1.0) * (-NEG_INF)
        return 0

    lax.fori_loop(0, qi // 2 + 1, expand, 0)

    def step(k_ref_, vt_ref_, ks, width, add, carry):
        m, acc = carry
        s = _dot(k_ref_[0, pl.ds(ks, width), :], q_all) + add
        m_new = jnp.maximum(m, jnp.max(s, axis=0, keepdims=True))
        p = jnp.exp2(s - m_new)
        return m_new, jnp.exp2(m - m_new) * acc + _dot(vt_ref_[0, 0, :, pl.ds(ks, width)], p.astype(BF16))

    init = jnp.full((1, rr * tq), -jnp.inf, F32), jnp.zeros((vst_ref.shape[2], rr * tq), F32)
    key_i = lax.broadcasted_iota(I32, (tw, tq), 0)

    def slc_wide(ks, limit, carry):
        own = jnp.where(key_i < limit, mask_ref[pl.ds(ks, tw), :], NEG_INF)
        return step(ks_ref, vst_ref, ks, tw, heads(own), carry)

    def slc_near(ks, which, carry):
        return step(ks_ref, vst_ref, ks, tw, heads(mask_ref[pl.ds(ks, tw), :]) + near_ref[0, which], carry)

    _, acc_s = _key_loops(qi, tq, 2, slc_wide, slc_near, init)

    def win_oldest():
        ks = pl.multiple_of((qi - WIN // tq) * tq, tq)
        kq = lax.broadcasted_iota(I32, (tq, tq), 0)
        qq = lax.broadcasted_iota(I32, (tq, tq), 1)
        return step(kw_ref, vwt_ref, ks, tq, heads(jnp.where(qq < kq, 0.0, NEG_INF)), init)

    carry_w = lax.cond(qi >= WIN // tq, win_oldest, lambda: init)
    far = pl.multiple_of(jnp.maximum(qi - 1, 0) * tq, tq)
    _, acc_w = step(kw_ref, vwt_ref, far, tw, near_ref[0, jnp.where(qi == 0, 1, 0)], carry_w)

    gs = _sigmoid(ga_ref[0, 0])
    for r in range(rr):
        a_s, a_w = head(acc_s, r), head(acc_w, r)
        o = (gs[3 * r:3 * r + 1] * head(o_c, r)[:A_DH]
             + gs[3 * r + 1:3 * r + 2] * (a_s[:A_DH] / a_s[A_DH:A_DH + 1])
             + gs[3 * r + 2:3 * r + 3] * (a_w[:A_DH] / a_w[A_DH:A_DH + 1]))
        o_ref[0, r * A_DH:(r + 1) * A_DH, :] = o.astype(BF16)


def _nsat(q, kvc, kvsw, ga, tab, b, s):
    tq = LANE_TQ
    nq = s // tq
    ncp = s // CMP_STRIDE
    n_sb = s // SLC_LEN
    n_vt = A_DH + 16
    assert n_sb <= LANES and WIN == 2 * tq and s % (2 * tq) == 0
    qt = jnp.swapaxes(q, 1, 2)
    vct = jnp.swapaxes(kvc.reshape(b, ncp, 2 * A_KV, LANES)[:, :, A_KV:, :n_vt], 1, 3).swapaxes(1, 2)
    vt = jnp.swapaxes(kvsw.reshape(b, s, 4 * A_KV, LANES)[..., :n_vt], 1, 3).swapaxes(1, 2)
    gat = jnp.swapaxes(ga.reshape(b, s, A_KV, LANES)[..., :16], 1, 3).swapaxes(1, 2)
    t = np.arange(s)[:, None]
    cmp_end = np.arange(ncp)[None, :] * CMP_STRIDE + CMP_LEN - 1
    bias_c = _rel_bias(tab, t - cmp_end, cmp_end <= t, False)
    bias_c = bias_c.reshape(A_KV, A_GRP, nq, tq, ncp).transpose(0, 2, 4, 1, 3).reshape(A_KV, nq, ncp, A_GRP * tq)
    near = _near_tiles(tab, tq).reshape(A_KV, A_GRP, 2, tq, 2 * tq).transpose(0, 2, 4, 1, 3)
    near = near.reshape(A_KV, 2, 2 * tq, A_GRP * tq)
    cmp_start = np.arange(ncp)[None, :] * CMP_STRIDE
    sb_start = np.arange(LANES)[:, None] * SLC_LEN
    ovl = ((cmp_start < sb_start + SLC_LEN) & (cmp_start + CMP_LEN - 1 >= sb_start)
           & (np.arange(ncp)[None, :] < ncp - 1) & (np.arange(LANES)[:, None] < n_sb))
    ovl = jnp.asarray(ovl.astype(np.float32))
    expand = jnp.asarray((np.arange(s)[:, None] // SLC_LEN == np.arange(LANES)[None, :]).astype(np.float32), BF16)
    k_spec = lambda blk: pl.BlockSpec((1, s, LANES), lambda i, g, j: (i, 0, 2 * blk + g))
    vt_spec = lambda blk: pl.BlockSpec((1, 1, n_vt, s), lambda i, g, j: (i, 2 * blk + g, 0, 0))
    o_t = pl.pallas_call(
        functools.partial(_nsat_body, tq=tq, seq=s, n_sb=n_sb),
        grid=(b, A_KV, nq),
        in_specs=[pl.BlockSpec((1, A_GRP * LANES, tq), lambda i, g, j: (i, g, j)),
                  pl.BlockSpec((1, ncp, LANES), lambda i, g, j: (i, 0, g)),
                  pl.BlockSpec((1, 1, n_vt, ncp), lambda i, g, j: (i, g, 0, 0)),
                  pl.BlockSpec((1, 1, ncp, A_GRP * tq), lambda i, g, j: (g, j, 0, 0)),
                  k_spec(0), vt_spec(1), k_spec(2), vt_spec(3),
                  pl.BlockSpec((1, 1, 16, tq), lambda i, g, j: (i, g, 0, j)),
                  pl.BlockSpec((1, 2, 2 * tq, A_GRP * tq), lambda i, g, j: (g, 0, 0, 0)),
                  _const_spec((LANES, ncp)), _const_spec((s, LANES))],
        out_specs=pl.BlockSpec((1, A_GRP * A_DH, tq), lambda i, g, j: (i, g, j)),
        out_shape=jax.ShapeDtypeStruct((b, A_HEADS * A_DH, s), BF16),
        scratch_shapes=[pltpu.VMEM((s, tq), F32)],
        compiler_params=_params("arbitrary", "arbitrary", "arbitrary"),
        name="nsa_attn",
    )(qt, kvc, vct, bias_c, kvsw, vt, kvsw, vt, gat, near, ovl, expand)
    return jnp.swapaxes(o_t, 1, 2)


def _dsat_body(qt_ref, k_ref, vt_ref, iqt_ref, ik_ref, iwt_ref, bias_ref, o_ref, keys_ref, *,
               tq, seq, k_top):
    qi = pl.program_id(1)
    nh = B_HEADS
    tw = 2 * tq
    n_wide = qi // 2 + 1
    key_i = lax.broadcasted_iota(I32, (tw, tq), 0)
    qry_i = lax.broadcasted_iota(I32, (tw, tq), 1)

    iwt = iwt_ref[0]

    def scores(kw, _):
        ks = pl.multiple_of(kw * tw, tw)
        ik = ik_ref[0, pl.ds(ks, tw), :]
        sc = jnp.zeros((tw, tq), F32)
        for h in range(IDX_HEADS):
            lg = _dot(ik, iqt_ref[0, h * LANES:(h + 1) * LANES, :])
            sc = sc + jnp.maximum(lg, 0.0) * iwt[h:h + 1, :]
        bits = lax.bitcast_convert_type(sc, I32)
        key = jnp.where(bits < 0, bits ^ 0x7FFFFFFF, bits)
        vis = (key_i + ks) <= (qry_i + qi * tq)
        keys_ref[pl.ds(ks, tw), :] = jnp.where(vis, key, INT_MIN)
        return 0

    lax.fori_loop(0, n_wide, scores, 0)

    def count(ind):
        def body(kw, acc):
            ks = pl.multiple_of(kw * tw, tw)
            x = ind(keys_ref[pl.ds(ks, tw), :], kw)
            return acc + jnp.sum(x.reshape(tw // ACC_ROWS, ACC_ROWS, tq), axis=0)
        acc = lax.fori_loop(0, n_wide, body, jnp.zeros((ACC_ROWS, tq), F32))
        return jnp.sum(acc, axis=0, keepdims=True)

    kf = float(k_top)
    c0 = count(lambda t, kw: jnp.where(t >= 0, 1.0, 0.0))
    thr = jnp.where(c0 >= kf, 0, INT_MIN).astype(I32)
    c_ge = jnp.where(c0 >= kf, c0, float(seq))

    def bisect(i, state):
        thr, c_ge = state
        cand = thr | (jnp.int32(1) << (30 - i))
        c = count(lambda t, kw: jnp.where(t >= cand, 1.0, 0.0))
        return jnp.where(c >= kf, cand, thr), jnp.where(c >= kf, c, c_ge)

    thr, c_ge = lax.fori_loop(0, 31, bisect, (thr, c_ge))

    n_bits = int(math.log2(seq))
    assert 2 ** n_bits == seq

    def cutoff():
        need = kf - count(lambda t, kw: jnp.where(t > thr, 1.0, 0.0))

        def step(i, cut):
            cand = cut | (jnp.int32(1) << (n_bits - 1 - i))
            c = count(lambda t, kw: jnp.where(t == thr, jnp.where(key_i + kw * tw < cand, 1.0, 0.0), 0.0))
            return jnp.where(c < need, cand, cut)
        return lax.fori_loop(0, n_bits, step, jnp.zeros((1, tq), I32))

    excess = jnp.max(jnp.where(c_ge > kf, 1.0, 0.0)) > 0.5
    cut = lax.cond(excess, cutoff, lambda: jnp.full((1, tq), seq, I32))

    q_all = jnp.concatenate([qt_ref[0, h * LANES:(h + 1) * LANES, :] for h in range(nh)], axis=1)

    def step(ks, own, bias, carry):
        m, acc = carry
        key = keys_ref[pl.ds(ks, tw), :]
        tie = jnp.where(key_i + ks <= cut, 0.0, NEG_INF)
        sel = jnp.where(key > thr, 0.0, jnp.where(key == thr, tie, NEG_INF))
        if own is not None:
            sel = jnp.where(own, sel, NEG_INF)
        s = _dot(k_ref[0, pl.ds(ks, tw), :], q_all) + jnp.concatenate([sel] * nh, axis=1)
        if bias is not None:
            s = s + bias
        m_new = jnp.maximum(m, jnp.max(s, axis=0, keepdims=True))
        p = jnp.exp2(s - m_new)
        return m_new, jnp.exp2(m - m_new) * acc + _dot(vt_ref[0, :, pl.ds(ks, tw)], p.astype(BF16))

    def wide_step(ks, limit, carry):
        return step(ks, key_i < limit, None, carry)

    def near_step(ks, which, carry):
        return step(ks, None, bias_ref[which], carry)

    init = jnp.full((1, nh * tq), -jnp.inf, F32), jnp.zeros((vt_ref.shape[1], nh * tq), F32)
    _, acc = _key_loops(qi, tq, 2, wide_step, near_step, init)
    for h in range(nh):
        a = acc[:, h * tq:(h + 1) * tq]
        o_ref[0, h * B_DH:(h + 1) * B_DH, :] = (a[:B_DH] / a[B_DH:B_DH + 1]).astype(BF16)


def _dsat(q, kbvb, iq, ik, iw, tab, b, s):
    tq = LANE_TQ
    k_top = min(IDX_TOPK_MAX, s // 4)
    assert s % (2 * tq) == 0
    qt = jnp.swapaxes(q, 1, 2)
    iqt = jnp.swapaxes(iq, 1, 2)
    iwt = jnp.swapaxes(iw[:, :, :IDX_HEADS], 1, 2)
    n_vt = B_DH + 16
    vt = jnp.swapaxes(kbvb[:, :, LANES:LANES + n_vt], 1, 2)
    near = _near_tiles(tab, tq).transpose(1, 3, 0, 2).reshape(2, 2 * tq, B_HEADS * tq)
    o_t = pl.pallas_call(
        functools.partial(_dsat_body, tq=tq, seq=s, k_top=k_top),
        grid=(b, s // tq),
        in_specs=[pl.BlockSpec((1, B_HEADS * LANES, tq), lambda i, j: (i, 0, j)),
                  pl.BlockSpec((1, s, LANES), lambda i, j: (i, 0, 0)),
                  pl.BlockSpec((1, n_vt, s), lambda i, j: (i, 0, 0)),
                  pl.BlockSpec((1, IDX_HEADS * LANES, tq), lambda i, j: (i, 0, j)),
                  pl.BlockSpec((1, s, LANES), lambda i, j: (i, 0, 0)),
                  pl.BlockSpec((1, IDX_HEADS, tq), lambda i, j: (i, 0, j)),
                  _const_spec((2, 2 * tq, B_HEADS * tq))],
        out_specs=pl.BlockSpec((1, B_HEADS * B_DH, tq), lambda i, j: (i, 0, j)),
        out_shape=jax.ShapeDtypeStruct((b, B_HEADS * B_DH, s), BF16),
        scratch_shapes=[pltpu.VMEM((s, tq), I32)],
        compiler_params=_params("arbitrary", "arbitrary"),
        name="dsa_attn",
    )(qt, kbvb, vt, iqt, ik, iwt, near)
    return jnp.swapaxes(o_t, 1, 2)


def _mla_prep_body(cq_ref, ckv_ref, kr_ref, qn_ref, kvn_ref, wq1_ref, wq2_ref, wk_ref, wv_ref,
                   cos_ref, sin_ref, ones_ref, q_o, k_o, v_o):
    cqn = _rms(cq_ref[...], qn_ref[...]).astype(BF16)
    ckn = _rms(ckv_ref[...], kvn_ref[...]).astype(BF16)
    cos, sin = cos_ref[...], sin_ref[...]
    cos_h = jnp.concatenate([cos] * C_HEADS, axis=1)
    sin_h = jnp.concatenate([sin] * C_HEADS, axis=1)
    q_o[...] = (_dot(cqn, wq1_ref[...]) * cos_h + _dot(cqn, wq2_ref[...]) * sin_h).astype(BF16)
    kr = kr_ref[...]
    kr_rot = kr[:, :LANES] * cos + kr[:, LANES:] * sin
    k_o[...] = (_dot(ckn, wk_ref[...]) + jnp.concatenate([kr_rot] * C_HEADS, axis=1)).astype(BF16)
    v_o[...] = (_dot(ckn, wv_ref[...]) + ones_ref[...]).astype(BF16)


def _rot_cols(w):
    half = C_DR // 2
    return jnp.concatenate([-w[..., half:], w[..., :half]], axis=-1)


def _mla_prep(cq, ckv, kr, q_norm, kv_norm, w_uq, w_ukv, s):
    n = cq.shape[0]
    hw = C_HEADS * LANES
    padq = LANES - C_DN - C_DR
    wq = w_uq.reshape(C_Q_RANK, C_HEADS, C_DN + C_DR) * ((C_DN + C_DR) ** -0.5 * LOG2E)
    wq1 = jnp.pad(wq, ((0, 0), (0, 0), (0, padq))).reshape(C_Q_RANK, hw)
    wq2 = jnp.pad(_rot_cols(wq[..., C_DN:]), ((0, 0), (0, 0), (C_DN, padq))).reshape(C_Q_RANK, hw)
    wkv = w_ukv.reshape(C_KV_RANK, C_HEADS, C_DN + C_DV)
    wk = jnp.pad(wkv[..., :C_DN], ((0, 0), (0, 0), (0, LANES - C_DN))).reshape(C_KV_RANK, hw)
    wv = jnp.pad(wkv[..., C_DN:], ((0, 0), (0, 0), (0, LANES - C_DV))).reshape(C_KV_RANK, hw)
    half = C_DR // 2
    inv = ROPE_THETA ** (-jnp.arange(half, dtype=F32) / half)
    ang = jnp.arange(s, dtype=F32)[:, None] * inv[None, :]
    cos, sin = jnp.cos(ang), jnp.sin(ang)
    cos_t = jnp.concatenate([jnp.ones((s, C_DN), F32), cos, cos, jnp.zeros((s, padq), F32)], axis=1)
    sin_t = jnp.concatenate([jnp.zeros((s, C_DN), F32), sin, sin, jnp.zeros((s, padq), F32)], axis=1)
    ones = jnp.tile(jnp.asarray(np.arange(LANES) == ONES_LANE, F32), C_HEADS).reshape(1, hw)
    tile_tok = lambda w_: pl.BlockSpec((TOK_TILE, w_), lambda i: (i, 0))
    n_pos = s // TOK_TILE
    pos_spec = pl.BlockSpec((TOK_TILE, LANES), lambda i: (i % n_pos, 0))
    return pl.pallas_call(
        _mla_prep_body,
        grid=(n // TOK_TILE,),
        in_specs=[tile_tok(C_Q_RANK), tile_tok(C_KV_RANK), tile_tok(2 * LANES),
                  _const_spec((1, C_Q_RANK)), _const_spec((1, C_KV_RANK)),
                  _const_spec((C_Q_RANK, hw)), _const_spec((C_Q_RANK, hw)),
                  _const_spec((C_KV_RANK, hw)), _const_spec((C_KV_RANK, hw)),
                  pos_spec, pos_spec, _const_spec((1, hw))],
        out_specs=[tile_tok(hw)] * 3,
        out_shape=[jax.ShapeDtypeStruct((n, hw), BF16)] * 3,
        compiler_params=_params("arbitrary"),
        name="mla_prep",
    )(cq, ckv, kr, q_norm.reshape(1, -1), kv_norm.reshape(1, -1), wq1.astype(BF16), wq2.astype(BF16),
      wk.astype(BF16), wv.astype(BF16), cos_t, sin_t, ones)


def _mla_attn_body(q_ref, k_ref, v_ref, o_ref, *, tq, nh):
    qt = pl.program_id(2)
    q = q_ref[0]
    lanes = lambda a, h: a[:, h * LANES:(h + 1) * LANES]

    def step(kt, carry, add):
        ks = pl.multiple_of(kt * tq, tq)
        k = k_ref[0, pl.ds(ks, tq), :]
        v = v_ref[0, pl.ds(ks, tq), :]
        return tuple(_online_step(lanes(q, h), lanes(k, h), lanes(v, h), carry[h], add=add)
                     for h in range(nh))

    init = tuple(_flash_init(tq, LANES) for _ in range(nh))
    carry = lax.fori_loop(0, qt, lambda kt, c: step(kt, c, None), init)
    row = lax.broadcasted_iota(I32, (tq, tq), 0)
    col = lax.broadcasted_iota(I32, (tq, tq), 1)
    carry = step(qt, carry, jnp.where(row >= col, 0.0, NEG_INF))
    for h in range(nh):
        acc = carry[h][1]
        o_ref[0, :, h * LANES:(h + 1) * LANES] = (acc / acc[:, ONES_LANE:ONES_LANE + 1]).astype(BF16)


def _mla_attn(q, k, v, b, s):
    tq = DENSE_TQ
    nh = MLA_HEADS_PER_STEP
    spec_q = pl.BlockSpec((1, tq, nh * LANES), lambda i, h, j: (i, j, h))
    spec_kv = pl.BlockSpec((1, s, nh * LANES), lambda i, h, j: (i, 0, h))
    return pl.pallas_call(
        functools.partial(_mla_attn_body, tq=tq, nh=nh),
        grid=(b, C_HEADS // nh, s // tq),
        in_specs=[spec_q, spec_kv, spec_kv],
        out_specs=spec_q,
        out_shape=jax.ShapeDtypeStruct((b, s, C_HEADS * LANES), BF16),
        compiler_params=_params("arbitrary", "arbitrary", "arbitrary"),
        name="mla_attn",
    )(q, k, v)


def _diff_body(lam_ref, q_ref, k_ref, v_ref, bias_ref, g_ref, o_ref, *, tq, nh, out_scale):
    qt = pl.program_id(2)
    q = q_ref[0]
    lanes = lambda a, c, w=LANES: a[:, c * w:(c + 1) * w]

    def step(kt, carry, near):
        ks = pl.multiple_of(kt * tq, tq)
        k = k_ref[0, pl.ds(ks, tq), :]
        v = v_ref[0, pl.ds(ks, tq), :]
        return tuple(_online_step(lanes(q, c), lanes(k, c), lanes(v, c // 2, 2 * LANES), carry[c],
                                  add=bias_ref[c // 2, qt - kt] if near else None)
                     for c in range(2 * nh))

    n_far = jnp.maximum(qt - 1, 0)
    init = tuple(_flash_init(tq, 2 * LANES) for _ in range(2 * nh))
    carry = lax.fori_loop(0, n_far, functools.partial(step, near=False), init)
    carry = lax.fori_loop(n_far, qt + 1, functools.partial(step, near=True), carry)
    outs = [carry[c][1][:, :D_DV] / carry[c][1][:, D_DV:D_DV + 1] for c in range(2 * nh)]
    for h in range(nh):
        o = outs[2 * h] - lam_ref[0] * outs[2 * h + 1]
        o_ref[0, :, h * D_DV:(h + 1) * D_DV] = (_rms(o, g_ref[...]) * out_scale).astype(BF16)


def _diff_attn(q, k, v, tab, lam_full, subln, lam_init, b, s):
    tq = DENSE_TQ
    nh = DIFF_HEADS_PER_STEP
    tiles = _bias_tiles(tab, tq)[:, :2]
    spec_q = pl.BlockSpec((1, tq, nh * 2 * LANES), lambda i, h, j: (i, j, h))
    spec_kv = pl.BlockSpec((1, s, nh * 2 * LANES), lambda i, h, j: (i, 0, h))
    return pl.pallas_call(
        functools.partial(_diff_body, tq=tq, nh=nh, out_scale=1.0 - lam_init),
        grid=(b, D_HEADS // nh, s // tq),
        in_specs=[pl.BlockSpec(memory_space=pltpu.SMEM),
                  spec_q, spec_kv, spec_kv,
                  pl.BlockSpec((nh, 2, tq, tq), lambda i, h, j: (h, 0, 0, 0)),
                  _const_spec((1, D_DV))],
        out_specs=pl.BlockSpec((1, tq, nh * D_DV), lambda i, h, j: (i, j, h)),
        out_shape=jax.ShapeDtypeStruct((b, s, D_HEADS * D_DV), BF16),
        compiler_params=_params("arbitrary", "arbitrary", "arbitrary"),
        name="diff_attn",
    )(lam_full.reshape(1), q, k, v, tiles, subln.reshape(1, D_DV))


def _pad_blocks(w, n, width, to=LANES, at=0):
    k = w.shape[0]
    return jnp.pad(w.reshape(k, n, width), ((0, 0), (0, 0), (at, to - width - at))).reshape(k, n * to)


def _ones_row(n_blocks, which, to=LANES, lane=ONES_LANE):
    row = np.zeros((n_blocks, to), np.float32)
    row[list(which), lane] = 1.0
    return jnp.asarray(row.reshape(1, n_blocks * to))


def _split(w, sizes):
    return jnp.split(w, [int(c) for c in np.cumsum(sizes)[:-1]], axis=-1)


def _pad_rows(w, n, width):
    d = w.shape[1]
    return jnp.pad(w.reshape(n, width, d), ((0, 0), (0, LANES - width), (0, 0))).reshape(n * LANES, d)


def _even_mixer(x2, g_pre, g_post, w_in, w_out, cmp_pe, cmp_w, rel_tab, b, s):
    qa, kva, ga, qb, kb, vb, iq, ik, iw = _split(w_in, EV_SIZES)
    half = 2 * A_KV * A_DH
    w_segs = [_pad_blocks(qa * (A_DH ** -0.5 * LOG2E), A_HEADS, A_DH), kva[:, :half],
              _pad_blocks(kva[:, half:], 4 * A_KV, A_DH), _pad_blocks(ga, A_KV, 3 * A_GRP),
              _pad_blocks(qb * (B_DH ** -0.5 * LOG2E), B_HEADS, B_DH),
              _pad_blocks(jnp.concatenate([kb, vb], axis=1), 2, B_DH),
              _pad_blocks(iq, IDX_HEADS, IDX_DH), _pad_blocks(ik, 1, IDX_DH), _pad_blocks(iw, 1, IDX_HEADS)]
    zeros = lambda w: jnp.zeros((1, w.shape[1]), F32)
    c_segs = [zeros(w) for w in w_segs]
    c_segs[2] = _ones_row(4 * A_KV, (2, 3, 6, 7))
    c_segs[5] = _ones_row(2, (1,))
    dts = [BF16, F32, BF16, F32, BF16, BF16, BF16, BF16, F32]
    q_a, kcvc, kvsw, g_a, q_b, kbvb, i_q, i_k, i_w = _proj(x2, g_pre, w_segs, c_segs, dts)
    r3 = lambda a: a.reshape(b, s, a.shape[-1])
    kvc = _nsa_compress(kcvc, cmp_pe, cmp_w, b, s)
    o_a = _nsat(r3(q_a), kvc, r3(kvsw), r3(g_a), rel_tab[:, :A_HEADS], b, s)
    o_b = _dsat(r3(q_b), r3(kbvb), r3(i_q), r3(i_k), r3(i_w), rel_tab[:, A_HEADS:A_HEADS + B_HEADS], b, s)
    na = A_HEADS * A_DH
    return _out_proj(o_a.reshape(b * s, -1), o_b.reshape(b * s, -1), w_out[:na], w_out[na:], x2, g_post)


def _odd_mixer(x2, g_pre, g_post, w_in, w_out, q_norm, kv_norm, w_uq, w_ukv, lam, subln, rel_tab,
               lam_init, b, s):
    c_q, c_kv, k_rope, qd, kd, vd = _split(w_in, OD_SIZES)
    w_kr = jnp.concatenate([_pad_blocks(k_rope, 1, C_DR, at=C_DN),
                            _pad_blocks(_rot_cols(k_rope), 1, C_DR, at=C_DN)], axis=1)
    w_segs = [c_q, c_kv, w_kr, _pad_blocks(qd * (D_DH ** -0.5 * LOG2E), 2 * D_HEADS, D_DH),
              _pad_blocks(kd, 2 * D_HEADS, D_DH), _pad_blocks(vd, D_HEADS, D_DV, to=2 * LANES)]
    c_segs = [jnp.zeros((1, w.shape[1]), F32) for w in w_segs]
    c_segs[5] = _ones_row(D_HEADS, range(D_HEADS), to=2 * LANES, lane=D_DV)
    dts = [F32, F32, F32, BF16, BF16, BF16]
    cq, ckv, kr, q_d, k_d, v_d = _proj(x2, g_pre, w_segs, c_segs, dts)
    q_m, k_m, v_m = _mla_prep(cq, ckv, kr, q_norm, kv_norm, w_uq, w_ukv, s)
    r3 = lambda a: a.reshape(b, s, a.shape[-1])
    o_c = _mla_attn(r3(q_m), r3(k_m), r3(v_m), b, s)
    lam32 = lam.astype(F32)
    lam_full = jnp.exp(jnp.sum(lam32[0] * lam32[1])) - jnp.exp(jnp.sum(lam32[2] * lam32[3])) + lam_init
    o_d = _diff_attn(r3(q_d), r3(k_d), r3(v_d), rel_tab[:, A_HEADS + B_HEADS:], lam_full, subln,
                     lam_init, b, s)
    nc = C_HEADS * C_DV
    wc = _pad_rows(w_out[:nc], C_HEADS, C_DV)
    return _out_proj(o_c.reshape(b * s, -1), o_d.reshape(b * s, -1), wc, w_out[nc:], x2, g_post)


def kernel(x, mem, rel_bias_table, norm_pre, norm_post, mem_norm, ffn_wg, ffn_wu, ffn_wd, ev_w_in,
           ev_cmp_pe, ev_cmp_w, ev_w_out, od_w_in, od_q_norm, od_kv_norm, od_w_uq, od_w_ukv, od_lambda,
           od_subln, od_w_out, xa_wq, xa_wkv, xa_wo):
    b, s, d = x.shape
    depth = norm_pre.shape[0]
    x2 = x.reshape(b * s, d)
    mem2 = mem.reshape(-1, d)
    for i in range(depth):
        g_pre, g_post = norm_pre[i], norm_post[i]
        x2 = _ffn(x2, g_pre[0], g_post[0], ffn_wg[i, 0], ffn_wu[i, 0], ffn_wd[i, 0])
        if i % 2 == 0:
            e = i // 2
            x2 = _even_mixer(x2, g_pre[1], g_post[1], ev_w_in[e], ev_w_out[e], ev_cmp_pe[e], ev_cmp_w[e],
                             rel_bias_table, b, s)
        else:
            o = i // 2
            lam_init = 0.8 - 0.6 * math.exp(-0.3 * i)
            x2 = _odd_mixer(x2, g_pre[1], g_post[1], od_w_in[o], od_w_out[o], od_q_norm[o], od_kv_norm[o],
                            od_w_uq[o], od_w_ukv[o], od_lambda[o], od_subln[o], rel_bias_table, lam_init, b, s)
        (kv,) = _proj(mem2, mem_norm[i], [xa_wkv[i]], [jnp.zeros((1, xa_wkv.shape[-1]), F32)], [BF16])
        x3 = _mem_xattn(x2.reshape(b, s, d), kv.reshape(b, -1, kv.shape[-1]), g_pre[2], g_post[2],
                        xa_wq[i], xa_wo[i])
        x2 = _ffn(x3.reshape(b * s, d), g_pre[3], g_post[3], ffn_wg[i, 1], ffn_wu[i, 1], ffn_wd[i, 1])
    return x2.reshape(b, s, d)
```

```python
import functools
import math

import numpy as np
import jax
import jax.numpy as jnp
from jax import lax
from jax.experimental import pallas as pl
from jax.experimental.pallas import tpu as pltpu

F32, BF16, I32 = jnp.float32, jnp.bfloat16, jnp.int32

N_BUCKETS = 32
MAX_DISTANCE = 128
RMS_EPS = 1e-6
ROPE_THETA = 10000.0
NEG_INF = -1e30
A_HEADS, A_KV, A_DH = 8, 2, 64
A_GRP = A_HEADS // A_KV
CMP_LEN, CMP_STRIDE, SLC_LEN, SLC_TOP, WIN = 32, 16, 64, 16, 512
FORCE_BONUS = 1e3
B_HEADS, B_DH, IDX_HEADS, IDX_DH, IDX_TOPK_MAX = 8, 64, 8, 32, 256
C_HEADS, C_Q_RANK, C_KV_RANK, C_DN, C_DR, C_DV = 8, 256, 128, 64, 32, 64
D_HEADS, D_DH, D_DV = 4, 64, 128
X_HEADS, X_DH = 4, 128
EV_SIZES = (A_HEADS * A_DH, 6 * A_KV * A_DH, 3 * A_HEADS,
            B_HEADS * B_DH, B_DH, B_DH, IDX_HEADS * IDX_DH, IDX_DH, IDX_HEADS)
OD_SIZES = (C_Q_RANK, C_KV_RANK, C_DR, D_HEADS * 2 * D_DH, D_HEADS * 2 * D_DH, D_HEADS * D_DV)

LANES = 128
VMEM_LIMIT_BYTES = 56 * 1024 * 1024
ONES_LANE = 64
INT_MIN = -2 ** 31
LOG2E = 1.4426950408889634

TOK_TILE = 512
FFN_CHUNK = 256
LANE_TQ = 256
ACC_ROWS = 32
DENSE_TQ = 512
MLA_HEADS_PER_STEP = 4
DIFF_HEADS_PER_STEP = 2


def _params(*sem):
    return pltpu.CompilerParams(dimension_semantics=sem, vmem_limit_bytes=VMEM_LIMIT_BYTES)


def _const_spec(shape):
    nd = len(shape)
    return pl.BlockSpec(shape, lambda *_: (0,) * nd, pipeline_mode=pl.Buffered(1))


def _dot(a, b):
    return jnp.dot(a, b, preferred_element_type=F32)


def _dot_t(a, b):
    return lax.dot_general(a, b, (((1,), (1,)), ((), ())), preferred_element_type=F32)


def _rms(x, g):
    return x * lax.rsqrt(jnp.mean(x * x, axis=-1, keepdims=True) + RMS_EPS) * g


def _sigmoid(x):
    return 1.0 / (1.0 + jnp.exp(-x))


def _online_step(q, k, v, carry, add=None):
    m, acc = carry
    s = _dot_t(q, k)
    if add is not None:
        s = s + add
    m_new = jnp.maximum(m, jnp.max(s, axis=-1, keepdims=True))
    p = jnp.exp2(s - m_new)
    acc = jnp.exp2(m - m_new) * acc + _dot(p.astype(BF16), v)
    return m_new, acc


def _flash_init(m_rows, n_lanes):
    return jnp.full((m_rows, 1), -jnp.inf, F32), jnp.zeros((m_rows, n_lanes), F32)


def _ffn_body(x_ref, gpre_ref, gpost_ref, wgu_ref, wd_ref, o_ref, *, n_chunks, fc):
    x = x_ref[...]
    h = _rms(x, gpre_ref[...]).astype(BF16)
    acc = jnp.zeros(x.shape, F32)
    for c in range(n_chunks):
        gu = _dot(h, wgu_ref[:, c * 2 * fc:(c + 1) * 2 * fc])
        g, u = gu[:, :fc], gu[:, fc:]
        a = g * _sigmoid(g) * u
        acc = acc + _dot(a.astype(BF16), wd_ref[c * fc:(c + 1) * fc, :])
    o_ref[...] = x + 0.5 * _rms(acc, gpost_ref[...])


def _ffn(x, g_pre, g_post, wg, wu, wd):
    n, d = x.shape
    f = wg.shape[1]
    fc = FFN_CHUNK
    n_chunks = f // fc
    assert n_chunks * fc == f and n % TOK_TILE == 0
    wgu = jnp.concatenate([wg.reshape(d, n_chunks, fc), wu.reshape(d, n_chunks, fc)], axis=-1)
    wgu = wgu.reshape(d, 2 * f).astype(BF16)
    return pl.pallas_call(
        functools.partial(_ffn_body, n_chunks=n_chunks, fc=fc),
        grid=(n // TOK_TILE,),
        in_specs=[pl.BlockSpec((TOK_TILE, d), lambda i: (i, 0)),
                  _const_spec((1, d)), _const_spec((1, d)),
                  _const_spec((d, 2 * f)), _const_spec((f, d))],
        out_specs=pl.BlockSpec((TOK_TILE, d), lambda i: (i, 0)),
        out_shape=jax.ShapeDtypeStruct((n, d), F32),
        compiler_params=_params("arbitrary"),
        name="ffn",
    )(x, g_pre.reshape(1, d), g_post.reshape(1, d), wgu, wd.astype(BF16))


def _proj_body(x_ref, g_ref, w_ref, c_ref, *o_refs, bounds):
    h = _rms(x_ref[...], g_ref[...]).astype(BF16)
    for o_ref, (a, b) in zip(o_refs, bounds):
        o_ref[...] = (_dot(h, w_ref[:, a:b]) + c_ref[:, a:b]).astype(o_ref.dtype)


def _proj(x, g, w_segs, c_segs, dtypes):
    n, d = x.shape
    widths = [w.shape[1] for w in w_segs]
    assert all(wd_ % LANES == 0 for wd_ in widths) and n % TOK_TILE == 0
    offs = np.concatenate([[0], np.cumsum(widths)])
    bounds = tuple((int(offs[i]), int(offs[i + 1])) for i in range(len(widths)))
    w = jnp.concatenate(w_segs, axis=1).astype(BF16)
    c = jnp.concatenate(c_segs, axis=1).astype(F32)
    tot = int(offs[-1])
    return pl.pallas_call(
        functools.partial(_proj_body, bounds=bounds),
        grid=(n // TOK_TILE,),
        in_specs=[pl.BlockSpec((TOK_TILE, d), lambda i: (i, 0)),
                  _const_spec((1, d)), _const_spec((d, tot)), _const_spec((1, tot))],
        out_specs=[pl.BlockSpec((TOK_TILE, wd_), lambda i: (i, 0)) for wd_ in widths],
        out_shape=[jax.ShapeDtypeStruct((n, wd_), dt) for wd_, dt in zip(widths, dtypes)],
        compiler_params=_params("arbitrary"),
        name="norm_proj",
    )(x, g.reshape(1, d), w, c)


def _out_body(a_ref, b_ref, wa_ref, wb_ref, x_ref, g_ref, o_ref):
    y = _dot(a_ref[...], wa_ref[...]) + _dot(b_ref[...], wb_ref[...])
    o_ref[...] = x_ref[...] + _rms(y, g_ref[...])


def _out_proj(oa, ob, wa, wb, x, g_post):
    n, d = x.shape
    ka, kb = oa.shape[1], ob.shape[1]
    return pl.pallas_call(
        _out_body,
        grid=(n // TOK_TILE,),
        in_specs=[pl.BlockSpec((TOK_TILE, ka), lambda i: (i, 0)),
                  pl.BlockSpec((TOK_TILE, kb), lambda i: (i, 0)),
                  _const_spec((ka, d)), _const_spec((kb, d)),
                  pl.BlockSpec((TOK_TILE, d), lambda i: (i, 0)),
                  _const_spec((1, d))],
        out_specs=pl.BlockSpec((TOK_TILE, d), lambda i: (i, 0)),
        out_shape=jax.ShapeDtypeStruct((n, d), F32),
        compiler_params=_params("arbitrary"),
        name="out_proj",
    )(oa, ob, wa.astype(BF16), wb.astype(BF16), x, g_post.reshape(1, d))


def _xattn_body(x_ref, gpre_ref, gpost_ref, wq_ref, kv_ref, wo_ref, o_ref, *, heads, dh):
    x = x_ref[0]
    h = _rms(x, gpre_ref[...]).astype(BF16)
    q = _dot(h, wq_ref[...]).astype(BF16)
    kv = kv_ref[0]
    scale = dh ** -0.5
    outs = []
    for hh in range(heads):
        k = kv[:, hh * dh:(hh + 1) * dh]
        v = kv[:, (heads + hh) * dh:(heads + hh + 1) * dh]
        s = _dot_t(q[:, hh * dh:(hh + 1) * dh], k) * scale
        p = jnp.exp(s - jnp.max(s, axis=-1, keepdims=True))
        p = p / jnp.sum(p, axis=-1, keepdims=True)
        outs.append(_dot(p.astype(BF16), v).astype(BF16))
    y = _dot(jnp.concatenate(outs, axis=-1), wo_ref[...])
    o_ref[0] = x + _rms(y, gpost_ref[...])


def _mem_xattn(x, kv, g_pre, g_post, wq, wo):
    b, s, d = x.shape
    mlen, kvw = kv.shape[1:]
    hd = X_HEADS * X_DH
    return pl.pallas_call(
        functools.partial(_xattn_body, heads=X_HEADS, dh=X_DH),
        grid=(b, s // TOK_TILE),
        in_specs=[pl.BlockSpec((1, TOK_TILE, d), lambda i, j: (i, j, 0)),
                  _const_spec((1, d)), _const_spec((1, d)), _const_spec((d, hd)),
                  pl.BlockSpec((1, mlen, kvw), lambda i, j: (i, 0, 0)),
                  _const_spec((hd, d))],
        out_specs=pl.BlockSpec((1, TOK_TILE, d), lambda i, j: (i, j, 0)),
        out_shape=jax.ShapeDtypeStruct((b, s, d), F32),
        compiler_params=_params("arbitrary", "arbitrary"),
        name="mem_xattn",
    )(x, g_pre.reshape(1, d), g_post.reshape(1, d), wq.astype(BF16), kv, wo.astype(BF16))


def _bucket_np(dist):
    max_exact = N_BUCKETS // 2
    d = np.maximum(dist, 0)
    ratio = np.log(np.maximum(d, 1).astype(np.float32) / np.float32(max_exact)) / np.float32(
        math.log(MAX_DISTANCE / max_exact))
    large = np.minimum(max_exact + (ratio * np.float32(N_BUCKETS - max_exact)).astype(np.int32),
                       N_BUCKETS - 1)
    return np.where(d < max_exact, d, large).astype(np.int32)


def _rel_bias(tab, dist, visible, rel_to_far, head_axis=0):
    dist = np.maximum(np.asarray(dist), 0).astype(np.int32)
    bk = _bucket_np(np.arange(max(int(dist.max()), 4 * MAX_DISTANCE) + 1))
    assert (np.diff(bk) >= 0).all() and bk[-1] == N_BUCKETS - 1
    tab = tab.astype(F32)
    col = lambda b: tab[b].reshape((1,) * head_axis + (-1,) + (1,) * (dist.ndim - head_axis))
    d = jnp.expand_dims(jnp.asarray(dist), head_axis)
    out = jnp.broadcast_to(col(0), dist.shape[:head_axis] + (tab.shape[1],) + dist.shape[head_axis:])
    for b in range(1, N_BUCKETS):
        first = np.nonzero(bk == b)[0]
        if first.size:
            out = jnp.where(d >= int(first[0]), col(b), out)
    if rel_to_far:
        out = out - col(N_BUCKETS - 1)
    return jnp.where(jnp.expand_dims(jnp.asarray(np.asarray(visible)), head_axis), out * LOG2E, NEG_INF)


def _far_distance():
    bk = _bucket_np(np.arange(4 * MAX_DISTANCE))
    return int(np.nonzero(bk == N_BUCKETS - 1)[0][0])


def _bias_tiles(tab, t):
    assert t + 1 >= _far_distance()
    r = np.arange(t)[:, None]
    c = np.arange(t)[None, :]
    dist = np.stack([r - c, t + r - c, 2 * t + r - c])
    everywhere = np.ones((t, t), bool)
    return _rel_bias(tab, dist, np.stack([r >= c, everywhere, everywhere]), True)


def _near_tiles(tab, t):
    tiles = _bias_tiles(tab, t)
    hidden = jnp.full_like(tiles[:, 0], NEG_INF)
    return jnp.stack([jnp.concatenate([tiles[:, 1], tiles[:, 0]], axis=-1),
                      jnp.concatenate([tiles[:, 0], hidden], axis=-1)], axis=1)


def _key_loops(qt, tq, wide, wide_step, near_step, init):
    far = jnp.maximum(qt - 1, 0) * tq
    tw = wide * tq
    n_wide = (far + tw - 1) // tw

    def body(j, carry):
        start = pl.multiple_of(jnp.maximum(far - (n_wide - j) * tw, 0), tq)
        return wide_step(start, jnp.where(j == 0, far - (n_wide - 1) * tw, tw), carry)

    carry = lax.fori_loop(0, n_wide, body, init)
    return near_step(pl.multiple_of(far, tq), jnp.where(qt == 0, 1, 0), carry)


def _cmp_body(x_ref, pe_ref, w_ref, o_ref):
    x = x_ref[0]
    xn = pltpu.roll(x, x.shape[0] - 1, 0)
    lo = (x + pe_ref[0:1, :]).astype(BF16)
    hi = (xn + pe_ref[1:2, :]).astype(BF16)
    o_ref[0] = (_dot(lo, w_ref[0]) + _dot(hi, w_ref[1])).astype(BF16)


def _nsa_compress(kcvc, cmp_pe, cmp_w, b, s):
    nb = s // CMP_STRIDE
    halves = CMP_LEN // CMP_STRIDE
    n_grp = 2 * A_KV
    kdim = CMP_STRIDE * n_grp * A_DH
    cw = cmp_w.reshape(2, halves, CMP_STRIDE, A_DH, A_DH)
    same = jnp.eye(n_grp, dtype=F32).reshape(2, A_KV, 2, A_KV)
    w = jnp.einsum("khlde,kgKG->hlkgdKGe", cw, same)
    w = jnp.pad(w, [(0, 0)] * 7 + [(0, LANES - A_DH)]).reshape(halves, kdim, n_grp * LANES).astype(BF16)
    pe = cmp_pe.reshape(2, halves, CMP_STRIDE, 1, A_DH).transpose(1, 2, 0, 3, 4)
    pe = jnp.broadcast_to(pe, (halves, CMP_STRIDE, 2, A_KV, A_DH)).reshape(halves, kdim)
    return pl.pallas_call(
        _cmp_body,
        grid=(b,),
        in_specs=[pl.BlockSpec((1, nb, kdim), lambda i: (i, 0, 0)),
                  _const_spec((halves, kdim)), _const_spec((halves, kdim, n_grp * LANES))],
        out_specs=pl.BlockSpec((1, nb, n_grp * LANES), lambda i: (i, 0, 0)),
        out_shape=jax.ShapeDtypeStruct((b, nb, n_grp * LANES), BF16),
        compiler_params=_params("arbitrary"),
        name="nsa_compress",
    )(kcvc.reshape(b, nb, kdim), pe, w)


def _nsat_body(qt_ref, kc_ref, vct_ref, bc_ref, ks_ref, vst_ref, kw_ref, vwt_ref, ga_ref, near_ref,
               ovl_ref, exp_ref, o_ref, mask_ref, *, tq, seq, n_sb):
    qi = pl.program_id(2)
    rr = A_GRP
    tw = 2 * tq
    q_all = jnp.concatenate([qt_ref[0, r * A_DH:(r + 1) * A_DH, :] for r in range(rr)], axis=1)
    head = lambda a, r: a[:, r * tq:(r + 1) * tq]
    heads = lambda a: jnp.concatenate([a] * rr, axis=1)

    s_c = _dot(kc_ref[0, :, :A_DH], q_all) + bc_ref[0, 0]
    p = jnp.where(s_c > 0.5 * NEG_INF, jnp.exp2(s_c - jnp.max(s_c, axis=0, keepdims=True)), 0.0)
    l = jnp.sum(p, axis=0, keepdims=True)
    pn = p * jnp.where(l > 0.0, 1.0 / l, 0.0)
    o_c = _dot(vct_ref[0, 0], pn.astype(BF16))

    imp = head(pn, 0)
    for r in range(1, rr):
        imp = imp + head(pn, r)
    imp_t = jnp.dot(ovl_ref[...], imp, precision=lax.Precision.HIGHEST, preferred_element_type=F32)
    blk = lax.broadcasted_iota(I32, (LANES, tq), 0)
    cur = (qi * tq + lax.broadcasted_iota(I32, (LANES, tq), 1)) // SLC_LEN
    forced = (blk == 0) | (blk == cur) | (blk == cur - 1)
    score = jnp.where(blk <= cur, imp_t + jnp.where(forced, FORCE_BONUS, 0.0), NEG_INF)
    groups = [score[8 * v:8 * v + 8] for v in range(-(-n_sb // 8))]
    ranks = [jnp.zeros((8, tq), F32) for _ in groups]
    sub = lax.broadcasted_iota(I32, (8, tq), 0)
    for i in range(n_sb):
        ci = score[i:i + 1, :]
        for v, sv in enumerate(groups):
            ge = jnp.where(ci >= sv, 1.0, 0.0)
            gt = jnp.where(ci > sv, 1.0, 0.0)
            if 8 * v > i:
                ranks[v] = ranks[v] + ge
            elif 8 * v + 7 < i:
                ranks[v] = ranks[v] + gt
            else:
                ranks[v] = ranks[v] + jnp.where(sub + 8 * v > i, ge, gt)
    rank = jnp.concatenate(ranks + [jnp.full((LANES - 8 * len(groups), tq), float(LANES), F32)], axis=0)
    sel = jnp.where(rank < float(min(SLC_TOP, n_sb)), jnp.where(blk <= cur, 1.0, 0.0), 0.0).astype(BF16)

    def expand(c, _):
        cs = pl.multiple_of(c * tw, tw)
        mask_ref[pl.ds(cs, tw), :] = (_dot(exp_ref[pl.ds(cs, tw), :], sel) - 1.0) * (-NEG_INF)
        return 0

    lax.fori_loop(0, qi // 2 + 1, expand, 0)

    def step(k_ref_, vt_ref_, ks, width, add, carry):
        m, acc = carry
        s = _dot(k_ref_[0, pl.ds(ks, width), :A_DH], q_all) + add
        m_new = jnp.maximum(m, jnp.max(s, axis=0, keepdims=True))
        p = jnp.exp2(s - m_new)
        return m_new, jnp.exp2(m - m_new) * acc + _dot(vt_ref_[0, 0, :, pl.ds(ks, width)], p.astype(BF16))

    init = jnp.full((1, rr * tq), -jnp.inf, F32), jnp.zeros((vst_ref.shape[2], rr * tq), F32)
    key_i = lax.broadcasted_iota(I32, (tw, tq), 0)

    def slc_wide(ks, limit, carry):
        own = jnp.where(key_i < limit, mask_ref[pl.ds(ks, tw), :], NEG_INF)
        return step(ks_ref, vst_ref, ks, tw, heads(own), carry)

    def slc_near(ks, which, carry):
        return step(ks_ref, vst_ref, ks, tw, heads(mask_ref[pl.ds(ks, tw), :]) + near_ref[0, which], carry)

    _, acc_s = _key_loops(qi, tq, 2, slc_wide, slc_near, init)

    def win_oldest():
        ks = pl.multiple_of((qi - WIN // tq) * tq, tq)
        kq = lax.broadcasted_iota(I32, (tq, tq), 0)
        qq = lax.broadcasted_iota(I32, (tq, tq), 1)
        return step(kw_ref, vwt_ref, ks, tq, heads(jnp.where(qq < kq, 0.0, NEG_INF)), init)

    carry_w = lax.cond(qi >= WIN // tq, win_oldest, lambda: init)
    far = pl.multiple_of(jnp.maximum(qi - 1, 0) * tq, tq)
    _, acc_w = step(kw_ref, vwt_ref, far, tw, near_ref[0, jnp.where(qi == 0, 1, 0)], carry_w)

    gs = _sigmoid(ga_ref[0, 0])
    for r in range(rr):
        a_s, a_w = head(acc_s, r), head(acc_w, r)
        o = (gs[3 * r:3 * r + 1] * head(o_c, r)[:A_DH]
             + gs[3 * r + 1:3 * r + 2] * (a_s[:A_DH] / a_s[A_DH:A_DH + 1])
             + gs[3 * r + 2:3 * r + 3] * (a_w[:A_DH] / a_w[A_DH:A_DH + 1]))
        o_ref[0, r * A_DH:(r + 1) * A_DH, :] = o.astype(BF16)


def _nsat(q, kvc, kvsw, ga, tab, b, s):
    tq = LANE_TQ
    nq = s // tq
    ncp = s // CMP_STRIDE
    n_sb = s // SLC_LEN
    n_vt = A_DH + 16
    assert n_sb <= LANES and WIN == 2 * tq and s % (2 * tq) == 0
    qt = jnp.swapaxes(q, 1, 2)
    vct = jnp.swapaxes(kvc.reshape(b, ncp, 2 * A_KV, LANES)[:, :, A_KV:, :n_vt], 1, 3).swapaxes(1, 2)
    vt = jnp.swapaxes(kvsw.reshape(b, s, 4 * A_KV, LANES)[..., :n_vt], 1, 3).swapaxes(1, 2)
    gat = jnp.swapaxes(ga.reshape(b, s, A_KV, LANES)[..., :16], 1, 3).swapaxes(1, 2)
    t = (np.arange(nq)[:, None, None] * tq + np.arange(tq)[None, None, :])
    cmp_end = (np.arange(ncp) * CMP_STRIDE + CMP_LEN - 1)[None, :, None]
    bias_c = jnp.stack([_rel_bias(tab[:, g * A_GRP:(g + 1) * A_GRP], t - cmp_end, cmp_end <= t, False,
                                  head_axis=2).reshape(nq, ncp, A_GRP * tq) for g in range(A_KV)])
    near = _near_tiles(tab, tq).reshape(A_KV, A_GRP, 2, tq, 2 * tq).transpose(0, 2, 4, 1, 3)
    near = near.reshape(A_KV, 2, 2 * tq, A_GRP * tq)
    cmp_start = np.arange(ncp)[None, :] * CMP_STRIDE
    sb_start = np.arange(LANES)[:, None] * SLC_LEN
    ovl = ((cmp_start < sb_start + SLC_LEN) & (cmp_start + CMP_LEN - 1 >= sb_start)
           & (np.arange(ncp)[None, :] < ncp - 1) & (np.arange(LANES)[:, None] < n_sb))
    ovl = jnp.asarray(ovl.astype(np.float32))
    expand = jnp.asarray((np.arange(s)[:, None] // SLC_LEN == np.arange(LANES)[None, :]).astype(np.float32), BF16)
    k_spec = lambda blk: pl.BlockSpec((1, s, LANES), lambda i, g, j: (i, 0, 2 * blk + g))
    vt_spec = lambda blk: pl.BlockSpec((1, 1, n_vt, s), lambda i, g, j: (i, 2 * blk + g, 0, 0))
    o_t = pl.pallas_call(
        functools.partial(_nsat_body, tq=tq, seq=s, n_sb=n_sb),
        grid=(b, A_KV, nq),
        in_specs=[pl.BlockSpec((1, A_GRP * A_DH, tq), lambda i, g, j: (i, g, j)),
                  pl.BlockSpec((1, ncp, LANES), lambda i, g, j: (i, 0, g)),
                  pl.BlockSpec((1, 1, n_vt, ncp), lambda i, g, j: (i, g, 0, 0)),
                  pl.BlockSpec((1, 1, ncp, A_GRP * tq), lambda i, g, j: (g, j, 0, 0)),
                  k_spec(0), vt_spec(1), k_spec(2), vt_spec(3),
                  pl.BlockSpec((1, 1, 16, tq), lambda i, g, j: (i, g, 0, j)),
                  pl.BlockSpec((1, 2, 2 * tq, A_GRP * tq), lambda i, g, j: (g, 0, 0, 0)),
                  _const_spec((LANES, ncp)), _const_spec((s, LANES))],
        out_specs=pl.BlockSpec((1, A_GRP * A_DH, tq), lambda i, g, j: (i, g, j)),
        out_shape=jax.ShapeDtypeStruct((b, A_HEADS * A_DH, s), BF16),
        scratch_shapes=[pltpu.VMEM((s, tq), F32)],
        compiler_params=_params("arbitrary", "arbitrary", "arbitrary"),
        name="nsa_attn",
    )(qt, kvc, vct, bias_c, kvsw, vt, kvsw, vt, gat, near, ovl, expand)
    return jnp.swapaxes(o_t, 1, 2)


def _dsat_body(qt_ref, k_ref, vt_ref, iqt_ref, ik_ref, iwt_ref, bias_ref, o_ref, keys_ref, *,
               tq, seq, k_top):
    qi = pl.program_id(1)
    nh = B_HEADS
    tw = 2 * tq
    n_wide = qi // 2 + 1
    key_i = lax.broadcasted_iota(I32, (tw, tq), 0)
    qry_i = lax.broadcasted_iota(I32, (tw, tq), 1)

    iwt = iwt_ref[0]

    def scores(kw, _):
        ks = pl.multiple_of(kw * tw, tw)
        ik = ik_ref[0, pl.ds(ks, tw), :IDX_DH]
        sc = jnp.zeros((tw, tq), F32)
        for h in range(IDX_HEADS):
            lg = _dot(ik, iqt_ref[0, h * IDX_DH:(h + 1) * IDX_DH, :])
            sc = sc + jnp.maximum(lg, 0.0) * iwt[h:h + 1, :]
        bits = lax.bitcast_convert_type(sc, I32)
        key = jnp.where(bits < 0, bits ^ 0x7FFFFFFF, bits)
        vis = (key_i + ks) <= (qry_i + qi * tq)
        keys_ref[pl.ds(ks, tw), :] = jnp.where(vis, key, INT_MIN)
        return 0

    lax.fori_loop(0, n_wide, scores, 0)

    def count(ind):
        def body(kw, acc):
            ks = pl.multiple_of(kw * tw, tw)
            x = ind(keys_ref[pl.ds(ks, tw), :], kw)
            return acc + jnp.sum(x.reshape(tw // ACC_ROWS, ACC_ROWS, tq), axis=0)
        acc = lax.fori_loop(0, n_wide, body, jnp.zeros((ACC_ROWS, tq), F32))
        return jnp.sum(acc, axis=0, keepdims=True)

    kf = float(k_top)
    c0 = count(lambda t, kw: jnp.where(t >= 0, 1.0, 0.0))
    thr = jnp.where(c0 >= kf, 0, INT_MIN).astype(I32)
    c_ge = jnp.where(c0 >= kf, c0, float(seq))

    def bisect(i, state):
        thr, c_ge = state
        cand = thr | (jnp.int32(1) << (30 - i))
        c = count(lambda t, kw: jnp.where(t >= cand, 1.0, 0.0))
        return jnp.where(c >= kf, cand, thr), jnp.where(c >= kf, c, c_ge)

    thr, c_ge = lax.fori_loop(0, 31, bisect, (thr, c_ge))

    n_bits = int(math.log2(seq))
    assert 2 ** n_bits == seq

    def cutoff():
        need = kf - count(lambda t, kw: jnp.where(t > thr, 1.0, 0.0))

        def step(i, cut):
            cand = cut | (jnp.int32(1) << (n_bits - 1 - i))
            c = count(lambda t, kw: jnp.where(t == thr, jnp.where(key_i + kw * tw < cand, 1.0, 0.0), 0.0))
            return jnp.where(c < need, cand, cut)
        return lax.fori_loop(0, n_bits, step, jnp.zeros((1, tq), I32))

    excess = jnp.max(jnp.where(c_ge > kf, 1.0, 0.0)) > 0.5
    cut = lax.cond(excess, cutoff, lambda: jnp.full((1, tq), seq, I32))

    q_all = jnp.concatenate([qt_ref[0, h * B_DH:(h + 1) * B_DH, :] for h in range(nh)], axis=1)

    def step(ks, own, bias, carry):
        m, acc = carry
        key = keys_ref[pl.ds(ks, tw), :]
        tie = jnp.where(key_i + ks <= cut, 0.0, NEG_INF)
        sel = jnp.where(key > thr, 0.0, jnp.where(key == thr, tie, NEG_INF))
        if own is not None:
            sel = jnp.where(own, sel, NEG_INF)
        s = _dot(k_ref[0, pl.ds(ks, tw), :B_DH], q_all) + jnp.concatenate([sel] * nh, axis=1)
        if bias is not None:
            s = s + bias
        m_new = jnp.maximum(m, jnp.max(s, axis=0, keepdims=True))
        p = jnp.exp2(s - m_new)
        return m_new, jnp.exp2(m - m_new) * acc + _dot(vt_ref[0, :, pl.ds(ks, tw)], p.astype(BF16))

    def wide_step(ks, limit, carry):
        return step(ks, key_i < limit, None, carry)

    def near_step(ks, which, carry):
        return step(ks, None, bias_ref[which], carry)

    init = jnp.full((1, nh * tq), -jnp.inf, F32), jnp.zeros((vt_ref.shape[1], nh * tq), F32)
    _, acc = _key_loops(qi, tq, 2, wide_step, near_step, init)
    for h in range(nh):
        a = acc[:, h * tq:(h + 1) * tq]
        o_ref[0, h * B_DH:(h + 1) * B_DH, :] = (a[:B_DH] / a[B_DH:B_DH + 1]).astype(BF16)


def _dsat(q, kbvb, iq, ik, iw, tab, b, s):
    tq = LANE_TQ
    k_top = min(IDX_TOPK_MAX, s // 4)
    assert s % (2 * tq) == 0
    qt = jnp.swapaxes(q, 1, 2)
    iqt = jnp.swapaxes(iq, 1, 2)
    iwt = jnp.swapaxes(iw[:, :, :IDX_HEADS], 1, 2)
    n_vt = B_DH + 16
    vt = jnp.swapaxes(kbvb[:, :, LANES:LANES + n_vt], 1, 2)
    near = _near_tiles(tab, tq).transpose(1, 3, 0, 2).reshape(2, 2 * tq, B_HEADS * tq)
    o_t = pl.pallas_call(
        functools.partial(_dsat_body, tq=tq, seq=s, k_top=k_top),
        grid=(b, s // tq),
        in_specs=[pl.BlockSpec((1, B_HEADS * B_DH, tq), lambda i, j: (i, 0, j)),
                  pl.BlockSpec((1, s, LANES), lambda i, j: (i, 0, 0)),
                  pl.BlockSpec((1, n_vt, s), lambda i, j: (i, 0, 0)),
                  pl.BlockSpec((1, IDX_HEADS * IDX_DH, tq), lambda i, j: (i, 0, j)),
                  pl.BlockSpec((1, s, LANES), lambda i, j: (i, 0, 0)),
                  pl.BlockSpec((1, IDX_HEADS, tq), lambda i, j: (i, 0, j)),
                  _const_spec((2, 2 * tq, B_HEADS * tq))],
        out_specs=pl.BlockSpec((1, B_HEADS * B_DH, tq), lambda i, j: (i, 0, j)),
        out_shape=jax.ShapeDtypeStruct((b, B_HEADS * B_DH, s), BF16),
        scratch_shapes=[pltpu.VMEM((s, tq), I32)],
        compiler_params=_params("arbitrary", "arbitrary"),
        name="dsa_attn",
    )(qt, kbvb, vt, iqt, ik, iwt, near)
    return jnp.swapaxes(o_t, 1, 2)


def _mla_prep_body(cq_ref, ckv_ref, kr_ref, qn_ref, kvn_ref, wq1_ref, wq2_ref, wk_ref, wv_ref,
                   cos_ref, sin_ref, ones_ref, q_o, k_o, v_o):
    cqn = _rms(cq_ref[...], qn_ref[...]).astype(BF16)
    ckn = _rms(ckv_ref[...], kvn_ref[...]).astype(BF16)
    cos, sin = cos_ref[...], sin_ref[...]
    cos_h = jnp.concatenate([cos] * C_HEADS, axis=1)
    sin_h = jnp.concatenate([sin] * C_HEADS, axis=1)
    q_o[...] = (_dot(cqn, wq1_ref[...]) * cos_h + _dot(cqn, wq2_ref[...]) * sin_h).astype(BF16)
    kr = kr_ref[...]
    kr_rot = kr[:, :LANES] * cos + kr[:, LANES:] * sin
    k_o[...] = (_dot(ckn, wk_ref[...]) + jnp.concatenate([kr_rot] * C_HEADS, axis=1)).astype(BF16)
    v_o[...] = (_dot(ckn, wv_ref[...]) + ones_ref[...]).astype(BF16)


def _rot_cols(w):
    half = C_DR // 2
    return jnp.concatenate([-w[..., half:], w[..., :half]], axis=-1)


def _mla_prep(cq, ckv, kr, q_norm, kv_norm, w_uq, w_ukv, s):
    n = cq.shape[0]
    hw = C_HEADS * LANES
    padq = LANES - C_DN - C_DR
    wq = w_uq.reshape(C_Q_RANK, C_HEADS, C_DN + C_DR) * ((C_DN + C_DR) ** -0.5 * LOG2E)
    wq1 = jnp.pad(wq, ((0, 0), (0, 0), (0, padq))).reshape(C_Q_RANK, hw)
    wq2 = jnp.pad(_rot_cols(wq[..., C_DN:]), ((0, 0), (0, 0), (C_DN, padq))).reshape(C_Q_RANK, hw)
    wkv = w_ukv.reshape(C_KV_RANK, C_HEADS, C_DN + C_DV)
    wk = jnp.pad(wkv[..., :C_DN], ((0, 0), (0, 0), (0, LANES - C_DN))).reshape(C_KV_RANK, hw)
    wv = jnp.pad(wkv[..., C_DN:], ((0, 0), (0, 0), (0, LANES - C_DV))).reshape(C_KV_RANK, hw)
    half = C_DR // 2
    inv = ROPE_THETA ** (-jnp.arange(half, dtype=F32) / half)
    ang = jnp.arange(s, dtype=F32)[:, None] * inv[None, :]
    cos, sin = jnp.cos(ang), jnp.sin(ang)
    cos_t = jnp.concatenate([jnp.ones((s, C_DN), F32), cos, cos, jnp.zeros((s, padq), F32)], axis=1)
    sin_t = jnp.concatenate([jnp.zeros((s, C_DN), F32), sin, sin, jnp.zeros((s, padq), F32)], axis=1)
    ones = jnp.tile(jnp.asarray(np.arange(LANES) == ONES_LANE, F32), C_HEADS).reshape(1, hw)
    tile_tok = lambda w_: pl.BlockSpec((TOK_TILE, w_), lambda i: (i, 0))
    n_pos = s // TOK_TILE
    pos_spec = pl.BlockSpec((TOK_TILE, LANES), lambda i: (i % n_pos, 0))
    return pl.pallas_call(
        _mla_prep_body,
        grid=(n // TOK_TILE,),
        in_specs=[tile_tok(C_Q_RANK), tile_tok(C_KV_RANK), tile_tok(2 * LANES),
                  _const_spec((1, C_Q_RANK)), _const_spec((1, C_KV_RANK)),
                  _const_spec((C_Q_RANK, hw)), _const_spec((C_Q_RANK, hw)),
                  _const_spec((C_KV_RANK, hw)), _const_spec((C_KV_RANK, hw)),
                  pos_spec, pos_spec, _const_spec((1, hw))],
        out_specs=[tile_tok(hw)] * 3,
        out_shape=[jax.ShapeDtypeStruct((n, hw), BF16)] * 3,
        compiler_params=_params("arbitrary"),
        name="mla_prep",
    )(cq, ckv, kr, q_norm.reshape(1, -1), kv_norm.reshape(1, -1), wq1.astype(BF16), wq2.astype(BF16),
      wk.astype(BF16), wv.astype(BF16), cos_t, sin_t, ones)


def _mla_attn_body(q_ref, k_ref, v_ref, o_ref, *, tq, nh):
    qt = pl.program_id(2)
    q = q_ref[0]
    lanes = lambda a, h: a[:, h * LANES:(h + 1) * LANES]

    def step(kt, carry, add):
        ks = pl.multiple_of(kt * tq, tq)
        k = k_ref[0, pl.ds(ks, tq), :]
        v = v_ref[0, pl.ds(ks, tq), :]
        return tuple(_online_step(lanes(q, h), lanes(k, h), lanes(v, h), carry[h], add=add)
                     for h in range(nh))

    init = tuple(_flash_init(tq, LANES) for _ in range(nh))
    carry = lax.fori_loop(0, qt, lambda kt, c: step(kt, c, None), init)
    row = lax.broadcasted_iota(I32, (tq, tq), 0)
    col = lax.broadcasted_iota(I32, (tq, tq), 1)
    carry = step(qt, carry, jnp.where(row >= col, 0.0, NEG_INF))
    for h in range(nh):
        acc = carry[h][1]
        o_ref[0, :, h * LANES:(h + 1) * LANES] = (acc / acc[:, ONES_LANE:ONES_LANE + 1]).astype(BF16)


def _mla_attn(q, k, v, b, s):
    tq = DENSE_TQ
    nh = MLA_HEADS_PER_STEP
    spec_q = pl.BlockSpec((1, tq, nh * LANES), lambda i, h, j: (i, j, h))
    spec_kv = pl.BlockSpec((1, s, nh * LANES), lambda i, h, j: (i, 0, h))
    return pl.pallas_call(
        functools.partial(_mla_attn_body, tq=tq, nh=nh),
        grid=(b, C_HEADS // nh, s // tq),
        in_specs=[spec_q, spec_kv, spec_kv],
        out_specs=spec_q,
        out_shape=jax.ShapeDtypeStruct((b, s, C_HEADS * LANES), BF16),
        compiler_params=_params("arbitrary", "arbitrary", "arbitrary"),
        name="mla_attn",
    )(q, k, v)


def _diff_body(lam_ref, q_ref, k_ref, v_ref, bias_ref, g_ref, o_ref, *, tq, nh, out_scale):
    qt = pl.program_id(2)
    q = q_ref[0]
    lanes = lambda a, c, w=LANES: a[:, c * w:(c + 1) * w]

    def step(kt, carry, near):
        ks = pl.multiple_of(kt * tq, tq)
        k = k_ref[0, pl.ds(ks, tq), :]
        v = v_ref[0, pl.ds(ks, tq), :]
        return tuple(_online_step(lanes(q, c), lanes(k, c), lanes(v, c // 2, 2 * LANES), carry[c],
                                  add=bias_ref[c // 2, qt - kt] if near else None)
                     for c in range(2 * nh))

    n_far = jnp.maximum(qt - 1, 0)
    init = tuple(_flash_init(tq, 2 * LANES) for _ in range(2 * nh))
    carry = lax.fori_loop(0, n_far, functools.partial(step, near=False), init)
    carry = lax.fori_loop(n_far, qt + 1, functools.partial(step, near=True), carry)
    outs = [carry[c][1][:, :D_DV] / carry[c][1][:, D_DV:D_DV + 1] for c in range(2 * nh)]
    for h in range(nh):
        o = outs[2 * h] - lam_ref[0] * outs[2 * h + 1]
        o_ref[0, :, h * D_DV:(h + 1) * D_DV] = (_rms(o, g_ref[...]) * out_scale).astype(BF16)


def _diff_attn(q, k, v, tab, lam_full, subln, lam_init, b, s):
    tq = DENSE_TQ
    nh = DIFF_HEADS_PER_STEP
    tiles = _bias_tiles(tab, tq)[:, :2]
    spec_q = pl.BlockSpec((1, tq, nh * 2 * LANES), lambda i, h, j: (i, j, h))
    spec_kv = pl.BlockSpec((1, s, nh * 2 * LANES), lambda i, h, j: (i, 0, h))
    return pl.pallas_call(
        functools.partial(_diff_body, tq=tq, nh=nh, out_scale=1.0 - lam_init),
        grid=(b, D_HEADS // nh, s // tq),
        in_specs=[pl.BlockSpec(memory_space=pltpu.SMEM),
                  spec_q, spec_kv, spec_kv,
                  pl.BlockSpec((nh, 2, tq, tq), lambda i, h, j: (h, 0, 0, 0)),
                  _const_spec((1, D_DV))],
        out_specs=pl.BlockSpec((1, tq, nh * D_DV), lambda i, h, j: (i, j, h)),
        out_shape=jax.ShapeDtypeStruct((b, s, D_HEADS * D_DV), BF16),
        compiler_params=_params("arbitrary", "arbitrary", "arbitrary"),
        name="diff_attn",
    )(lam_full.reshape(1), q, k, v, tiles, subln.reshape(1, D_DV))


def _pad_blocks(w, n, width, to=LANES, at=0):
    k = w.shape[0]
    return jnp.pad(w.reshape(k, n, width), ((0, 0), (0, 0), (at, to - width - at))).reshape(k, n * to)


def _ones_row(n_blocks, which, to=LANES, lane=ONES_LANE):
    row = np.zeros((n_blocks, to), np.float32)
    row[list(which), lane] = 1.0
    return jnp.asarray(row.reshape(1, n_blocks * to))


def _split(w, sizes):
    return jnp.split(w, [int(c) for c in np.cumsum(sizes)[:-1]], axis=-1)


def _pad_rows(w, n, width):
    d = w.shape[1]
    return jnp.pad(w.reshape(n, width, d), ((0, 0), (0, LANES - width), (0, 0))).reshape(n * LANES, d)


def _even_mixer(x2, g_pre, g_post, w_in, w_out, cmp_pe, cmp_w, rel_tab, b, s):
    qa, kva, ga, qb, kb, vb, iq, ik, iw = _split(w_in, EV_SIZES)
    half = 2 * A_KV * A_DH
    w_segs = [qa * (A_DH ** -0.5 * LOG2E), kva[:, :half],
              _pad_blocks(kva[:, half:], 4 * A_KV, A_DH), _pad_blocks(ga, A_KV, 3 * A_GRP),
              qb * (B_DH ** -0.5 * LOG2E),
              _pad_blocks(jnp.concatenate([kb, vb], axis=1), 2, B_DH),
              iq, _pad_blocks(ik, 1, IDX_DH), _pad_blocks(iw, 1, IDX_HEADS)]
    zeros = lambda w: jnp.zeros((1, w.shape[1]), F32)
    c_segs = [zeros(w) for w in w_segs]
    c_segs[2] = _ones_row(4 * A_KV, (2, 3, 6, 7))
    c_segs[5] = _ones_row(2, (1,))
    dts = [BF16, F32, BF16, F32, BF16, BF16, BF16, BF16, F32]
    q_a, kcvc, kvsw, g_a, q_b, kbvb, i_q, i_k, i_w = _proj(x2, g_pre, w_segs, c_segs, dts)
    r3 = lambda a: a.reshape(b, s, a.shape[-1])
    kvc = _nsa_compress(kcvc, cmp_pe, cmp_w, b, s)
    o_a = _nsat(r3(q_a), kvc, r3(kvsw), r3(g_a), rel_tab[:, :A_HEADS], b, s)
    o_b = _dsat(r3(q_b), r3(kbvb), r3(i_q), r3(i_k), r3(i_w), rel_tab[:, A_HEADS:A_HEADS + B_HEADS], b, s)
    na = A_HEADS * A_DH
    return _out_proj(o_a.reshape(b * s, -1), o_b.reshape(b * s, -1), w_out[:na], w_out[na:], x2, g_post)


def _odd_mixer(x2, g_pre, g_post, w_in, w_out, q_norm, kv_norm, w_uq, w_ukv, lam, subln, rel_tab,
               lam_init, b, s):
    c_q, c_kv, k_rope, qd, kd, vd = _split(w_in, OD_SIZES)
    w_kr = jnp.concatenate([_pad_blocks(k_rope, 1, C_DR, at=C_DN),
                            _pad_blocks(_rot_cols(k_rope), 1, C_DR, at=C_DN)], axis=1)
    w_segs = [c_q, c_kv, w_kr, _pad_blocks(qd * (D_DH ** -0.5 * LOG2E), 2 * D_HEADS, D_DH),
              _pad_blocks(kd, 2 * D_HEADS, D_DH), _pad_blocks(vd, D_HEADS, D_DV, to=2 * LANES)]
    c_segs = [jnp.zeros((1, w.shape[1]), F32) for w in w_segs]
    c_segs[5] = _ones_row(D_HEADS, range(D_HEADS), to=2 * LANES, lane=D_DV)
    dts = [F32, F32, F32, BF16, BF16, BF16]
    cq, ckv, kr, q_d, k_d, v_d = _proj(x2, g_pre, w_segs, c_segs, dts)
    q_m, k_m, v_m = _mla_prep(cq, ckv, kr, q_norm, kv_norm, w_uq, w_ukv, s)
    r3 = lambda a: a.reshape(b, s, a.shape[-1])
    o_c = _mla_attn(r3(q_m), r3(k_m), r3(v_m), b, s)
    lam32 = lam.astype(F32)
    lam_full = jnp.exp(jnp.sum(lam32[0] * lam32[1])) - jnp.exp(jnp.sum(lam32[2] * lam32[3])) + lam_init
    o_d = _diff_attn(r3(q_d), r3(k_d), r3(v_d), rel_tab[:, A_HEADS + B_HEADS:], lam_full, subln,
                     lam_init, b, s)
    nc = C_HEADS * C_DV
    wc = _pad_rows(w_out[:nc], C_HEADS, C_DV)
    return _out_proj(o_c.reshape(b * s, -1), o_d.reshape(b * s, -1), wc, w_out[nc:], x2, g_post)


def kernel(x, mem, rel_bias_table, norm_pre, norm_post, mem_norm, ffn_wg, ffn_wu, ffn_wd, ev_w_in,
           ev_cmp_pe, ev_cmp_w, ev_w_out, od_w_in, od_q_norm, od_kv_norm, od_w_uq, od_w_ukv, od_lambda,
           od_subln, od_w_out, xa_wq, xa_wkv, xa_wo):
    b, s, d = x.shape
    depth = norm_pre.shape[0]
    x2 = x.reshape(b * s, d)
    mem2 = mem.reshape(-1, d)
    for i in range(depth):
        g_pre, g_post = norm_pre[i], norm_post[i]
        x2 = _ffn(x2, g_pre[0], g_post[0], ffn_wg[i, 0], ffn_wu[i, 0], ffn_wd[i, 0])
        if i % 2 == 0:
            e = i // 2
            x2 = _even_mixer(x2, g_pre[1], g_post[1], ev_w_in[e], ev_w_out[e], ev_cmp_pe[e], ev_cmp_w[e],
                             rel_bias_table, b, s)
        else:
            o = i // 2
            lam_init = 0.8 - 0.6 * math.exp(-0.3 * i)
            x2 = _odd_mixer(x2, g_pre[1], g_post[1], od_w_in[o], od_w_out[o], od_q_norm[o], od_kv_norm[o],
                            od_w_uq[o], od_w_ukv[o], od_lambda[o], od_subln[o], rel_bias_table, lam_init, b, s)
        (kv,) = _proj(mem2, mem_norm[i], [xa_wkv[i]], [jnp.zeros((1, xa_wkv.shape[-1]), F32)], [BF16])
        x3 = _mem_xattn(x2.reshape(b, s, d), kv.reshape(b, -1, kv.shape[-1]), g_pre[2], g_post[2],
                        xa_wq[i], xa_wo[i])
        x2 = _ffn(x3.reshape(b * s, d), g_pre[3], g_post[3], ffn_wg[i, 1], ffn_wu[i, 1], ffn_wd[i, 1])
    return x2.reshape(b, s, d)
```

```python
import functools
import math

import numpy as np
import jax
import jax.numpy as jnp
from jax import lax
from jax.experimental import pallas as pl
from jax.experimental.pallas import tpu as pltpu

F32, BF16, I32 = jnp.float32, jnp.bfloat16, jnp.int32

N_BUCKETS = 32
MAX_DISTANCE = 128
RMS_EPS = 1e-6
ROPE_THETA = 10000.0
NEG_INF = -1e30
A_HEADS, A_KV, A_DH = 8, 2, 64
A_GRP = A_HEADS // A_KV
CMP_LEN, CMP_STRIDE, SLC_LEN, SLC_TOP, WIN = 32, 16, 64, 16, 512
FORCE_BONUS = 1e3
B_HEADS, B_DH, IDX_HEADS, IDX_DH, IDX_TOPK_MAX = 8, 64, 8, 32, 256
C_HEADS, C_Q_RANK, C_KV_RANK, C_DN, C_DR, C_DV = 8, 256, 128, 64, 32, 64
D_HEADS, D_DH, D_DV = 4, 64, 128
X_HEADS, X_DH = 4, 128
EV_SIZES = (A_HEADS * A_DH, 6 * A_KV * A_DH, 3 * A_HEADS,
            B_HEADS * B_DH, B_DH, B_DH, IDX_HEADS * IDX_DH, IDX_DH, IDX_HEADS)
OD_SIZES = (C_Q_RANK, C_KV_RANK, C_DR, D_HEADS * 2 * D_DH, D_HEADS * 2 * D_DH, D_HEADS * D_DV)

LANES = 128
VMEM_LIMIT_BYTES = 56 * 1024 * 1024
ONES_LANE = 64
INT_MIN = -2 ** 31
LOG2E = 1.4426950408889634

TOK_TILE = 512
FFN_CHUNK = 256
LANE_TQ = 256
ACC_ROWS = 32
DENSE_TQ = 512
MLA_HEADS_PER_STEP = 4
DIFF_HEADS_PER_STEP = 2


def _params(*sem):
    return pltpu.CompilerParams(dimension_semantics=sem, vmem_limit_bytes=VMEM_LIMIT_BYTES)


def _const_spec(shape):
    nd = len(shape)
    return pl.BlockSpec(shape, lambda *_: (0,) * nd, pipeline_mode=pl.Buffered(1))


def _dot(a, b):
    return jnp.dot(a, b, preferred_element_type=F32)


def _dot_t(a, b):
    return lax.dot_general(a, b, (((1,), (1,)), ((), ())), preferred_element_type=F32)


def _rms(x, g):
    return x * lax.rsqrt(jnp.mean(x * x, axis=-1, keepdims=True) + RMS_EPS) * g


def _sigmoid(x):
    return 1.0 / (1.0 + jnp.exp(-x))


def _online_step(q, k, v, carry, add=None):
    m, acc = carry
    s = _dot_t(q, k)
    if add is not None:
        s = s + add
    m_new = jnp.maximum(m, jnp.max(s, axis=-1, keepdims=True))
    p = jnp.exp2(s - m_new)
    acc = jnp.exp2(m - m_new) * acc + _dot(p.astype(BF16), v)
    return m_new, acc


def _flash_init(m_rows, n_lanes):
    return jnp.full((m_rows, 1), -jnp.inf, F32), jnp.zeros((m_rows, n_lanes), F32)


def _ffn_body(x_ref, gpre_ref, gpost_ref, wgu_ref, wd_ref, o_ref, *, n_chunks, fc):
    x = x_ref[...]
    h = _rms(x, gpre_ref[...]).astype(BF16)
    acc = jnp.zeros(x.shape, F32)
    for c in range(n_chunks):
        gu = _dot(h, wgu_ref[:, c * 2 * fc:(c + 1) * 2 * fc])
        g, u = gu[:, :fc], gu[:, fc:]
        a = g * _sigmoid(g) * u
        acc = acc + _dot(a.astype(BF16), wd_ref[c * fc:(c + 1) * fc, :])
    o_ref[...] = x + 0.5 * _rms(acc, gpost_ref[...])


def _ffn(x, g_pre, g_post, wg, wu, wd):
    n, d = x.shape
    f = wg.shape[1]
    fc = FFN_CHUNK
    n_chunks = f // fc
    assert n_chunks * fc == f and n % TOK_TILE == 0
    wgu = jnp.concatenate([wg.reshape(d, n_chunks, fc), wu.reshape(d, n_chunks, fc)], axis=-1)
    wgu = wgu.reshape(d, 2 * f).astype(BF16)
    return pl.pallas_call(
        functools.partial(_ffn_body, n_chunks=n_chunks, fc=fc),
        grid=(n // TOK_TILE,),
        in_specs=[pl.BlockSpec((TOK_TILE, d), lambda i: (i, 0)),
                  _const_spec((1, d)), _const_spec((1, d)),
                  _const_spec((d, 2 * f)), _const_spec((f, d))],
        out_specs=pl.BlockSpec((TOK_TILE, d), lambda i: (i, 0)),
        out_shape=jax.ShapeDtypeStruct((n, d), F32),
        compiler_params=_params("arbitrary"),
        name="ffn",
    )(x, g_pre.reshape(1, d), g_post.reshape(1, d), wgu, wd.astype(BF16))


def _proj_body(x_ref, g_ref, w_ref, c_ref, *o_refs, bounds):
    h = _rms(x_ref[...], g_ref[...]).astype(BF16)
    for o_ref, (a, b) in zip(o_refs, bounds):
        o_ref[...] = (_dot(h, w_ref[:, a:b]) + c_ref[:, a:b]).astype(o_ref.dtype)


def _proj(x, g, w_segs, c_segs, dtypes):
    n, d = x.shape
    widths = [w.shape[1] for w in w_segs]
    assert all(wd_ % LANES == 0 for wd_ in widths) and n % TOK_TILE == 0
    offs = np.concatenate([[0], np.cumsum(widths)])
    bounds = tuple((int(offs[i]), int(offs[i + 1])) for i in range(len(widths)))
    w = jnp.concatenate(w_segs, axis=1).astype(BF16)
    c = jnp.concatenate(c_segs, axis=1).astype(F32)
    tot = int(offs[-1])
    return pl.pallas_call(
        functools.partial(_proj_body, bounds=bounds),
        grid=(n // TOK_TILE,),
        in_specs=[pl.BlockSpec((TOK_TILE, d), lambda i: (i, 0)),
                  _const_spec((1, d)), _const_spec((d, tot)), _const_spec((1, tot))],
        out_specs=[pl.BlockSpec((TOK_TILE, wd_), lambda i: (i, 0)) for wd_ in widths],
        out_shape=[jax.ShapeDtypeStruct((n, wd_), dt) for wd_, dt in zip(widths, dtypes)],
        compiler_params=_params("arbitrary"),
        name="norm_proj",
    )(x, g.reshape(1, d), w, c)


def _out_body(a_ref, b_ref, wa_ref, wb_ref, x_ref, g_ref, o_ref):
    y = _dot(a_ref[...], wa_ref[...]) + _dot(b_ref[...], wb_ref[...])
    o_ref[...] = x_ref[...] + _rms(y, g_ref[...])


def _out_proj(oa, ob, wa, wb, x, g_post):
    n, d = x.shape
    ka, kb = oa.shape[1], ob.shape[1]
    return pl.pallas_call(
        _out_body,
        grid=(n // TOK_TILE,),
        in_specs=[pl.BlockSpec((TOK_TILE, ka), lambda i: (i, 0)),
                  pl.BlockSpec((TOK_TILE, kb), lambda i: (i, 0)),
                  _const_spec((ka, d)), _const_spec((kb, d)),
                  pl.BlockSpec((TOK_TILE, d), lambda i: (i, 0)),
                  _const_spec((1, d))],
        out_specs=pl.BlockSpec((TOK_TILE, d), lambda i: (i, 0)),
        out_shape=jax.ShapeDtypeStruct((n, d), F32),
        compiler_params=_params("arbitrary"),
        name="out_proj",
    )(oa, ob, wa.astype(BF16), wb.astype(BF16), x, g_post.reshape(1, d))


def _xattn_body(x_ref, gpre_ref, gpost_ref, wq_ref, kv_ref, wo_ref, o_ref, *, heads, dh):
    x = x_ref[0]
    h = _rms(x, gpre_ref[...]).astype(BF16)
    q = _dot(h, wq_ref[...]).astype(BF16)
    kv = kv_ref[0]
    scale = dh ** -0.5
    outs = []
    for hh in range(heads):
        k = kv[:, hh * dh:(hh + 1) * dh]
        v = kv[:, (heads + hh) * dh:(heads + hh + 1) * dh]
        s = _dot_t(q[:, hh * dh:(hh + 1) * dh], k) * scale
        p = jnp.exp(s - jnp.max(s, axis=-1, keepdims=True))
        p = p / jnp.sum(p, axis=-1, keepdims=True)
        outs.append(_dot(p.astype(BF16), v).astype(BF16))
    y = _dot(jnp.concatenate(outs, axis=-1), wo_ref[...])
    o_ref[0] = x + _rms(y, gpost_ref[...])


def _mem_xattn(x, kv, g_pre, g_post, wq, wo):
    b, s, d = x.shape
    mlen, kvw = kv.shape[1:]
    hd = X_HEADS * X_DH
    return pl.pallas_call(
        functools.partial(_xattn_body, heads=X_HEADS, dh=X_DH),
        grid=(b, s // TOK_TILE),
        in_specs=[pl.BlockSpec((1, TOK_TILE, d), lambda i, j: (i, j, 0)),
                  _const_spec((1, d)), _const_spec((1, d)), _const_spec((d, hd)),
                  pl.BlockSpec((1, mlen, kvw), lambda i, j: (i, 0, 0)),
                  _const_spec((hd, d))],
        out_specs=pl.BlockSpec((1, TOK_TILE, d), lambda i, j: (i, j, 0)),
        out_shape=jax.ShapeDtypeStruct((b, s, d), F32),
        compiler_params=_params("arbitrary", "arbitrary"),
        name="mem_xattn",
    )(x, g_pre.reshape(1, d), g_post.reshape(1, d), wq.astype(BF16), kv, wo.astype(BF16))


def _bucket_np(dist):
    max_exact = N_BUCKETS // 2
    d = np.maximum(dist, 0)
    ratio = np.log(np.maximum(d, 1).astype(np.float32) / np.float32(max_exact)) / np.float32(
        math.log(MAX_DISTANCE / max_exact))
    large = np.minimum(max_exact + (ratio * np.float32(N_BUCKETS - max_exact)).astype(np.int32),
                       N_BUCKETS - 1)
    return np.where(d < max_exact, d, large).astype(np.int32)


def _rel_bias(tab, dist, visible, rel_to_far, head_axis=0):
    dist = np.maximum(np.asarray(dist), 0).astype(np.int32)
    bk = _bucket_np(np.arange(max(int(dist.max()), 4 * MAX_DISTANCE) + 1))
    assert (np.diff(bk) >= 0).all() and bk[-1] == N_BUCKETS - 1
    tab = tab.astype(F32)
    col = lambda b: tab[b].reshape((1,) * head_axis + (-1,) + (1,) * (dist.ndim - head_axis))
    d = jnp.expand_dims(jnp.asarray(dist), head_axis)
    out = jnp.broadcast_to(col(0), dist.shape[:head_axis] + (tab.shape[1],) + dist.shape[head_axis:])
    for b in range(1, N_BUCKETS):
        first = np.nonzero(bk == b)[0]
        if first.size:
            out = jnp.where(d >= int(first[0]), col(b), out)
    if rel_to_far:
        out = out - col(N_BUCKETS - 1)
    return jnp.where(jnp.expand_dims(jnp.asarray(np.asarray(visible)), head_axis), out * LOG2E, NEG_INF)


def _far_distance():
    bk = _bucket_np(np.arange(4 * MAX_DISTANCE))
    return int(np.nonzero(bk == N_BUCKETS - 1)[0][0])


def _bias_tiles(tab, t):
    assert t + 1 >= _far_distance()
    r = np.arange(t)[:, None]
    c = np.arange(t)[None, :]
    dist = np.stack([r - c, t + r - c, 2 * t + r - c])
    everywhere = np.ones((t, t), bool)
    return _rel_bias(tab, dist, np.stack([r >= c, everywhere, everywhere]), True)


def _near_tiles(tab, t):
    tiles = _bias_tiles(tab, t)
    hidden = jnp.full_like(tiles[:, 0], NEG_INF)
    return jnp.stack([jnp.concatenate([tiles[:, 1], tiles[:, 0]], axis=-1),
                      jnp.concatenate([tiles[:, 0], hidden], axis=-1)], axis=1)


def _key_loops(qt, tq, wide, wide_step, near_step, init):
    far = jnp.maximum(qt - 1, 0) * tq
    tw = wide * tq
    n_wide = (far + tw - 1) // tw

    def body(j, carry):
        start = pl.multiple_of(jnp.maximum(far - (n_wide - j) * tw, 0), tq)
        return wide_step(start, jnp.where(j == 0, far - (n_wide - 1) * tw, tw), carry)

    carry = lax.fori_loop(0, n_wide, body, init)
    return near_step(pl.multiple_of(far, tq), jnp.where(qt == 0, 1, 0), carry)


def _cmp_body(x_ref, pe_ref, w_ref, o_ref):
    x = x_ref[0]
    xn = pltpu.roll(x, x.shape[0] - 1, 0)
    lo = (x + pe_ref[0:1, :]).astype(BF16)
    hi = (xn + pe_ref[1:2, :]).astype(BF16)
    o_ref[0] = (_dot(lo, w_ref[0]) + _dot(hi, w_ref[1])).astype(BF16)


def _nsa_compress(kcvc, cmp_pe, cmp_w, b, s):
    nb = s // CMP_STRIDE
    halves = CMP_LEN // CMP_STRIDE
    n_grp = 2 * A_KV
    kdim = CMP_STRIDE * n_grp * A_DH
    cw = cmp_w.reshape(2, halves, CMP_STRIDE, A_DH, A_DH)
    same = jnp.eye(n_grp, dtype=F32).reshape(2, A_KV, 2, A_KV)
    w = jnp.einsum("khlde,kgKG->hlkgdKGe", cw, same)
    w = jnp.pad(w, [(0, 0)] * 7 + [(0, LANES - A_DH)]).reshape(halves, kdim, n_grp * LANES).astype(BF16)
    pe = cmp_pe.reshape(2, halves, CMP_STRIDE, 1, A_DH).transpose(1, 2, 0, 3, 4)
    pe = jnp.broadcast_to(pe, (halves, CMP_STRIDE, 2, A_KV, A_DH)).reshape(halves, kdim)
    return pl.pallas_call(
        _cmp_body,
        grid=(b,),
        in_specs=[pl.BlockSpec((1, nb, kdim), lambda i: (i, 0, 0)),
                  _const_spec((halves, kdim)), _const_spec((halves, kdim, n_grp * LANES))],
        out_specs=pl.BlockSpec((1, nb, n_grp * LANES), lambda i: (i, 0, 0)),
        out_shape=jax.ShapeDtypeStruct((b, nb, n_grp * LANES), BF16),
        compiler_params=_params("arbitrary"),
        name="nsa_compress",
    )(kcvc.reshape(b, nb, kdim), pe, w)


def _nsat_body(qt_ref, kc_ref, vct_ref, bc_ref, ks_ref, vst_ref, kw_ref, vwt_ref, ga_ref, near_ref,
               ovl_ref, exp_ref, o_ref, mask_ref, *, tq, seq, n_sb):
    qi = pl.program_id(2)
    rr = A_GRP
    tw = 2 * tq
    q_all = jnp.concatenate([qt_ref[0, r * A_DH:(r + 1) * A_DH, :] for r in range(rr)], axis=1)
    head = lambda a, r: a[:, r * tq:(r + 1) * tq]
    heads = lambda a: jnp.concatenate([a] * rr, axis=1)

    s_c = _dot(kc_ref[0, :, :A_DH], q_all) + bc_ref[0, 0]
    p = jnp.where(s_c > 0.5 * NEG_INF, jnp.exp2(s_c - jnp.max(s_c, axis=0, keepdims=True)), 0.0)
    l = jnp.sum(p, axis=0, keepdims=True)
    pn = p * jnp.where(l > 0.0, 1.0 / l, 0.0)
    o_c = _dot(vct_ref[0, 0], pn.astype(BF16))

    imp = head(pn, 0)
    for r in range(1, rr):
        imp = imp + head(pn, r)
    imp_t = jnp.dot(ovl_ref[...], imp, precision=lax.Precision.HIGHEST, preferred_element_type=F32)
    blk = lax.broadcasted_iota(I32, (LANES, tq), 0)
    cur = (qi * tq + lax.broadcasted_iota(I32, (LANES, tq), 1)) // SLC_LEN
    forced = (blk == 0) | (blk == cur) | (blk == cur - 1)
    score = jnp.where(blk <= cur, imp_t + jnp.where(forced, FORCE_BONUS, 0.0), NEG_INF)
    groups = [score[8 * v:8 * v + 8] for v in range(-(-n_sb // 8))]
    ranks = [jnp.zeros((8, tq), F32) for _ in groups]
    sub = lax.broadcasted_iota(I32, (8, tq), 0)
    for i in range(n_sb):
        ci = score[i:i + 1, :]
        for v, sv in enumerate(groups):
            ge = jnp.where(ci >= sv, 1.0, 0.0)
            gt = jnp.where(ci > sv, 1.0, 0.0)
            if 8 * v > i:
                ranks[v] = ranks[v] + ge
            elif 8 * v + 7 < i:
                ranks[v] = ranks[v] + gt
            else:
                ranks[v] = ranks[v] + jnp.where(sub + 8 * v > i, ge, gt)
    rank = jnp.concatenate(ranks + [jnp.full((LANES - 8 * len(groups), tq), float(LANES), F32)], axis=0)
    sel = jnp.where(rank < float(min(SLC_TOP, n_sb)), jnp.where(blk <= cur, 1.0, 0.0), 0.0).astype(BF16)

    def expand(c, _):
        cs = pl.multiple_of(c * tw, tw)
        mask_ref[pl.ds(cs, tw), :] = (_dot(exp_ref[pl.ds(cs, tw), :], sel) - 1.0) * (-NEG_INF)
        return 0

    lax.fori_loop(0, qi // 2 + 1, expand, 0)

    def step(k_ref_, vt_ref_, ks, width, add, carry):
        m, acc = carry
        s = _dot(k_ref_[0, pl.ds(ks, width), :A_DH], q_all) + add
        m_new = jnp.maximum(m, jnp.max(s, axis=0, keepdims=True))
        p = jnp.exp2(s - m_new)
        return m_new, jnp.exp2(m - m_new) * acc + _dot(vt_ref_[0, 0, :, pl.ds(ks, width)], p.astype(BF16))

    init = jnp.full((1, rr * tq), -jnp.inf, F32), jnp.zeros((vst_ref.shape[2], rr * tq), F32)
    key_i = lax.broadcasted_iota(I32, (tw, tq), 0)

    def slc_wide(ks, limit, carry):
        own = jnp.where(key_i < limit, mask_ref[pl.ds(ks, tw), :], NEG_INF)
        return step(ks_ref, vst_ref, ks, tw, heads(own), carry)

    def slc_near(ks, which, carry):
        return step(ks_ref, vst_ref, ks, tw, heads(mask_ref[pl.ds(ks, tw), :]) + near_ref[0, which], carry)

    _, acc_s = _key_loops(qi, tq, 2, slc_wide, slc_near, init)

    def win_oldest():
        ks = pl.multiple_of((qi - WIN // tq) * tq, tq)
        kq = lax.broadcasted_iota(I32, (tq, tq), 0)
        qq = lax.broadcasted_iota(I32, (tq, tq), 1)
        return step(kw_ref, vwt_ref, ks, tq, heads(jnp.where(qq < kq, 0.0, NEG_INF)), init)

    carry_w = lax.cond(qi >= WIN // tq, win_oldest, lambda: init)
    far = pl.multiple_of(jnp.maximum(qi - 1, 0) * tq, tq)
    _, acc_w = step(kw_ref, vwt_ref, far, tw, near_ref[0, jnp.where(qi == 0, 1, 0)], carry_w)

    gs = _sigmoid(ga_ref[0, 0])
    for r in range(rr):
        a_s, a_w = head(acc_s, r), head(acc_w, r)
        o = (gs[3 * r:3 * r + 1] * head(o_c, r)[:A_DH]
             + gs[3 * r + 1:3 * r + 2] * (a_s[:A_DH] / a_s[A_DH:A_DH + 1])
             + gs[3 * r + 2:3 * r + 3] * (a_w[:A_DH] / a_w[A_DH:A_DH + 1]))
        o_ref[0, r * A_DH:(r + 1) * A_DH, :] = o.astype(BF16)


def _nsat(q, kvc, kvsw, ga, tab, b, s):
    tq = LANE_TQ
    nq = s // tq
    ncp = s // CMP_STRIDE
    n_sb = s // SLC_LEN
    n_vt = A_DH + 16
    assert n_sb <= LANES and WIN == 2 * tq and s % (2 * tq) == 0
    qt = jnp.swapaxes(q, 1, 2)
    vct = jnp.swapaxes(kvc.reshape(b, ncp, 2 * A_KV, LANES)[:, :, A_KV:, :n_vt], 1, 3).swapaxes(1, 2)
    vt = jnp.swapaxes(kvsw.reshape(b, s, 4 * A_KV, LANES)[..., :n_vt], 1, 3).swapaxes(1, 2)
    gat = jnp.swapaxes(ga.reshape(b, s, A_KV, LANES)[..., :16], 1, 3).swapaxes(1, 2)
    t = (np.arange(nq)[:, None, None] * tq + np.arange(tq)[None, None, :])
    cmp_end = (np.arange(ncp) * CMP_STRIDE + CMP_LEN - 1)[None, :, None]
    bias_c = jnp.stack([_rel_bias(tab[:, g * A_GRP:(g + 1) * A_GRP], t - cmp_end, cmp_end <= t, False,
                                  head_axis=2).reshape(nq, ncp, A_GRP * tq) for g in range(A_KV)])
    near = _near_tiles(tab, tq).reshape(A_KV, A_GRP, 2, tq, 2 * tq).transpose(0, 2, 4, 1, 3)
    near = near.reshape(A_KV, 2, 2 * tq, A_GRP * tq)
    cmp_start = np.arange(ncp)[None, :] * CMP_STRIDE
    sb_start = np.arange(LANES)[:, None] * SLC_LEN
    ovl = ((cmp_start < sb_start + SLC_LEN) & (cmp_start + CMP_LEN - 1 >= sb_start)
           & (np.arange(ncp)[None, :] < ncp - 1) & (np.arange(LANES)[:, None] < n_sb))
    ovl = jnp.asarray(ovl.astype(np.float32))
    expand = jnp.asarray((np.arange(s)[:, None] // SLC_LEN == np.arange(LANES)[None, :]).astype(np.float32), BF16)
    k_spec = lambda blk: pl.BlockSpec((1, s, LANES), lambda i, g, j: (i, 0, 2 * blk + g))
    vt_spec = lambda blk: pl.BlockSpec((1, 1, n_vt, s), lambda i, g, j: (i, 2 * blk + g, 0, 0))
    o_t = pl.pallas_call(
        functools.partial(_nsat_body, tq=tq, seq=s, n_sb=n_sb),
        grid=(b, A_KV, nq),
        in_specs=[pl.BlockSpec((1, A_GRP * A_DH, tq), lambda i, g, j: (i, g, j)),
                  pl.BlockSpec((1, ncp, LANES), lambda i, g, j: (i, 0, g)),
                  pl.BlockSpec((1, 1, n_vt, ncp), lambda i, g, j: (i, g, 0, 0)),
                  pl.BlockSpec((1, 1, ncp, A_GRP * tq), lambda i, g, j: (g, j, 0, 0)),
                  k_spec(0), vt_spec(1), k_spec(2), vt_spec(3),
                  pl.BlockSpec((1, 1, 16, tq), lambda i, g, j: (i, g, 0, j)),
                  pl.BlockSpec((1, 2, 2 * tq, A_GRP * tq), lambda i, g, j: (g, 0, 0, 0)),
                  _const_spec((LANES, ncp)), _const_spec((s, LANES))],
        out_specs=pl.BlockSpec((1, A_GRP * A_DH, tq), lambda i, g, j: (i, g, j)),
        out_shape=jax.ShapeDtypeStruct((b, A_HEADS * A_DH, s), BF16),
        scratch_shapes=[pltpu.VMEM((s, tq), F32)],
        compiler_params=_params("arbitrary", "arbitrary", "arbitrary"),
        name="nsa_attn",
    )(qt, kvc, vct, bias_c, kvsw, vt, kvsw, vt, gat, near, ovl, expand)
    return jnp.swapaxes(o_t, 1, 2)


def _dsat_body(qt_ref, k_ref, vt_ref, iqt_ref, ik_ref, iwt_ref, bias_ref, o_ref, keys_ref, *,
               tq, seq, k_top):
    qi = pl.program_id(1)
    nh = B_HEADS
    tw = 2 * tq
    n_wide = qi // 2 + 1
    key_i = lax.broadcasted_iota(I32, (tw, tq), 0)
    qry_i = lax.broadcasted_iota(I32, (tw, tq), 1)

    iwt = iwt_ref[0]

    def scores(kw, _):
        ks = pl.multiple_of(kw * tw, tw)
        ik = ik_ref[0, pl.ds(ks, tw), :IDX_DH]
        sc = jnp.zeros((tw, tq), F32)
        for h in range(IDX_HEADS):
            lg = _dot(ik, iqt_ref[0, h * IDX_DH:(h + 1) * IDX_DH, :])
            sc = sc + jnp.maximum(lg, 0.0) * iwt[h:h + 1, :]
        bits = lax.bitcast_convert_type(sc, I32)
        key = jnp.where(bits < 0, bits ^ 0x7FFFFFFF, bits)
        vis = (key_i + ks) <= (qry_i + qi * tq)
        keys_ref[pl.ds(ks, tw), :] = jnp.where(vis, key, INT_MIN)
        return 0

    lax.fori_loop(0, n_wide, scores, 0)

    def count(ind):
        def body(kw, acc):
            ks = pl.multiple_of(kw * tw, tw)
            x = ind(keys_ref[pl.ds(ks, tw), :], kw)
            return acc + jnp.sum(x.reshape(tw // ACC_ROWS, ACC_ROWS, tq), axis=0)
        acc = lax.fori_loop(0, n_wide, body, jnp.zeros((ACC_ROWS, tq), F32))
        return jnp.sum(acc, axis=0, keepdims=True)

    kf = float(k_top)
    c0 = count(lambda t, kw: jnp.where(t >= 0, 1.0, 0.0))
    thr = jnp.where(c0 >= kf, 0, INT_MIN).astype(I32)
    c_ge = jnp.where(c0 >= kf, c0, float(seq))

    def bisect(i, state):
        thr, c_ge = state
        cand = thr | (jnp.int32(1) << (30 - i))
        c = count(lambda t, kw: jnp.where(t >= cand, 1.0, 0.0))
        return jnp.where(c >= kf, cand, thr), jnp.where(c >= kf, c, c_ge)

    thr, c_ge = lax.fori_loop(0, 31, bisect, (thr, c_ge))

    n_bits = int(math.log2(seq))
    assert 2 ** n_bits == seq

    def cutoff():
        need = kf - count(lambda t, kw: jnp.where(t > thr, 1.0, 0.0))

        def step(i, cut):
            cand = cut | (jnp.int32(1) << (n_bits - 1 - i))
            c = count(lambda t, kw: jnp.where(t == thr, jnp.where(key_i + kw * tw < cand, 1.0, 0.0), 0.0))
            return jnp.where(c < need, cand, cut)
        return lax.fori_loop(0, n_bits, step, jnp.zeros((1, tq), I32))

    excess = jnp.max(jnp.where(c_ge > kf, 1.0, 0.0)) > 0.5
    cut = lax.cond(excess, cutoff, lambda: jnp.full((1, tq), seq, I32))

    q_all = jnp.concatenate([qt_ref[0, h * B_DH:(h + 1) * B_DH, :] for h in range(nh)], axis=1)

    def step(ks, own, bias, carry):
        m, acc = carry
        key = keys_ref[pl.ds(ks, tw), :]
        tie = jnp.where(key_i + ks <= cut, 0.0, NEG_INF)
        sel = jnp.where(key > thr, 0.0, jnp.where(key == thr, tie, NEG_INF))
        if own is not None:
            sel = jnp.where(own, sel, NEG_INF)
        s = _dot(k_ref[0, pl.ds(ks, tw), :B_DH], q_all) + jnp.concatenate([sel] * nh, axis=1)
        if bias is not None:
            s = s + bias
        m_new = jnp.maximum(m, jnp.max(s, axis=0, keepdims=True))
        p = jnp.exp2(s - m_new)
        return m_new, jnp.exp2(m - m_new) * acc + _dot(vt_ref[0, :, pl.ds(ks, tw)], p.astype(BF16))

    def wide_step(ks, limit, carry):
        return step(ks, key_i < limit, None, carry)

    def near_step(ks, which, carry):
        return step(ks, None, bias_ref[which], carry)

    init = jnp.full((1, nh * tq), -jnp.inf, F32), jnp.zeros((vt_ref.shape[1], nh * tq), F32)
    _, acc = _key_loops(qi, tq, 2, wide_step, near_step, init)
    for h in range(nh):
        a = acc[:, h * tq:(h + 1) * tq]
        o_ref[0, h * B_DH:(h + 1) * B_DH, :] = (a[:B_DH] / a[B_DH:B_DH + 1]).astype(BF16)


def _dsat(q, kbvb, iq, ik, iw, tab, b, s):
    tq = LANE_TQ
    k_top = min(IDX_TOPK_MAX, s // 4)
    assert s % (2 * tq) == 0
    qt = jnp.swapaxes(q, 1, 2)
    iqt = jnp.swapaxes(iq, 1, 2)
    iwt = jnp.swapaxes(iw[:, :, :IDX_HEADS], 1, 2)
    n_vt = B_DH + 16
    vt = jnp.swapaxes(kbvb[:, :, LANES:LANES + n_vt], 1, 2)
    near = _near_tiles(tab, tq).transpose(1, 3, 0, 2).reshape(2, 2 * tq, B_HEADS * tq)
    o_t = pl.pallas_call(
        functools.partial(_dsat_body, tq=tq, seq=s, k_top=k_top),
        grid=(b, s // tq),
        in_specs=[pl.BlockSpec((1, B_HEADS * B_DH, tq), lambda i, j: (i, 0, j)),
                  pl.BlockSpec((1, s, LANES), lambda i, j: (i, 0, 0)),
                  pl.BlockSpec((1, n_vt, s), lambda i, j: (i, 0, 0)),
                  pl.BlockSpec((1, IDX_HEADS * IDX_DH, tq), lambda i, j: (i, 0, j)),
                  pl.BlockSpec((1, s, LANES), lambda i, j: (i, 0, 0)),
                  pl.BlockSpec((1, IDX_HEADS, tq), lambda i, j: (i, 0, j)),
                  _const_spec((2, 2 * tq, B_HEADS * tq))],
        out_specs=pl.BlockSpec((1, B_HEADS * B_DH, tq), lambda i, j: (i, 0, j)),
        out_shape=jax.ShapeDtypeStruct((b, B_HEADS * B_DH, s), BF16),
        scratch_shapes=[pltpu.VMEM((s, tq), I32)],
        compiler_params=_params("arbitrary", "arbitrary"),
        name="dsa_attn",
    )(qt, kbvb, vt, iqt, ik, iwt, near)
    return jnp.swapaxes(o_t, 1, 2)


def _mla_prep_body(cq_ref, ckv_ref, kr_ref, qn_ref, kvn_ref, wq1_ref, wq2_ref, wk_ref, wv_ref,
                   cos_ref, sin_ref, ones_ref, q_o, k_o, v_o):
    cqn = _rms(cq_ref[...], qn_ref[...]).astype(BF16)
    ckn = _rms(ckv_ref[...], kvn_ref[...]).astype(BF16)
    cos, sin = cos_ref[...], sin_ref[...]
    cos_h = jnp.concatenate([cos] * C_HEADS, axis=1)
    sin_h = jnp.concatenate([sin] * C_HEADS, axis=1)
    q_o[...] = (_dot(cqn, wq1_ref[...]) * cos_h + _dot(cqn, wq2_ref[...]) * sin_h).astype(BF16)
    kr = kr_ref[...]
    kr_rot = kr[:, :LANES] * cos + kr[:, LANES:] * sin
    k_o[...] = (_dot(ckn, wk_ref[...]) + jnp.concatenate([kr_rot] * C_HEADS, axis=1)).astype(BF16)
    v_o[...] = (_dot(ckn, wv_ref[...]) + ones_ref[...]).astype(BF16)


def _rot_cols(w):
    half = C_DR // 2
    return jnp.concatenate([-w[..., half:], w[..., :half]], axis=-1)


def _mla_prep(cq, ckv, kr, q_norm, kv_norm, w_uq, w_ukv, s):
    n = cq.shape[0]
    hw = C_HEADS * LANES
    padq = LANES - C_DN - C_DR
    wq = w_uq.reshape(C_Q_RANK, C_HEADS, C_DN + C_DR) * ((C_DN + C_DR) ** -0.5 * LOG2E)
    wq1 = jnp.pad(wq, ((0, 0), (0, 0), (0, padq))).reshape(C_Q_RANK, hw)
    wq2 = jnp.pad(_rot_cols(wq[..., C_DN:]), ((0, 0), (0, 0), (C_DN, padq))).reshape(C_Q_RANK, hw)
    wkv = w_ukv.reshape(C_KV_RANK, C_HEADS, C_DN + C_DV)
    wk = jnp.pad(wkv[..., :C_DN], ((0, 0), (0, 0), (0, LANES - C_DN))).reshape(C_KV_RANK, hw)
    wv = jnp.pad(wkv[..., C_DN:], ((0, 0), (0, 0), (0, LANES - C_DV))).reshape(C_KV_RANK, hw)
    half = C_DR // 2
    inv = ROPE_THETA ** (-jnp.arange(half, dtype=F32) / half)
    ang = jnp.arange(s, dtype=F32)[:, None] * inv[None, :]
    cos, sin = jnp.cos(ang), jnp.sin(ang)
    cos_t = jnp.concatenate([jnp.ones((s, C_DN), F32), cos, cos, jnp.zeros((s, padq), F32)], axis=1)
    sin_t = jnp.concatenate([jnp.zeros((s, C_DN), F32), sin, sin, jnp.zeros((s, padq), F32)], axis=1)
    ones = jnp.tile(jnp.asarray(np.arange(LANES) == ONES_LANE, F32), C_HEADS).reshape(1, hw)
    tile_tok = lambda w_: pl.BlockSpec((TOK_TILE, w_), lambda i: (i, 0))
    n_pos = s // TOK_TILE
    pos_spec = pl.BlockSpec((TOK_TILE, LANES), lambda i: (i % n_pos, 0))
    return pl.pallas_call(
        _mla_prep_body,
        grid=(n // TOK_TILE,),
        in_specs=[tile_tok(C_Q_RANK), tile_tok(C_KV_RANK), tile_tok(2 * LANES),
                  _const_spec((1, C_Q_RANK)), _const_spec((1, C_KV_RANK)),
                  _const_spec((C_Q_RANK, hw)), _const_spec((C_Q_RANK, hw)),
                  _const_spec((C_KV_RANK, hw)), _const_spec((C_KV_RANK, hw)),
                  pos_spec, pos_spec, _const_spec((1, hw))],
        out_specs=[tile_tok(hw)] * 3,
        out_shape=[jax.ShapeDtypeStruct((n, hw), BF16)] * 3,
        compiler_params=_params("arbitrary"),
        name="mla_prep",
    )(cq, ckv, kr, q_norm.reshape(1, -1), kv_norm.reshape(1, -1), wq1.astype(BF16), wq2.astype(BF16),
      wk.astype(BF16), wv.astype(BF16), cos_t, sin_t, ones)


def _mla_attn_body(q_ref, k_ref, v_ref, o_ref, *, tq, nh):
    qt = pl.program_id(2)
    q = q_ref[0]
    lanes = lambda a, h: a[:, h * LANES:(h + 1) * LANES]

    def step(kt, carry, add):
        ks = pl.multiple_of(kt * tq, tq)
        k = k_ref[0, pl.ds(ks, tq), :]
        v = v_ref[0, pl.ds(ks, tq), :]
        return tuple(_online_step(lanes(q, h), lanes(k, h), lanes(v, h), carry[h], add=add)
                     for h in range(nh))

    init = tuple(_flash_init(tq, LANES) for _ in range(nh))
    carry = lax.fori_loop(0, qt, lambda kt, c: step(kt, c, None), init)
    row = lax.broadcasted_iota(I32, (tq, tq), 0)
    col = lax.broadcasted_iota(I32, (tq, tq), 1)
    carry = step(qt, carry, jnp.where(row >= col, 0.0, NEG_INF))
    for h in range(nh):
        acc = carry[h][1]
        o_ref[0, :, h * LANES:(h + 1) * LANES] = (acc / acc[:, ONES_LANE:ONES_LANE + 1]).astype(BF16)


def _mla_attn(q, k, v, b, s):
    tq = DENSE_TQ
    nh = MLA_HEADS_PER_STEP
    spec_q = pl.BlockSpec((1, tq, nh * LANES), lambda i, h, j: (i, j, h))
    spec_kv = pl.BlockSpec((1, s, nh * LANES), lambda i, h, j: (i, 0, h))
    return pl.pallas_call(
        functools.partial(_mla_attn_body, tq=tq, nh=nh),
        grid=(b, C_HEADS // nh, s // tq),
        in_specs=[spec_q, spec_kv, spec_kv],
        out_specs=spec_q,
        out_shape=jax.ShapeDtypeStruct((b, s, C_HEADS * LANES), BF16),
        compiler_params=_params("arbitrary", "arbitrary", "arbitrary"),
        name="mla_attn",
    )(q, k, v)


def _diff_body(lam_ref, q_ref, k_ref, v_ref, bias_ref, g_ref, o_ref, *, tq, nh, out_scale):
    qt = pl.program_id(2)
    q = q_ref[0]
    lanes = lambda a, c, w=LANES: a[:, c * w:(c + 1) * w]

    def step(kt, carry, near):
        ks = pl.multiple_of(kt * tq, tq)
        k = k_ref[0, pl.ds(ks, tq), :]
        v = v_ref[0, pl.ds(ks, tq), :]
        return tuple(_online_step(lanes(q, c), lanes(k, c), lanes(v, c // 2, 2 * LANES), carry[c],
                                  add=bias_ref[c // 2, qt - kt] if near else None)
                     for c in range(2 * nh))

    n_far = jnp.maximum(qt - 1, 0)
    init = tuple(_flash_init(tq, 2 * LANES) for _ in range(2 * nh))
    carry = lax.fori_loop(0, n_far, functools.partial(step, near=False), init)
    carry = lax.fori_loop(n_far, qt + 1, functools.partial(step, near=True), carry)
    outs = [carry[c][1][:, :D_DV] / carry[c][1][:, D_DV:D_DV + 1] for c in range(2 * nh)]
    for h in range(nh):
        o = outs[2 * h] - lam_ref[0] * outs[2 * h + 1]
        o_ref[0, :, h * D_DV:(h + 1) * D_DV] = (_rms(o, g_ref[...]) * out_scale).astype(BF16)


def _diff_attn(q, k, v, tab, lam_full, subln, lam_init, b, s):
    tq = DENSE_TQ
    nh = DIFF_HEADS_PER_STEP
    half = _bias_tiles(tab, tq // 2)
    t0, t1, zero = half[:, 0], half[:, 1], half[:, 2]
    hidden = jnp.full_like(zero, NEG_INF)
    block = lambda a, b_, c, d: jnp.concatenate([jnp.concatenate([a, b_], axis=-1),
                                                 jnp.concatenate([c, d], axis=-1)], axis=-2)
    tiles = jnp.stack([block(t0, hidden, t1, t0), block(zero, t1, zero, zero)], axis=1)
    spec_q = pl.BlockSpec((1, tq, nh * 2 * LANES), lambda i, h, j: (i, j, h))
    spec_kv = pl.BlockSpec((1, s, nh * 2 * LANES), lambda i, h, j: (i, 0, h))
    return pl.pallas_call(
        functools.partial(_diff_body, tq=tq, nh=nh, out_scale=1.0 - lam_init),
        grid=(b, D_HEADS // nh, s // tq),
        in_specs=[pl.BlockSpec(memory_space=pltpu.SMEM),
                  spec_q, spec_kv, spec_kv,
                  pl.BlockSpec((nh, 2, tq, tq), lambda i, h, j: (h, 0, 0, 0)),
                  _const_spec((1, D_DV))],
        out_specs=pl.BlockSpec((1, tq, nh * D_DV), lambda i, h, j: (i, j, h)),
        out_shape=jax.ShapeDtypeStruct((b, s, D_HEADS * D_DV), BF16),
        compiler_params=_params("arbitrary", "arbitrary", "arbitrary"),
        name="diff_attn",
    )(lam_full.reshape(1), q, k, v, tiles, subln.reshape(1, D_DV))


def _pad_blocks(w, n, width, to=LANES, at=0):
    k = w.shape[0]
    return jnp.pad(w.reshape(k, n, width), ((0, 0), (0, 0), (at, to - width - at))).reshape(k, n * to)


def _ones_row(n_blocks, which, to=LANES, lane=ONES_LANE):
    row = np.zeros((n_blocks, to), np.float32)
    row[list(which), lane] = 1.0
    return jnp.asarray(row.reshape(1, n_blocks * to))


def _split(w, sizes):
    return jnp.split(w, [int(c) for c in np.cumsum(sizes)[:-1]], axis=-1)


def _pad_rows(w, n, width):
    d = w.shape[1]
    return jnp.pad(w.reshape(n, width, d), ((0, 0), (0, LANES - width), (0, 0))).reshape(n * LANES, d)


def _even_mixer(x2, g_pre, g_post, w_in, w_out, cmp_pe, cmp_w, rel_tab, b, s):
    qa, kva, ga, qb, kb, vb, iq, ik, iw = _split(w_in, EV_SIZES)
    half = 2 * A_KV * A_DH
    w_segs = [qa * (A_DH ** -0.5 * LOG2E), kva[:, :half],
              _pad_blocks(kva[:, half:], 4 * A_KV, A_DH), _pad_blocks(ga, A_KV, 3 * A_GRP),
              qb * (B_DH ** -0.5 * LOG2E),
              _pad_blocks(jnp.concatenate([kb, vb], axis=1), 2, B_DH),
              iq, _pad_blocks(ik, 1, IDX_DH), _pad_blocks(iw, 1, IDX_HEADS)]
    zeros = lambda w: jnp.zeros((1, w.shape[1]), F32)
    c_segs = [zeros(w) for w in w_segs]
    c_segs[2] = _ones_row(4 * A_KV, (2, 3, 6, 7))
    c_segs[5] = _ones_row(2, (1,))
    dts = [BF16, F32, BF16, F32, BF16, BF16, BF16, BF16, F32]
    q_a, kcvc, kvsw, g_a, q_b, kbvb, i_q, i_k, i_w = _proj(x2, g_pre, w_segs, c_segs, dts)
    r3 = lambda a: a.reshape(b, s, a.shape[-1])
    kvc = _nsa_compress(kcvc, cmp_pe, cmp_w, b, s)
    o_a = _nsat(r3(q_a), kvc, r3(kvsw), r3(g_a), rel_tab[:, :A_HEADS], b, s)
    o_b = _dsat(r3(q_b), r3(kbvb), r3(i_q), r3(i_k), r3(i_w), rel_tab[:, A_HEADS:A_HEADS + B_HEADS], b, s)
    na = A_HEADS * A_DH
    return _out_proj(o_a.reshape(b * s, -1), o_b.reshape(b * s, -1), w_out[:na], w_out[na:], x2, g_post)


def _odd_mixer(x2, g_pre, g_post, w_in, w_out, q_norm, kv_norm, w_uq, w_ukv, lam, subln, rel_tab,
               lam_init, b, s):
    c_q, c_kv, k_rope, qd, kd, vd = _split(w_in, OD_SIZES)
    w_kr = jnp.concatenate([_pad_blocks(k_rope, 1, C_DR, at=C_DN),
                            _pad_blocks(_rot_cols(k_rope), 1, C_DR, at=C_DN)], axis=1)
    w_segs = [c_q, c_kv, w_kr, _pad_blocks(qd * (D_DH ** -0.5 * LOG2E), 2 * D_HEADS, D_DH),
              _pad_blocks(kd, 2 * D_HEADS, D_DH), _pad_blocks(vd, D_HEADS, D_DV, to=2 * LANES)]
    c_segs = [jnp.zeros((1, w.shape[1]), F32) for w in w_segs]
    c_segs[5] = _ones_row(D_HEADS, range(D_HEADS), to=2 * LANES, lane=D_DV)
    dts = [F32, F32, F32, BF16, BF16, BF16]
    cq, ckv, kr, q_d, k_d, v_d = _proj(x2, g_pre, w_segs, c_segs, dts)
    q_m, k_m, v_m = _mla_prep(cq, ckv, kr, q_norm, kv_norm, w_uq, w_ukv, s)
    r3 = lambda a: a.reshape(b, s, a.shape[-1])
    o_c = _mla_attn(r3(q_m), r3(k_m), r3(v_m), b, s)
    lam32 = lam.astype(F32)
    lam_full = jnp.exp(jnp.sum(lam32[0] * lam32[1])) - jnp.exp(jnp.sum(lam32[2] * lam32[3])) + lam_init
    o_d = _diff_attn(r3(q_d), r3(k_d), r3(v_d), rel_tab[:, A_HEADS + B_HEADS:], lam_full, subln,
                     lam_init, b, s)
    nc = C_HEADS * C_DV
    wc = _pad_rows(w_out[:nc], C_HEADS, C_DV)
    return _out_proj(o_c.reshape(b * s, -1), o_d.reshape(b * s, -1), wc, w_out[nc:], x2, g_post)


def kernel(x, mem, rel_bias_table, norm_pre, norm_post, mem_norm, ffn_wg, ffn_wu, ffn_wd, ev_w_in,
           ev_cmp_pe, ev_cmp_w, ev_w_out, od_w_in, od_q_norm, od_kv_norm, od_w_uq, od_w_ukv, od_lambda,
           od_subln, od_w_out, xa_wq, xa_wkv, xa_wo):
    b, s, d = x.shape
    depth = norm_pre.shape[0]
    x2 = x.reshape(b * s, d)
    mem2 = mem.reshape(-1, d)
    for i in range(depth):
        g_pre, g_post = norm_pre[i], norm_post[i]
        x2 = _ffn(x2, g_pre[0], g_post[0], ffn_wg[i, 0], ffn_wu[i, 0], ffn_wd[i, 0])
        if i % 2 == 0:
            e = i // 2
            x2 = _even_mixer(x2, g_pre[1], g_post[1], ev_w_in[e], ev_w_out[e], ev_cmp_pe[e], ev_cmp_w[e],
                             rel_bias_table, b, s)
        else:
            o = i // 2
            lam_init = 0.8 - 0.6 * math.exp(-0.3 * i)
            x2 = _odd_mixer(x2, g_pre[1], g_post[1], od_w_in[o], od_w_out[o], od_q_norm[o], od_kv_norm[o],
                            od_w_uq[o], od_w_ukv[o], od_lambda[o], od_subln[o], rel_bias_table, lam_init, b, s)
        (kv,) = _proj(mem2, mem_norm[i], [xa_wkv[i]], [jnp.zeros((1, xa_wkv.shape[-1]), F32)], [BF16])
        x3 = _mem_xattn(x2.reshape(b, s, d), kv.reshape(b, -1, kv.shape[-1]), g_pre[2], g_post[2],
                        xa_wq[i], xa_wo[i])
        x2 = _ffn(x3.reshape(b * s, d), g_pre[3], g_post[3], ffn_wg[i, 1], ffn_wu[i, 1], ffn_wd[i, 1])
    return x2.reshape(b, s, d)
```

```python
import functools
import math

import numpy as np
import jax
import jax.numpy as jnp
from jax import lax
from jax.experimental import pallas as pl
from jax.experimental.pallas import tpu as pltpu

F32, BF16, I32 = jnp.float32, jnp.bfloat16, jnp.int32

N_BUCKETS = 32
MAX_DISTANCE = 128
RMS_EPS = 1e-6
ROPE_THETA = 10000.0
NEG_INF = -1e30
A_HEADS, A_KV, A_DH = 8, 2, 64
A_GRP = A_HEADS // A_KV
CMP_LEN, CMP_STRIDE, SLC_LEN, SLC_TOP, WIN = 32, 16, 64, 16, 512
FORCE_BONUS = 1e3
B_HEADS, B_DH, IDX_HEADS, IDX_DH, IDX_TOPK_MAX = 8, 64, 8, 32, 256
C_HEADS, C_Q_RANK, C_KV_RANK, C_DN, C_DR, C_DV = 8, 256, 128, 64, 32, 64
D_HEADS, D_DH, D_DV = 4, 64, 128
X_HEADS, X_DH = 4, 128
EV_SIZES = (A_HEADS * A_DH, 6 * A_KV * A_DH, 3 * A_HEADS,
            B_HEADS * B_DH, B_DH, B_DH, IDX_HEADS * IDX_DH, IDX_DH, IDX_HEADS)
OD_SIZES = (C_Q_RANK, C_KV_RANK, C_DR, D_HEADS * 2 * D_DH, D_HEADS * 2 * D_DH, D_HEADS * D_DV)

LANES = 128
VMEM_LIMIT_BYTES = 56 * 1024 * 1024
ONES_LANE = 64
INT_MIN = -2 ** 31
LOG2E = 1.4426950408889634

TOK_TILE = 512
FFN_CHUNK = 256
LANE_TQ = 256
ACC_ROWS = 32
DENSE_TQ = 512
MLA_HEADS_PER_STEP = 4
DIFF_HEADS_PER_STEP = 2


def _params(*sem):
    return pltpu.CompilerParams(dimension_semantics=sem, vmem_limit_bytes=VMEM_LIMIT_BYTES)


def _const_spec(shape):
    nd = len(shape)
    return pl.BlockSpec(shape, lambda *_: (0,) * nd, pipeline_mode=pl.Buffered(1))


def _dot(a, b):
    return jnp.dot(a, b, preferred_element_type=F32)


def _dot_t(a, b):
    return lax.dot_general(a, b, (((1,), (1,)), ((), ())), preferred_element_type=F32)


def _rms(x, g):
    return x * lax.rsqrt(jnp.mean(x * x, axis=-1, keepdims=True) + RMS_EPS) * g


def _sigmoid(x):
    return 1.0 / (1.0 + jnp.exp(-x))


def _online_step(q, k, v, carry, add=None):
    m, acc = carry
    s = _dot_t(q, k)
    if add is not None:
        s = s + add
    m_new = jnp.maximum(m, jnp.max(s, axis=-1, keepdims=True))
    p = jnp.exp2(s - m_new)
    acc = jnp.exp2(m - m_new) * acc + _dot(p.astype(BF16), v)
    return m_new, acc


def _flash_init(m_rows, n_lanes):
    return jnp.full((m_rows, 1), -jnp.inf, F32), jnp.zeros((m_rows, n_lanes), F32)


def _ffn_body(x_ref, gpre_ref, gpost_ref, wgu_ref, wd_ref, o_ref, *, n_chunks, fc):
    x = x_ref[...]
    h = _rms(x, gpre_ref[...]).astype(BF16)
    acc = jnp.zeros(x.shape, F32)
    for c in range(n_chunks):
        gu = _dot(h, wgu_ref[:, c * 2 * fc:(c + 1) * 2 * fc])
        g, u = gu[:, :fc], gu[:, fc:]
        a = g * _sigmoid(g) * u
        acc = acc + _dot(a.astype(BF16), wd_ref[c * fc:(c + 1) * fc, :])
    o_ref[...] = x + 0.5 * _rms(acc, gpost_ref[...])


def _ffn(x, g_pre, g_post, wg, wu, wd):
    n, d = x.shape
    f = wg.shape[1]
    fc = FFN_CHUNK
    n_chunks = f // fc
    assert n_chunks * fc == f and n % TOK_TILE == 0
    wgu = jnp.concatenate([wg.reshape(d, n_chunks, fc), wu.reshape(d, n_chunks, fc)], axis=-1)
    wgu = wgu.reshape(d, 2 * f).astype(BF16)
    return pl.pallas_call(
        functools.partial(_ffn_body, n_chunks=n_chunks, fc=fc),
        grid=(n // TOK_TILE,),
        in_specs=[pl.BlockSpec((TOK_TILE, d), lambda i: (i, 0)),
                  _const_spec((1, d)), _const_spec((1, d)),
                  _const_spec((d, 2 * f)), _const_spec((f, d))],
        out_specs=pl.BlockSpec((TOK_TILE, d), lambda i: (i, 0)),
        out_shape=jax.ShapeDtypeStruct((n, d), F32),
        compiler_params=_params("arbitrary"),
        name="ffn",
    )(x, g_pre.reshape(1, d), g_post.reshape(1, d), wgu, wd.astype(BF16))


def _proj_body(x_ref, g_ref, w_ref, c_ref, *o_refs, bounds):
    h = _rms(x_ref[...], g_ref[...]).astype(BF16)
    for o_ref, (a, b) in zip(o_refs, bounds):
        o_ref[...] = (_dot(h, w_ref[:, a:b]) + c_ref[:, a:b]).astype(o_ref.dtype)


def _proj(x, g, w_segs, c_segs, dtypes):
    n, d = x.shape
    widths = [w.shape[1] for w in w_segs]
    assert all(wd_ % LANES == 0 for wd_ in widths) and n % TOK_TILE == 0
    offs = np.concatenate([[0], np.cumsum(widths)])
    bounds = tuple((int(offs[i]), int(offs[i + 1])) for i in range(len(widths)))
    w = jnp.concatenate(w_segs, axis=1).astype(BF16)
    c = jnp.concatenate(c_segs, axis=1).astype(F32)
    tot = int(offs[-1])
    return pl.pallas_call(
        functools.partial(_proj_body, bounds=bounds),
        grid=(n // TOK_TILE,),
        in_specs=[pl.BlockSpec((TOK_TILE, d), lambda i: (i, 0)),
                  _const_spec((1, d)), _const_spec((d, tot)), _const_spec((1, tot))],
        out_specs=[pl.BlockSpec((TOK_TILE, wd_), lambda i: (i, 0)) for wd_ in widths],
        out_shape=[jax.ShapeDtypeStruct((n, wd_), dt) for wd_, dt in zip(widths, dtypes)],
        compiler_params=_params("arbitrary"),
        name="norm_proj",
    )(x, g.reshape(1, d), w, c)


def _out_xattn_body(a_ref, b_ref, wa_ref, wb_ref, x_ref, gmix_ref, gpre_ref, gpost_ref, wq_ref, kv_ref,
                    wo_ref, o_ref, *, heads, dh):
    y = _dot(a_ref[0], wa_ref[...]) + _dot(b_ref[0], wb_ref[...])
    x = x_ref[0] + _rms(y, gmix_ref[...])
    h = _rms(x, gpre_ref[...]).astype(BF16)
    q = _dot(h, wq_ref[...]).astype(BF16)
    kv = kv_ref[0]
    scale = dh ** -0.5
    outs = []
    for hh in range(heads):
        k = kv[:, hh * dh:(hh + 1) * dh]
        v = kv[:, (heads + hh) * dh:(heads + hh + 1) * dh]
        s = _dot_t(q[:, hh * dh:(hh + 1) * dh], k) * scale
        p = jnp.exp(s - jnp.max(s, axis=-1, keepdims=True))
        p = p / jnp.sum(p, axis=-1, keepdims=True)
        outs.append(_dot(p.astype(BF16), v).astype(BF16))
    y = _dot(jnp.concatenate(outs, axis=-1), wo_ref[...])
    o_ref[0] = x + _rms(y, gpost_ref[...])


def _out_xattn(oa, ob, wa, wb, x, g_mix, kv, g_pre, g_post, wq, wo):
    b, s, d = x.shape
    ka, kb = oa.shape[2], ob.shape[2]
    mlen, kvw = kv.shape[1:]
    hd = X_HEADS * X_DH
    tile = lambda w_: pl.BlockSpec((1, TOK_TILE, w_), lambda i, j: (i, j, 0))
    return pl.pallas_call(
        functools.partial(_out_xattn_body, heads=X_HEADS, dh=X_DH),
        grid=(b, s // TOK_TILE),
        in_specs=[tile(ka), tile(kb), _const_spec((ka, d)), _const_spec((kb, d)), tile(d),
                  _const_spec((1, d)), _const_spec((1, d)), _const_spec((1, d)), _const_spec((d, hd)),
                  pl.BlockSpec((1, mlen, kvw), lambda i, j: (i, 0, 0)),
                  _const_spec((hd, d))],
        out_specs=tile(d),
        out_shape=jax.ShapeDtypeStruct((b, s, d), F32),
        compiler_params=_params("arbitrary", "arbitrary"),
        name="out_xattn",
    )(oa, ob, wa.astype(BF16), wb.astype(BF16), x, g_mix.reshape(1, d), g_pre.reshape(1, d),
      g_post.reshape(1, d), wq.astype(BF16), kv, wo.astype(BF16))


def _bucket_np(dist):
    max_exact = N_BUCKETS // 2
    d = np.maximum(dist, 0)
    ratio = np.log(np.maximum(d, 1).astype(np.float32) / np.float32(max_exact)) / np.float32(
        math.log(MAX_DISTANCE / max_exact))
    large = np.minimum(max_exact + (ratio * np.float32(N_BUCKETS - max_exact)).astype(np.int32),
                       N_BUCKETS - 1)
    return np.where(d < max_exact, d, large).astype(np.int32)


def _rel_bias(tab, dist, visible, rel_to_far, head_axis=0):
    dist = np.maximum(np.asarray(dist), 0).astype(np.int32)
    bk = _bucket_np(np.arange(max(int(dist.max()), 4 * MAX_DISTANCE) + 1))
    assert (np.diff(bk) >= 0).all() and bk[-1] == N_BUCKETS - 1
    tab = tab.astype(F32)
    col = lambda b: tab[b].reshape((1,) * head_axis + (-1,) + (1,) * (dist.ndim - head_axis))
    d = jnp.expand_dims(jnp.asarray(dist), head_axis)
    out = jnp.broadcast_to(col(0), dist.shape[:head_axis] + (tab.shape[1],) + dist.shape[head_axis:])
    for b in range(1, N_BUCKETS):
        first = np.nonzero(bk == b)[0]
        if first.size:
            out = jnp.where(d >= int(first[0]), col(b), out)
    if rel_to_far:
        out = out - col(N_BUCKETS - 1)
    return jnp.where(jnp.expand_dims(jnp.asarray(np.asarray(visible)), head_axis), out * LOG2E, NEG_INF)


def _far_distance():
    bk = _bucket_np(np.arange(4 * MAX_DISTANCE))
    return int(np.nonzero(bk == N_BUCKETS - 1)[0][0])


def _bias_tiles(tab, t):
    assert t + 1 >= _far_distance()
    r = np.arange(t)[:, None]
    c = np.arange(t)[None, :]
    dist = np.stack([r - c, t + r - c, 2 * t + r - c])
    everywhere = np.ones((t, t), bool)
    return _rel_bias(tab, dist, np.stack([r >= c, everywhere, everywhere]), True)


def _near_tiles(tab, t):
    tiles = _bias_tiles(tab, t)
    hidden = jnp.full_like(tiles[:, 0], NEG_INF)
    return jnp.stack([jnp.concatenate([tiles[:, 1], tiles[:, 0]], axis=-1),
                      jnp.concatenate([tiles[:, 0], hidden], axis=-1)], axis=1)


def _key_loops(qt, tq, wide, wide_step, near_step, init):
    far = jnp.maximum(qt - 1, 0) * tq
    tw = wide * tq
    n_wide = (far + tw - 1) // tw

    def body(j, carry):
        start = pl.multiple_of(jnp.maximum(far - (n_wide - j) * tw, 0), tq)
        return wide_step(start, jnp.where(j == 0, far - (n_wide - 1) * tw, tw), carry)

    carry = lax.fori_loop(0, n_wide, body, init)
    return near_step(pl.multiple_of(far, tq), jnp.where(qt == 0, 1, 0), carry)


def _cmp_body(x_ref, pe_ref, w_ref, o_ref):
    x = x_ref[0]
    xn = pltpu.roll(x, x.shape[0] - 1, 0)
    lo = (x + pe_ref[0:1, :]).astype(BF16)
    hi = (xn + pe_ref[1:2, :]).astype(BF16)
    o_ref[0] = (_dot(lo, w_ref[0]) + _dot(hi, w_ref[1])).astype(BF16)


def _nsa_compress(kcvc, cmp_pe, cmp_w, b, s):
    nb = s // CMP_STRIDE
    halves = CMP_LEN // CMP_STRIDE
    n_grp = 2 * A_KV
    kdim = CMP_STRIDE * n_grp * A_DH
    cw = cmp_w.reshape(2, halves, CMP_STRIDE, A_DH, A_DH)
    same = jnp.eye(n_grp, dtype=F32).reshape(2, A_KV, 2, A_KV)
    w = jnp.einsum("khlde,kgKG->hlkgdKGe", cw, same)
    w = jnp.pad(w, [(0, 0)] * 7 + [(0, LANES - A_DH)]).reshape(halves, kdim, n_grp * LANES).astype(BF16)
    pe = cmp_pe.reshape(2, halves, CMP_STRIDE, 1, A_DH).transpose(1, 2, 0, 3, 4)
    pe = jnp.broadcast_to(pe, (halves, CMP_STRIDE, 2, A_KV, A_DH)).reshape(halves, kdim)
    return pl.pallas_call(
        _cmp_body,
        grid=(b,),
        in_specs=[pl.BlockSpec((1, nb, kdim), lambda i: (i, 0, 0)),
                  _const_spec((halves, kdim)), _const_spec((halves, kdim, n_grp * LANES))],
        out_specs=pl.BlockSpec((1, nb, n_grp * LANES), lambda i: (i, 0, 0)),
        out_shape=jax.ShapeDtypeStruct((b, nb, n_grp * LANES), BF16),
        compiler_params=_params("arbitrary"),
        name="nsa_compress",
    )(kcvc.reshape(b, nb, kdim), pe, w)


def _nsat_body(qt_ref, kc_ref, vct_ref, bc_ref, ks_ref, vst_ref, kw_ref, vwt_ref, ga_ref, near_ref,
               ovl_ref, exp_ref, o_ref, mask_ref, *, tq, seq, n_sb):
    qi = pl.program_id(2)
    rr = A_GRP
    tw = 2 * tq
    q_all = jnp.concatenate([qt_ref[0, r * A_DH:(r + 1) * A_DH, :] for r in range(rr)], axis=1)
    head = lambda a, r: a[:, r * tq:(r + 1) * tq]
    heads = lambda a: jnp.concatenate([a] * rr, axis=1)

    s_c = _dot(kc_ref[0, :, :A_DH], q_all) + bc_ref[0, 0]
    p = jnp.where(s_c > 0.5 * NEG_INF, jnp.exp2(s_c - jnp.max(s_c, axis=0, keepdims=True)), 0.0)
    l = jnp.sum(p, axis=0, keepdims=True)
    pn = p * jnp.where(l > 0.0, 1.0 / l, 0.0)
    o_c = _dot(vct_ref[0, 0], pn.astype(BF16))

    imp = head(pn, 0)
    for r in range(1, rr):
        imp = imp + head(pn, r)
    imp_t = jnp.dot(ovl_ref[...], imp, precision=lax.Precision.HIGHEST, preferred_element_type=F32)
    blk = lax.broadcasted_iota(I32, (LANES, tq), 0)
    cur = (qi * tq + lax.broadcasted_iota(I32, (LANES, tq), 1)) // SLC_LEN
    forced = (blk == 0) | (blk == cur) | (blk == cur - 1)
    score = jnp.where(blk <= cur, imp_t + jnp.where(forced, FORCE_BONUS, 0.0), NEG_INF)
    groups = [score[8 * v:8 * v + 8] for v in range(-(-n_sb // 8))]
    ranks = [jnp.zeros((8, tq), F32) for _ in groups]
    sub = lax.broadcasted_iota(I32, (8, tq), 0)
    for i in range(n_sb):
        ci = score[i:i + 1, :]
        for v, sv in enumerate(groups):
            ge = jnp.where(ci >= sv, 1.0, 0.0)
            gt = jnp.where(ci > sv, 1.0, 0.0)
            if 8 * v > i:
                ranks[v] = ranks[v] + ge
            elif 8 * v + 7 < i:
                ranks[v] = ranks[v] + gt
            else:
                ranks[v] = ranks[v] + jnp.where(sub + 8 * v > i, ge, gt)
    rank = jnp.concatenate(ranks + [jnp.full((LANES - 8 * len(groups), tq), float(LANES), F32)], axis=0)
    sel = jnp.where(rank < float(min(SLC_TOP, n_sb)), jnp.where(blk <= cur, 1.0, 0.0), 0.0).astype(BF16)

    def expand(c, _):
        cs = pl.multiple_of(c * tw, tw)
        mask_ref[pl.ds(cs, tw), :] = (_dot(exp_ref[pl.ds(cs, tw), :], sel) - 1.0) * (-NEG_INF)
        return 0

    lax.fori_loop(0, qi // 2 + 1, expand, 0)

    def step(k_ref_, vt_ref_, ks, width, add, carry):
        m, acc = carry
        s = _dot(k_ref_[0, pl.ds(ks, width), :A_DH], q_all) + add
        m_new = jnp.maximum(m, jnp.max(s, axis=0, keepdims=True))
        p = jnp.exp2(s - m_new)
        return m_new, jnp.exp2(m - m_new) * acc + _dot(vt_ref_[0, 0, :, pl.ds(ks, width)], p.astype(BF16))

    init = jnp.full((1, rr * tq), -jnp.inf, F32), jnp.zeros((vst_ref.shape[2], rr * tq), F32)
    key_i = lax.broadcasted_iota(I32, (tw, tq), 0)

    def slc_wide(ks, limit, carry):
        own = jnp.where(key_i < limit, mask_ref[pl.ds(ks, tw), :], NEG_INF)
        return step(ks_ref, vst_ref, ks, tw, heads(own), carry)

    def slc_near(ks, which, carry):
        return step(ks_ref, vst_ref, ks, tw, heads(mask_ref[pl.ds(ks, tw), :]) + near_ref[0, which], carry)

    _, acc_s = _key_loops(qi, tq, 2, slc_wide, slc_near, init)

    def win_oldest():
        ks = pl.multiple_of((qi - WIN // tq) * tq, tq)
        kq = lax.broadcasted_iota(I32, (tq, tq), 0)
        qq = lax.broadcasted_iota(I32, (tq, tq), 1)
        return step(kw_ref, vwt_ref, ks, tq, heads(jnp.where(qq < kq, 0.0, NEG_INF)), init)

    carry_w = lax.cond(qi >= WIN // tq, win_oldest, lambda: init)
    far = pl.multiple_of(jnp.maximum(qi - 1, 0) * tq, tq)
    _, acc_w = step(kw_ref, vwt_ref, far, tw, near_ref[0, jnp.where(qi == 0, 1, 0)], carry_w)

    gs = _sigmoid(ga_ref[0, 0])
    for r in range(rr):
        a_s, a_w = head(acc_s, r), head(acc_w, r)
        o = (gs[3 * r:3 * r + 1] * head(o_c, r)[:A_DH]
             + gs[3 * r + 1:3 * r + 2] * (a_s[:A_DH] / a_s[A_DH:A_DH + 1])
             + gs[3 * r + 2:3 * r + 3] * (a_w[:A_DH] / a_w[A_DH:A_DH + 1]))
        o_ref[0, r * A_DH:(r + 1) * A_DH, :] = o.astype(BF16)


def _nsat(q, kvc, kvsw, ga, tab, b, s):
    tq = LANE_TQ
    nq = s // tq
    ncp = s // CMP_STRIDE
    n_sb = s // SLC_LEN
    n_vt = A_DH + 16
    assert n_sb <= LANES and WIN == 2 * tq and s % (2 * tq) == 0
    qt = jnp.swapaxes(q, 1, 2)
    vct = jnp.swapaxes(kvc.reshape(b, ncp, 2 * A_KV, LANES)[:, :, A_KV:, :n_vt], 1, 3).swapaxes(1, 2)
    vt = jnp.swapaxes(kvsw.reshape(b, s, 4 * A_KV, LANES)[..., :n_vt], 1, 3).swapaxes(1, 2)
    gat = jnp.swapaxes(ga.reshape(b, s, A_KV, LANES)[..., :16], 1, 3).swapaxes(1, 2)
    t = (np.arange(nq)[:, None, None] * tq + np.arange(tq)[None, None, :])
    cmp_end = (np.arange(ncp) * CMP_STRIDE + CMP_LEN - 1)[None, :, None]
    bias_c = jnp.stack([_rel_bias(tab[:, g * A_GRP:(g + 1) * A_GRP], t - cmp_end, cmp_end <= t, False,
                                  head_axis=2).reshape(nq, ncp, A_GRP * tq) for g in range(A_KV)])
    near = _near_tiles(tab, tq).reshape(A_KV, A_GRP, 2, tq, 2 * tq).transpose(0, 2, 4, 1, 3)
    near = near.reshape(A_KV, 2, 2 * tq, A_GRP * tq)
    cmp_start = np.arange(ncp)[None, :] * CMP_STRIDE
    sb_start = np.arange(LANES)[:, None] * SLC_LEN
    ovl = ((cmp_start < sb_start + SLC_LEN) & (cmp_start + CMP_LEN - 1 >= sb_start)
           & (np.arange(ncp)[None, :] < ncp - 1) & (np.arange(LANES)[:, None] < n_sb))
    ovl = jnp.asarray(ovl.astype(np.float32))
    expand = jnp.asarray((np.arange(s)[:, None] // SLC_LEN == np.arange(LANES)[None, :]).astype(np.float32), BF16)
    k_spec = lambda blk: pl.BlockSpec((1, s, LANES), lambda i, g, j: (i, 0, 2 * blk + g))
    vt_spec = lambda blk: pl.BlockSpec((1, 1, n_vt, s), lambda i, g, j: (i, 2 * blk + g, 0, 0))
    o_t = pl.pallas_call(
        functools.partial(_nsat_body, tq=tq, seq=s, n_sb=n_sb),
        grid=(b, A_KV, nq),
        in_specs=[pl.BlockSpec((1, A_GRP * A_DH, tq), lambda i, g, j: (i, g, j)),
                  pl.BlockSpec((1, ncp, LANES), lambda i, g, j: (i, 0, g)),
                  pl.BlockSpec((1, 1, n_vt, ncp), lambda i, g, j: (i, g, 0, 0)),
                  pl.BlockSpec((1, 1, ncp, A_GRP * tq), lambda i, g, j: (g, j, 0, 0)),
                  k_spec(0), vt_spec(1), k_spec(2), vt_spec(3),
                  pl.BlockSpec((1, 1, 16, tq), lambda i, g, j: (i, g, 0, j)),
                  pl.BlockSpec((1, 2, 2 * tq, A_GRP * tq), lambda i, g, j: (g, 0, 0, 0)),
                  _const_spec((LANES, ncp)), _const_spec((s, LANES))],
        out_specs=pl.BlockSpec((1, A_GRP * A_DH, tq), lambda i, g, j: (i, g, j)),
        out_shape=jax.ShapeDtypeStruct((b, A_HEADS * A_DH, s), BF16),
        scratch_shapes=[pltpu.VMEM((s, tq), F32)],
        compiler_params=_params("arbitrary", "arbitrary", "arbitrary"),
        name="nsa_attn",
    )(qt, kvc, vct, bias_c, kvsw, vt, kvsw, vt, gat, near, ovl, expand)
    return jnp.swapaxes(o_t, 1, 2)


def _dsat_body(qt_ref, k_ref, vt_ref, iqt_ref, ik_ref, iwt_ref, bias_ref, o_ref, keys_ref, *,
               tq, seq, k_top):
    qi = pl.program_id(1)
    nh = B_HEADS
    tw = 2 * tq
    n_wide = qi // 2 + 1
    key_i = lax.broadcasted_iota(I32, (tw, tq), 0)
    qry_i = lax.broadcasted_iota(I32, (tw, tq), 1)

    iwt = iwt_ref[0]

    def scores(kw, _):
        ks = pl.multiple_of(kw * tw, tw)
        ik = ik_ref[0, pl.ds(ks, tw), :IDX_DH]
        sc = jnp.zeros((tw, tq), F32)
        for h in range(IDX_HEADS):
            lg = _dot(ik, iqt_ref[0, h * IDX_DH:(h + 1) * IDX_DH, :])
            sc = sc + jnp.maximum(lg, 0.0) * iwt[h:h + 1, :]
        bits = lax.bitcast_convert_type(sc, I32)
        key = jnp.where(bits < 0, bits ^ 0x7FFFFFFF, bits)
        vis = (key_i + ks) <= (qry_i + qi * tq)
        keys_ref[pl.ds(ks, tw), :] = jnp.where(vis, key, INT_MIN)
        return 0

    lax.fori_loop(0, n_wide, scores, 0)

    def count(ind):
        def body(kw, acc):
            ks = pl.multiple_of(kw * tw, tw)
            x = ind(keys_ref[pl.ds(ks, tw), :], kw)
            return acc + jnp.sum(x.reshape(tw // ACC_ROWS, ACC_ROWS, tq), axis=0)
        acc = lax.fori_loop(0, n_wide, body, jnp.zeros((ACC_ROWS, tq), F32))
        return jnp.sum(acc, axis=0, keepdims=True)

    kf = float(k_top)
    c0 = count(lambda t, kw: jnp.where(t >= 0, 1.0, 0.0))
    thr = jnp.where(c0 >= kf, 0, INT_MIN).astype(I32)
    c_ge = jnp.where(c0 >= kf, c0, float(seq))

    def bisect(i, state):
        thr, c_ge = state
        cand = thr | (jnp.int32(1) << (30 - i))
        c = count(lambda t, kw: jnp.where(t >= cand, 1.0, 0.0))
        return jnp.where(c >= kf, cand, thr), jnp.where(c >= kf, c, c_ge)

    thr, c_ge = lax.fori_loop(0, 31, bisect, (thr, c_ge))

    n_bits = int(math.log2(seq))
    assert 2 ** n_bits == seq

    def cutoff():
        need = kf - count(lambda t, kw: jnp.where(t > thr, 1.0, 0.0))

        def step(i, cut):
            cand = cut | (jnp.int32(1) << (n_bits - 1 - i))
            c = count(lambda t, kw: jnp.where(t == thr, jnp.where(key_i + kw * tw < cand, 1.0, 0.0), 0.0))
            return jnp.where(c < need, cand, cut)
        return lax.fori_loop(0, n_bits, step, jnp.zeros((1, tq), I32))

    excess = jnp.max(jnp.where(c_ge > kf, 1.0, 0.0)) > 0.5
    cut = lax.cond(excess, cutoff, lambda: jnp.full((1, tq), seq, I32))

    q_all = jnp.concatenate([qt_ref[0, h * B_DH:(h + 1) * B_DH, :] for h in range(nh)], axis=1)

    def step(ks, own, bias, carry):
        m, acc = carry
        key = keys_ref[pl.ds(ks, tw), :]
        tie = jnp.where(key_i + ks <= cut, 0.0, NEG_INF)
        sel = jnp.where(key > thr, 0.0, jnp.where(key == thr, tie, NEG_INF))
        if own is not None:
            sel = jnp.where(own, sel, NEG_INF)
        s = _dot(k_ref[0, pl.ds(ks, tw), :B_DH], q_all) + jnp.concatenate([sel] * nh, axis=1)
        if bias is not None:
            s = s + bias
        m_new = jnp.maximum(m, jnp.max(s, axis=0, keepdims=True))
        p = jnp.exp2(s - m_new)
        return m_new, jnp.exp2(m - m_new) * acc + _dot(vt_ref[0, :, pl.ds(ks, tw)], p.astype(BF16))

    def wide_step(ks, limit, carry):
        return step(ks, key_i < limit, None, carry)

    def near_step(ks, which, carry):
        return step(ks, None, bias_ref[which], carry)

    init = jnp.full((1, nh * tq), -jnp.inf, F32), jnp.zeros((vt_ref.shape[1], nh * tq), F32)
    _, acc = _key_loops(qi, tq, 2, wide_step, near_step, init)
    for h in range(nh):
        a = acc[:, h * tq:(h + 1) * tq]
        o_ref[0, h * B_DH:(h + 1) * B_DH, :] = (a[:B_DH] / a[B_DH:B_DH + 1]).astype(BF16)


def _dsat(q, kbvb, iq, ik, iw, tab, b, s):
    tq = LANE_TQ
    k_top = min(IDX_TOPK_MAX, s // 4)
    assert s % (2 * tq) == 0
    qt = jnp.swapaxes(q, 1, 2)
    iqt = jnp.swapaxes(iq, 1, 2)
    iwt = jnp.swapaxes(iw[:, :, :IDX_HEADS], 1, 2)
    n_vt = B_DH + 16
    vt = jnp.swapaxes(kbvb[:, :, LANES:LANES + n_vt], 1, 2)
    near = _near_tiles(tab, tq).transpose(1, 3, 0, 2).reshape(2, 2 * tq, B_HEADS * tq)
    o_t = pl.pallas_call(
        functools.partial(_dsat_body, tq=tq, seq=s, k_top=k_top),
        grid=(b, s // tq),
        in_specs=[pl.BlockSpec((1, B_HEADS * B_DH, tq), lambda i, j: (i, 0, j)),
                  pl.BlockSpec((1, s, LANES), lambda i, j: (i, 0, 0)),
                  pl.BlockSpec((1, n_vt, s), lambda i, j: (i, 0, 0)),
                  pl.BlockSpec((1, IDX_HEADS * IDX_DH, tq), lambda i, j: (i, 0, j)),
                  pl.BlockSpec((1, s, LANES), lambda i, j: (i, 0, 0)),
                  pl.BlockSpec((1, IDX_HEADS, tq), lambda i, j: (i, 0, j)),
                  _const_spec((2, 2 * tq, B_HEADS * tq))],
        out_specs=pl.BlockSpec((1, B_HEADS * B_DH, tq), lambda i, j: (i, 0, j)),
        out_shape=jax.ShapeDtypeStruct((b, B_HEADS * B_DH, s), BF16),
        scratch_shapes=[pltpu.VMEM((s, tq), I32)],
        compiler_params=_params("arbitrary", "arbitrary"),
        name="dsa_attn",
    )(qt, kbvb, vt, iqt, ik, iwt, near)
    return jnp.swapaxes(o_t, 1, 2)


def _mla_prep_body(cq_ref, ckv_ref, kr_ref, qn_ref, kvn_ref, wq1_ref, wq2_ref, wk_ref, wv_ref,
                   cos_ref, sin_ref, ones_ref, q_o, k_o, v_o):
    cqn = _rms(cq_ref[...], qn_ref[...]).astype(BF16)
    ckn = _rms(ckv_ref[...], kvn_ref[...]).astype(BF16)
    cos, sin = cos_ref[...], sin_ref[...]
    cos_h = jnp.concatenate([cos] * C_HEADS, axis=1)
    sin_h = jnp.concatenate([sin] * C_HEADS, axis=1)
    q_o[...] = (_dot(cqn, wq1_ref[...]) * cos_h + _dot(cqn, wq2_ref[...]) * sin_h).astype(BF16)
    kr = kr_ref[...]
    kr_rot = kr[:, :LANES] * cos + kr[:, LANES:] * sin
    k_o[...] = (_dot(ckn, wk_ref[...]) + jnp.concatenate([kr_rot] * C_HEADS, axis=1)).astype(BF16)
    v_o[...] = (_dot(ckn, wv_ref[...]) + ones_ref[...]).astype(BF16)


def _rot_cols(w):
    half = C_DR // 2
    return jnp.concatenate([-w[..., half:], w[..., :half]], axis=-1)


def _mla_prep(cq, ckv, kr, q_norm, kv_norm, w_uq, w_ukv, s):
    n = cq.shape[0]
    hw = C_HEADS * LANES
    padq = LANES - C_DN - C_DR
    wq = w_uq.reshape(C_Q_RANK, C_HEADS, C_DN + C_DR) * ((C_DN + C_DR) ** -0.5 * LOG2E)
    wq1 = jnp.pad(wq, ((0, 0), (0, 0), (0, padq))).reshape(C_Q_RANK, hw)
    wq2 = jnp.pad(_rot_cols(wq[..., C_DN:]), ((0, 0), (0, 0), (C_DN, padq))).reshape(C_Q_RANK, hw)
    wkv = w_ukv.reshape(C_KV_RANK, C_HEADS, C_DN + C_DV)
    wk = jnp.pad(wkv[..., :C_DN], ((0, 0), (0, 0), (0, LANES - C_DN))).reshape(C_KV_RANK, hw)
    wv = jnp.pad(wkv[..., C_DN:], ((0, 0), (0, 0), (0, LANES - C_DV))).reshape(C_KV_RANK, hw)
    half = C_DR // 2
    inv = ROPE_THETA ** (-jnp.arange(half, dtype=F32) / half)
    ang = jnp.arange(s, dtype=F32)[:, None] * inv[None, :]
    cos, sin = jnp.cos(ang), jnp.sin(ang)
    cos_t = jnp.concatenate([jnp.ones((s, C_DN), F32), cos, cos, jnp.zeros((s, padq), F32)], axis=1)
    sin_t = jnp.concatenate([jnp.zeros((s, C_DN), F32), sin, sin, jnp.zeros((s, padq), F32)], axis=1)
    ones = jnp.tile(jnp.asarray(np.arange(LANES) == ONES_LANE, F32), C_HEADS).reshape(1, hw)
    tile_tok = lambda w_: pl.BlockSpec((TOK_TILE, w_), lambda i: (i, 0))
    n_pos = s // TOK_TILE
    pos_spec = pl.BlockSpec((TOK_TILE, LANES), lambda i: (i % n_pos, 0))
    return pl.pallas_call(
        _mla_prep_body,
        grid=(n // TOK_TILE,),
        in_specs=[tile_tok(C_Q_RANK), tile_tok(C_KV_RANK), tile_tok(2 * LANES),
                  _const_spec((1, C_Q_RANK)), _const_spec((1, C_KV_RANK)),
                  _const_spec((C_Q_RANK, hw)), _const_spec((C_Q_RANK, hw)),
                  _const_spec((C_KV_RANK, hw)), _const_spec((C_KV_RANK, hw)),
                  pos_spec, pos_spec, _const_spec((1, hw))],
        out_specs=[tile_tok(hw)] * 3,
        out_shape=[jax.ShapeDtypeStruct((n, hw), BF16)] * 3,
        compiler_params=_params("arbitrary"),
        name="mla_prep",
    )(cq, ckv, kr, q_norm.reshape(1, -1), kv_norm.reshape(1, -1), wq1.astype(BF16), wq2.astype(BF16),
      wk.astype(BF16), wv.astype(BF16), cos_t, sin_t, ones)


def _mla_attn_body(q_ref, k_ref, v_ref, o_ref, *, tq, nh):
    qt = pl.program_id(2)
    q = q_ref[0]
    lanes = lambda a, h: a[:, h * LANES:(h + 1) * LANES]

    def step(kt, carry, add):
        ks = pl.multiple_of(kt * tq, tq)
        k = k_ref[0, pl.ds(ks, tq), :]
        v = v_ref[0, pl.ds(ks, tq), :]
        return tuple(_online_step(lanes(q, h), lanes(k, h), lanes(v, h), carry[h], add=add)
                     for h in range(nh))

    init = tuple(_flash_init(tq, LANES) for _ in range(nh))
    carry = lax.fori_loop(0, qt, lambda kt, c: step(kt, c, None), init)
    row = lax.broadcasted_iota(I32, (tq, tq), 0)
    col = lax.broadcasted_iota(I32, (tq, tq), 1)
    carry = step(qt, carry, jnp.where(row >= col, 0.0, NEG_INF))
    for h in range(nh):
        acc = carry[h][1]
        o_ref[0, :, h * LANES:(h + 1) * LANES] = (acc / acc[:, ONES_LANE:ONES_LANE + 1]).astype(BF16)


def _mla_attn(q, k, v, b, s):
    tq = DENSE_TQ
    nh = MLA_HEADS_PER_STEP
    spec_q = pl.BlockSpec((1, tq, nh * LANES), lambda i, h, j: (i, j, h))
    spec_kv = pl.BlockSpec((1, s, nh * LANES), lambda i, h, j: (i, 0, h))
    return pl.pallas_call(
        functools.partial(_mla_attn_body, tq=tq, nh=nh),
        grid=(b, C_HEADS // nh, s // tq),
        in_specs=[spec_q, spec_kv, spec_kv],
        out_specs=spec_q,
        out_shape=jax.ShapeDtypeStruct((b, s, C_HEADS * LANES), BF16),
        compiler_params=_params("arbitrary", "arbitrary", "arbitrary"),
        name="mla_attn",
    )(q, k, v)


def _diff_body(lam_ref, q_ref, k_ref, v_ref, bias_ref, g_ref, o_ref, *, tq, nh, out_scale):
    qt = pl.program_id(2)
    q = q_ref[0]
    lanes = lambda a, c, w=LANES: a[:, c * w:(c + 1) * w]

    def step(kt, carry, near):
        ks = pl.multiple_of(kt * tq, tq)
        k = k_ref[0, pl.ds(ks, tq), :]
        v = v_ref[0, pl.ds(ks, tq), :]
        return tuple(_online_step(lanes(q, c), lanes(k, c), lanes(v, c // 2, 2 * LANES), carry[c],
                                  add=bias_ref[c // 2, qt - kt] if near else None)
                     for c in range(2 * nh))

    n_far = jnp.maximum(qt - 1, 0)
    init = tuple(_flash_init(tq, 2 * LANES) for _ in range(2 * nh))
    carry = lax.fori_loop(0, n_far, functools.partial(step, near=False), init)
    carry = lax.fori_loop(n_far, qt + 1, functools.partial(step, near=True), carry)
    outs = [carry[c][1][:, :D_DV] / carry[c][1][:, D_DV:D_DV + 1] for c in range(2 * nh)]
    for h in range(nh):
        o = outs[2 * h] - lam_ref[0] * outs[2 * h + 1]
        o_ref[0, :, h * D_DV:(h + 1) * D_DV] = (_rms(o, g_ref[...]) * out_scale).astype(BF16)


def _diff_attn(q, k, v, tab, lam_full, subln, lam_init, b, s):
    tq = DENSE_TQ
    nh = DIFF_HEADS_PER_STEP
    half = _bias_tiles(tab, tq // 2)
    t0, t1, zero = half[:, 0], half[:, 1], half[:, 2]
    hidden = jnp.full_like(zero, NEG_INF)
    block = lambda a, b_, c, d: jnp.concatenate([jnp.concatenate([a, b_], axis=-1),
                                                 jnp.concatenate([c, d], axis=-1)], axis=-2)
    tiles = jnp.stack([block(t0, hidden, t1, t0), block(zero, t1, zero, zero)], axis=1)
    spec_q = pl.BlockSpec((1, tq, nh * 2 * LANES), lambda i, h, j: (i, j, h))
    spec_kv = pl.BlockSpec((1, s, nh * 2 * LANES), lambda i, h, j: (i, 0, h))
    return pl.pallas_call(
        functools.partial(_diff_body, tq=tq, nh=nh, out_scale=1.0 - lam_init),
        grid=(b, D_HEADS // nh, s // tq),
        in_specs=[pl.BlockSpec(memory_space=pltpu.SMEM),
                  spec_q, spec_kv, spec_kv,
                  pl.BlockSpec((nh, 2, tq, tq), lambda i, h, j: (h, 0, 0, 0)),
                  _const_spec((1, D_DV))],
        out_specs=pl.BlockSpec((1, tq, nh * D_DV), lambda i, h, j: (i, j, h)),
        out_shape=jax.ShapeDtypeStruct((b, s, D_HEADS * D_DV), BF16),
        compiler_params=_params("arbitrary", "arbitrary", "arbitrary"),
        name="diff_attn",
    )(lam_full.reshape(1), q, k, v, tiles, subln.reshape(1, D_DV))


def _pad_blocks(w, n, width, to=LANES, at=0):
    k = w.shape[0]
    return jnp.pad(w.reshape(k, n, width), ((0, 0), (0, 0), (at, to - width - at))).reshape(k, n * to)


def _ones_row(n_blocks, which, to=LANES, lane=ONES_LANE):
    row = np.zeros((n_blocks, to), np.float32)
    row[list(which), lane] = 1.0
    return jnp.asarray(row.reshape(1, n_blocks * to))


def _split(w, sizes):
    return jnp.split(w, [int(c) for c in np.cumsum(sizes)[:-1]], axis=-1)


def _pad_rows(w, n, width):
    d = w.shape[1]
    return jnp.pad(w.reshape(n, width, d), ((0, 0), (0, LANES - width), (0, 0))).reshape(n * LANES, d)


def _even_mixer(x2, g_pre, w_in, w_out, cmp_pe, cmp_w, rel_tab, b, s):
    qa, kva, ga, qb, kb, vb, iq, ik, iw = _split(w_in, EV_SIZES)
    half = 2 * A_KV * A_DH
    w_segs = [qa * (A_DH ** -0.5 * LOG2E), kva[:, :half],
              _pad_blocks(kva[:, half:], 4 * A_KV, A_DH), _pad_blocks(ga, A_KV, 3 * A_GRP),
              qb * (B_DH ** -0.5 * LOG2E),
              _pad_blocks(jnp.concatenate([kb, vb], axis=1), 2, B_DH),
              iq, _pad_blocks(ik, 1, IDX_DH), _pad_blocks(iw, 1, IDX_HEADS)]
    zeros = lambda w: jnp.zeros((1, w.shape[1]), F32)
    c_segs = [zeros(w) for w in w_segs]
    c_segs[2] = _ones_row(4 * A_KV, (2, 3, 6, 7))
    c_segs[5] = _ones_row(2, (1,))
    dts = [BF16, F32, BF16, F32, BF16, BF16, BF16, BF16, F32]
    q_a, kcvc, kvsw, g_a, q_b, kbvb, i_q, i_k, i_w = _proj(x2, g_pre, w_segs, c_segs, dts)
    r3 = lambda a: a.reshape(b, s, a.shape[-1])
    kvc = _nsa_compress(kcvc, cmp_pe, cmp_w, b, s)
    o_a = _nsat(r3(q_a), kvc, r3(kvsw), r3(g_a), rel_tab[:, :A_HEADS], b, s)
    o_b = _dsat(r3(q_b), r3(kbvb), r3(i_q), r3(i_k), r3(i_w), rel_tab[:, A_HEADS:A_HEADS + B_HEADS], b, s)
    na = A_HEADS * A_DH
    return o_a, o_b, w_out[:na], w_out[na:]


def _odd_mixer(x2, g_pre, w_in, w_out, q_norm, kv_norm, w_uq, w_ukv, lam, subln, rel_tab,
               lam_init, b, s):
    c_q, c_kv, k_rope, qd, kd, vd = _split(w_in, OD_SIZES)
    w_kr = jnp.concatenate([_pad_blocks(k_rope, 1, C_DR, at=C_DN),
                            _pad_blocks(_rot_cols(k_rope), 1, C_DR, at=C_DN)], axis=1)
    w_segs = [c_q, c_kv, w_kr, _pad_blocks(qd * (D_DH ** -0.5 * LOG2E), 2 * D_HEADS, D_DH),
              _pad_blocks(kd, 2 * D_HEADS, D_DH), _pad_blocks(vd, D_HEADS, D_DV, to=2 * LANES)]
    c_segs = [jnp.zeros((1, w.shape[1]), F32) for w in w_segs]
    c_segs[5] = _ones_row(D_HEADS, range(D_HEADS), to=2 * LANES, lane=D_DV)
    dts = [F32, F32, F32, BF16, BF16, BF16]
    cq, ckv, kr, q_d, k_d, v_d = _proj(x2, g_pre, w_segs, c_segs, dts)
    q_m, k_m, v_m = _mla_prep(cq, ckv, kr, q_norm, kv_norm, w_uq, w_ukv, s)
    r3 = lambda a: a.reshape(b, s, a.shape[-1])
    o_c = _mla_attn(r3(q_m), r3(k_m), r3(v_m), b, s)
    lam32 = lam.astype(F32)
    lam_full = jnp.exp(jnp.sum(lam32[0] * lam32[1])) - jnp.exp(jnp.sum(lam32[2] * lam32[3])) + lam_init
    o_d = _diff_attn(r3(q_d), r3(k_d), r3(v_d), rel_tab[:, A_HEADS + B_HEADS:], lam_full, subln,
                     lam_init, b, s)
    nc = C_HEADS * C_DV
    wc = _pad_rows(w_out[:nc], C_HEADS, C_DV)
    return o_c, o_d, wc, w_out[nc:]


def kernel(x, mem, rel_bias_table, norm_pre, norm_post, mem_norm, ffn_wg, ffn_wu, ffn_wd, ev_w_in,
           ev_cmp_pe, ev_cmp_w, ev_w_out, od_w_in, od_q_norm, od_kv_norm, od_w_uq, od_w_ukv, od_lambda,
           od_subln, od_w_out, xa_wq, xa_wkv, xa_wo):
    b, s, d = x.shape
    depth = norm_pre.shape[0]
    x2 = x.reshape(b * s, d)
    mem2 = mem.reshape(-1, d)
    for i in range(depth):
        g_pre, g_post = norm_pre[i], norm_post[i]
        x2 = _ffn(x2, g_pre[0], g_post[0], ffn_wg[i, 0], ffn_wu[i, 0], ffn_wd[i, 0])
        if i % 2 == 0:
            e = i // 2
            mixed = _even_mixer(x2, g_pre[1], ev_w_in[e], ev_w_out[e], ev_cmp_pe[e], ev_cmp_w[e],
                                rel_bias_table, b, s)
        else:
            o = i // 2
            lam_init = 0.8 - 0.6 * math.exp(-0.3 * i)
            mixed = _odd_mixer(x2, g_pre[1], od_w_in[o], od_w_out[o], od_q_norm[o], od_kv_norm[o],
                               od_w_uq[o], od_w_ukv[o], od_lambda[o], od_subln[o], rel_bias_table, lam_init, b, s)
        (kv,) = _proj(mem2, mem_norm[i], [xa_wkv[i]], [jnp.zeros((1, xa_wkv.shape[-1]), F32)], [BF16])
        x3 = _out_xattn(*mixed, x2.reshape(b, s, d), g_post[1], kv.reshape(b, -1, kv.shape[-1]),
                        g_pre[2], g_post[2], xa_wq[i], xa_wo[i])
        x2 = _ffn(x3.reshape(b * s, d), g_pre[3], g_post[3], ffn_wg[i, 1], ffn_wu[i, 1], ffn_wd[i, 1])
    return x2.reshape(b, s, d)
```
